```python
import jax, jax.numpy as jnp
from jax import lax
import numpy as np


D_MODEL = 1024
BATCH = 16
SEQ = 256
DEPTH = 1
DEC_BATCH = 4
DEC_SEQ = 1024
PAST_LEN = 512

GRID_W = 64
D_MIX = D_MODEL
CONV_W = D_MIX // 2
CONV_K = 31
CONV_PAD = (CONV_K - 1) // 2
RWKV_W = D_MIX - CONV_W
HEAD_SIZE = 64
N_RWKV_HEADS = RWKV_W // HEAD_SIZE
DECAY_LORA = 64
AAA_LORA = 64
GATE_LORA = 128
N_DIR = 2
N_EXPERTS = 16
EC_CAPACITY = 2
D_FF = 2816
D_IN = 2 * CONV_W + 3 * RWKV_W + N_DIR * DECAY_LORA + N_DIR * AAA_LORA + GATE_LORA
SPLIT_IDX = (CONV_W, 2 * CONV_W, 2 * CONV_W + RWKV_W, 2 * CONV_W + 2 * RWKV_W, 2 * CONV_W + 3 * RWKV_W,
             2 * CONV_W + 3 * RWKV_W + N_DIR * DECAY_LORA,
             2 * CONV_W + 3 * RWKV_W + N_DIR * (DECAY_LORA + AAA_LORA))
RMS_EPS = 1e-6
LN_EPS = 1e-5
GN_EPS = 64e-5

kernel_name = 'hymba_conformer_rwkv7_ec_diffusion_step'


def _rmsnorm(x, g):
    xf = x.astype(jnp.float32)
    return (xf * lax.rsqrt(jnp.mean(xf * xf, -1, keepdims=True) + RMS_EPS) * g).astype(x.dtype)


def _heads(t):
    return t.reshape(*t.shape[:-1], N_RWKV_HEADS, HEAD_SIZE)


def _time_major(t):
    t = jnp.transpose(t, (1, 2, 0, 3, 4))
    return jnp.stack([t[:, 0], t[::-1, 1]], axis=1)


def _conv_group(u_val, u_gate, grid, dw, db, ln_g, ln_b):
    u = u_val * jax.nn.sigmoid(u_gate)
    B, T, C = u.shape
    seq = u.reshape(B * (T // GRID_W), GRID_W, C) if grid else u
    out = lax.conv_general_dilated(seq, dw[:, None, :].astype(u.dtype), (1,), [(CONV_PAD, CONV_PAD)],
                                   dimension_numbers=('NWC', 'WIO', 'NWC'), feature_group_count=C)
    out = out.reshape(B, T, C).astype(jnp.float32) + db
    mu = jnp.mean(out, -1, keepdims=True)
    var = jnp.mean(jnp.square(out - mu), -1, keepdims=True)
    out = (out - mu) * lax.rsqrt(var + LN_EPS) * ln_g + ln_b
    return jax.nn.silu(out).astype(u.dtype)


def _wkv_step(S, inp):
    r_t, kk_t, w_t, b_t, k_t, v_t = inp
    sa = jnp.einsum('dbhvk,dbhk->dbhv', S, kk_t)
    S = S * w_t[..., None, :] - sa[..., :, None] * b_t[..., None, :] + v_t[..., :, None] * k_t[..., None, :]
    return S, jnp.einsum('dbhvk,dbhk->dbhv', S, r_t)


def _rwkv_group(r, k, v, xw, xa, xg, S0, w0, w_up, a0, a_up, g_up, k_k, k_a, r_k, gn_g, gn_b):
    f32 = jnp.float32
    B, T, _ = r.shape
    r, k, v, xw, xa, xg = (t.astype(f32) for t in (r, k, v, xw, xa, xg))
    xw = xw.reshape(B, T, N_DIR, DECAY_LORA)
    xa = xa.reshape(B, T, N_DIR, AAA_LORA)
    wl = w0 + jnp.einsum('btdl,dlc->btdc', jnp.tanh(xw), w_up)
    decay = jnp.exp(-jnp.exp(-jax.nn.softplus(-wl) - 0.5))
    a = jax.nn.sigmoid(a0 + jnp.einsum('btdl,dlc->btdc', xa, a_up))
    g = jax.nn.sigmoid(xg) @ g_up
    kk = _heads(k * k_k)
    kk = kk * lax.rsqrt(jnp.maximum(jnp.sum(kk * kk, -1, keepdims=True), 1e-24))
    a_h = _heads(a)
    k_dir = _heads(k[:, :, None, :] * (1.0 + (a - 1.0) * k_a))
    r_h, v_h = _heads(r), _heads(v)
    shp = a_h.shape
    kk2 = jnp.broadcast_to(kk[:, :, None], shp)
    xs = (_time_major(jnp.broadcast_to(r_h[:, :, None], shp)), _time_major(kk2), _time_major(_heads(decay)),
          _time_major(kk2 * a_h), _time_major(k_dir), _time_major(jnp.broadcast_to(v_h[:, :, None], shp)))
    S_fin, ys = lax.scan(_wkv_step, S0.astype(f32), xs)
    y = jnp.transpose(ys[:, 0] + ys[::-1, 1], (1, 0, 2, 3))
    mu = jnp.mean(y, -1, keepdims=True)
    var = jnp.mean(jnp.square(y - mu), -1, keepdims=True)
    yn = (y - mu) * lax.rsqrt(var + GN_EPS) * gn_g.reshape(N_RWKV_HEADS, HEAD_SIZE) + gn_b.reshape(N_RWKV_HEADS, HEAD_SIZE)
    bonus = jnp.einsum('bthn,btdhn,hn->bth', r_h, k_dir, r_k)[..., None] * v_h
    out = (yn + bonus).reshape(B, T, RWKV_W) * g
    return out, S_fin


def _expert_choice_ffn(h, router, w1, w3, w2):
    B, T, D = h.shape
    x = h.reshape(B * T, D)
    n = B * T
    cap = min(n, max(1, EC_CAPACITY * n // N_EXPERTS))
    aff = jax.nn.softmax((x @ router).astype(jnp.float32), axis=-1)
    gate, idx = lax.top_k(aff.T, cap)
    xe = x[idx]
    hid = jax.nn.silu(jnp.einsum('ecd,edf->ecf', xe, w1)) * jnp.einsum('ecd,edf->ecf', xe, w3)
    ye = jnp.einsum('ecf,efd->ecd', hid, w2) * gate[..., None].astype(hid.dtype)
    y = jnp.zeros((n, D), ye.dtype).at[idx.reshape(-1)].add(ye.reshape(-1, D))
    return y.reshape(B, T, D).astype(h.dtype)


def _layer(x, cond, S0, grid, norm1_g, norm2_g, w_mod, b_mod, w_in, conv_dw, conv_b, conv_ln_g, conv_ln_b,
           rwkv_w0, rwkv_w_up, rwkv_a0, rwkv_a_up, rwkv_g_up, rwkv_k_k, rwkv_k_a, rwkv_r_k, rwkv_gn_g, rwkv_gn_b,
           w_out, router, exp_w1, exp_w3, exp_w2):
    mod = (jax.nn.silu(cond) @ w_mod + b_mod)[:, None, :].astype(x.dtype)
    sh1, sc1, g1, sh2, sc2, g2 = jnp.split(mod, 6, axis=-1)
    h = _rmsnorm(x, norm1_g) * (1.0 + sc1) + sh1
    proj = h @ w_in
    u_val, u_gate, r, k, v, xw, xa, xg = jnp.split(proj, SPLIT_IDX, axis=-1)
    conv_out = _conv_group(u_val, u_gate, grid, conv_dw, conv_b, conv_ln_g, conv_ln_b)
    rw_out, S_fin = _rwkv_group(r, k, v, xw, xa, xg, S0, rwkv_w0, rwkv_w_up, rwkv_a0, rwkv_a_up, rwkv_g_up,
                                rwkv_k_k, rwkv_k_a, rwkv_r_k, rwkv_gn_g, rwkv_gn_b)
    mix = jnp.concatenate([conv_out, rw_out.astype(conv_out.dtype)], axis=-1) @ w_out
    x = x + g1 * mix.astype(x.dtype)
    h = _rmsnorm(x, norm2_g) * (1.0 + sc2) + sh2
    x = x + g2 * _expert_choice_ffn(h, router, exp_w1, exp_w3, exp_w2)
    return x, S_fin


def setup_inputs(seed: int = 0) -> dict:
    key = jax.random.key(seed)
    ks = jax.random.split(key, 32)
    f32 = jnp.float32

    def nrm(k, shape, s):
        return jax.random.normal(k, shape, f32) * s

    L, D, E, F, H, N = DEPTH, D_MODEL, N_EXPERTS, D_FF, N_RWKV_HEADS, HEAD_SIZE
    return {
        'x_prompt': nrm(ks[0], (BATCH, SEQ, D), 1.0),
        'x_sample': nrm(ks[1], (DEC_BATCH, DEC_SEQ, D), 1.0),
        'state_wkv': nrm(ks[2], (DEC_BATCH, DEPTH, N_DIR, H, N, N), 0.5),
        'c': nrm(ks[3], (DEC_BATCH, D), 1.0),
        'c_ctx': nrm(ks[4], (D,), 1.0),
        'norm1_g': 1.0 + nrm(ks[5], (L, D), 0.02),
        'norm2_g': 1.0 + nrm(ks[6], (L, D), 0.02),
        'w_mod': nrm(ks[7], (L, D, 6 * D), 0.3 * D ** -0.5),
        'b_mod': nrm(ks[8], (L, 6 * D), 0.02),
        'w_in': nrm(ks[9], (L, D, D_IN), D ** -0.5),
        'conv_dw': nrm(ks[10], (L, CONV_K, CONV_W), CONV_K ** -0.5),
        'conv_b': nrm(ks[11], (L, CONV_W), 0.02),
        'conv_ln_g': 1.0 + nrm(ks[12], (L, CONV_W), 0.02),
        'conv_ln_b': nrm(ks[13], (L, CONV_W), 0.02),
        'rwkv_w0': jax.random.uniform(ks[14], (L, N_DIR, RWKV_W), f32, -2.5, 0.5),
        'rwkv_w_up': nrm(ks[15], (L, N_DIR, DECAY_LORA, RWKV_W), 0.1 * DECAY_LORA ** -0.5),
        'rwkv_a0': nrm(ks[16], (L, N_DIR, RWKV_W), 0.5),
        'rwkv_a_up': nrm(ks[17], (L, N_DIR, AAA_LORA, RWKV_W), 0.3 * AAA_LORA ** -0.5),
        'rwkv_g_up': nrm(ks[18], (L, GATE_LORA, RWKV_W), GATE_LORA ** -0.5),
        'rwkv_k_k': 0.85 + nrm(ks[19], (L, RWKV_W), 0.05),
        'rwkv_k_a': 1.0 + nrm(ks[20], (L, RWKV_W), 0.05),
        'rwkv_r_k': nrm(ks[21], (L, H, N), 0.1),
        'rwkv_gn_g': 1.0 + nrm(ks[22], (L, RWKV_W), 0.02),
        'rwkv_gn_b': nrm(ks[23], (L, RWKV_W), 0.02),
        'w_out': nrm(ks[24], (L, D_MIX, D), D_MIX ** -0.5),
        'router': nrm(ks[25], (L, D, E), D ** -0.5),
        'exp_w1': nrm(ks[26], (L, E, D, F), D ** -0.5),
        'exp_w3': nrm(ks[27], (L, E, D, F), D ** -0.5),
        'exp_w2': nrm(ks[28], (L, E, F, D), F ** -0.5),
        'final_norm_g': 1.0 + nrm(ks[29], (D,), 0.02),
    }


def reference(x_prompt, x_sample, state_wkv, c, c_ctx, norm1_g, norm2_g, w_mod, b_mod, w_in, conv_dw, conv_b,
              conv_ln_g, conv_ln_b, rwkv_w0, rwkv_w_up, rwkv_a0, rwkv_a_up, rwkv_g_up, rwkv_k_k, rwkv_k_a, rwkv_r_k,
              rwkv_gn_g, rwkv_gn_b, w_out, router, exp_w1, exp_w3, exp_w2, final_norm_g):
    stacked = (norm1_g, norm2_g, w_mod, b_mod, w_in, conv_dw, conv_b, conv_ln_g, conv_ln_b, rwkv_w0, rwkv_w_up,
               rwkv_a0, rwkv_a_up, rwkv_g_up, rwkv_k_k, rwkv_k_a, rwkv_r_k, rwkv_gn_g, rwkv_gn_b, w_out, router,
               exp_w1, exp_w3, exp_w2)
    xp = x_prompt
    bp = xp.shape[0]
    states = []
    for l in range(DEPTH):
        lp = tuple(w[l] for w in stacked)
        S0 = jnp.zeros((N_DIR, bp, N_RWKV_HEADS, HEAD_SIZE, HEAD_SIZE), jnp.float32)
        xp, S_fin = _layer(xp, c_ctx[None, :], S0, False, *lp)
        states.append(jnp.transpose(S_fin, (1, 0, 2, 3, 4)))
    new_state_wkv = jnp.stack(states, axis=1)
    xs = x_sample
    for l in range(DEPTH):
        lp = tuple(w[l] for w in stacked)
        S0 = jnp.transpose(state_wkv[:, l], (1, 0, 2, 3, 4))
        xs, _ = _layer(xs, c, S0, True, *lp)
    y_prompt = _rmsnorm(xp, final_norm_g)
    y_sample = _rmsnorm(xs, final_norm_g)
    return (y_prompt, y_sample, new_state_wkv)
```

```python
import functools

import jax
import jax.numpy as jnp
from jax import lax
from jax.experimental import pallas as pl
from jax.experimental.pallas import tpu as pltpu

F32 = jnp.float32
BF16 = jnp.bfloat16

D_MODEL = 1024
BATCH = 16
SEQ = 256
DEC_BATCH = 4
DEC_SEQ = 1024
GRID_W = 64
CONV_W = 512
CONV_K = 31
CONV_PAD = 15
RWKV_W = 512
HEAD = 64
N_HEADS = 8
LORA = 64
GATE_LORA = 128
N_EXPERTS = 16
D_FF = 2816
D_IN = 2944
CAP = 512
N_TOK = 4096
RMS_EPS = 1e-6
LN_EPS = 1e-5
GN_EPS = 64e-5

ROW_TILE = 512
CONV_TILE = 256
FF_TILE = 256
SCAN_TB = 16
TOK_STRIDE = CAP + 8
BISECT_ITERS = 160
VMEM_LIMIT = 56 * 1024 * 1024


def _cparams(n_axes, vmem=None):
    return pltpu.CompilerParams(dimension_semantics=("arbitrary",) * n_axes,
                                vmem_limit_bytes=vmem or VMEM_LIMIT)


def _cond_index(i):
    n_prompt = (BATCH * SEQ) // ROW_TILE
    per_batch = DEC_SEQ // ROW_TILE
    return jnp.where(i < n_prompt, 0, 1 + (i - n_prompt) // per_batch)


def _split_dot(x, w_bf16):
    hi = x.astype(BF16)
    lo = (x - hi.astype(F32)).astype(BF16)
    return (jnp.dot(hi, w_bf16, preferred_element_type=F32)
            + jnp.dot(lo, w_bf16, preferred_element_type=F32))


def _sigmoid(x):
    return 1.0 / (1.0 + jnp.exp(-x))


def _mod_kernel(c_ref, w_ref, b_ref, o_ref):
    c = c_ref[...]
    s = (c * _sigmoid(c)).astype(BF16)
    o_ref[...] = jnp.dot(s, w_ref[...].astype(BF16), preferred_element_type=F32) + b_ref[...]


def _modulation(cond8, w_mod, b_mod):
    n = 6 * D_MODEL
    tn = D_MODEL
    return pl.pallas_call(
        _mod_kernel,
        grid=(n // tn,),
        in_specs=[pl.BlockSpec((8, D_MODEL), lambda j: (0, 0)),
                  pl.BlockSpec((D_MODEL, tn), lambda j: (0, j)),
                  pl.BlockSpec((1, tn), lambda j: (0, j))],
        out_specs=pl.BlockSpec((8, tn), lambda j: (0, j)),
        out_shape=jax.ShapeDtypeStruct((8, n), F32),
        compiler_params=_cparams(1),
        name="modulation",
    )(cond8, w_mod, b_mod)


def _inproj_kernel(x_ref, sh_ref, sc_ref, g_ref, w_ref, u_ref, rkv_ref, lora_ref):
    x = x_ref[...]
    ms = jnp.mean(x * x, axis=-1, keepdims=True)
    h = x * lax.rsqrt(ms + RMS_EPS) * g_ref[...]
    h = (h * (1.0 + sc_ref[0]) + sh_ref[0]).astype(BF16)
    uv = jnp.dot(h, w_ref[:, 0:CONV_W], preferred_element_type=F32)
    ug = jnp.dot(h, w_ref[:, CONV_W:2 * CONV_W], preferred_element_type=F32)
    u_ref[...] = uv * _sigmoid(ug)
    rkv_ref[...] = jnp.dot(h, w_ref[:, 2 * CONV_W:2 * CONV_W + 3 * RWKV_W], preferred_element_type=F32)
    lora_ref[...] = jnp.dot(h, w_ref[:, 2 * CONV_W + 3 * RWKV_W:D_IN], preferred_element_type=F32)


def _inproj(x, mod3, norm1_g, w_in_bf16):
    n = x.shape[0]
    n_lora = D_IN - 2 * CONV_W - 3 * RWKV_W
    return pl.pallas_call(
        _inproj_kernel,
        grid=(n // ROW_TILE,),
        in_specs=[pl.BlockSpec((ROW_TILE, D_MODEL), lambda i: (i, 0)),
                  pl.BlockSpec((1, 1, D_MODEL), lambda i: (_cond_index(i), 0, 0)),
                  pl.BlockSpec((1, 1, D_MODEL), lambda i: (_cond_index(i), 0, 1)),
                  pl.BlockSpec((1, D_MODEL), lambda i: (0, 0)),
                  pl.BlockSpec((D_MODEL, D_IN), lambda i: (0, 0))],
        out_specs=[pl.BlockSpec((ROW_TILE, CONV_W), lambda i: (i, 0)),
                   pl.BlockSpec((ROW_TILE, 3 * RWKV_W), lambda i: (i, 0)),
                   pl.BlockSpec((ROW_TILE, n_lora), lambda i: (i, 0))],
        out_shape=[jax.ShapeDtypeStruct((n, CONV_W), F32),
                   jax.ShapeDtypeStruct((n, 3 * RWKV_W), F32),
                   jax.ShapeDtypeStruct((n, n_lora), F32)],
        compiler_params=_cparams(1),
        name="inproj",
    )(x, mod3, mod3, norm1_g, w_in_bf16)


def _conv_kernel(u_ref, dw_ref, db_ref, g_ref, b_ref, o_ref, *, seg):
    u = u_ref[...]
    pos = lax.broadcasted_iota(jnp.int32, (CONV_TILE, 1), 0) % seg
    acc = jnp.zeros_like(u)
    for k in range(CONV_K):
        s = k - CONV_PAD
        shifted = u if s == 0 else pltpu.roll(u, (-s) % CONV_TILE, 0)
        valid = (pos + s >= 0) & (pos + s < seg)
        acc = acc + jnp.where(valid, shifted, 0.0) * dw_ref[k:k + 1, :]
    out = acc + db_ref[...]
    mu = jnp.mean(out, axis=-1, keepdims=True)
    d = out - mu
    var = jnp.mean(d * d, axis=-1, keepdims=True)
    out = d * lax.rsqrt(var + LN_EPS) * g_ref[...] + b_ref[...]
    o_ref[...] = out * _sigmoid(out)


def _conv_branch(u, dw, db, ln_g, ln_b, seg):
    n = u.shape[0]
    return pl.pallas_call(
        functools.partial(_conv_kernel, seg=seg),
        grid=(n // CONV_TILE,),
        in_specs=[pl.BlockSpec((CONV_TILE, CONV_W), lambda i: (i, 0)),
                  pl.BlockSpec((CONV_K, CONV_W), lambda i: (0, 0)),
                  pl.BlockSpec((1, CONV_W), lambda i: (0, 0)),
                  pl.BlockSpec((1, CONV_W), lambda i: (0, 0)),
                  pl.BlockSpec((1, CONV_W), lambda i: (0, 0))],
        out_specs=pl.BlockSpec((CONV_TILE, CONV_W), lambda i: (i, 0)),
        out_shape=jax.ShapeDtypeStruct((n, CONV_W), F32),
        compiler_params=_cparams(1),
        name=f"conv_seg{seg}",
    )(u, dw, db, ln_g, ln_b)


def _prep_kernel(rkv_ref, lora_ref, ones_ref, w0_ref, wup_ref, a0_ref, aup_ref, gup_ref, kk_ref, ka_ref, rk_ref,
                 r_o, kap_o, v_o, w0_o, w1_o, b0_o, b1_o, kd0_o, kd1_o, g_o, bonus_o):
    r = rkv_ref[:, 0:RWKV_W]
    k = rkv_ref[:, RWKV_W:2 * RWKV_W]
    v = rkv_ref[:, 2 * RWKV_W:3 * RWKV_W]
    xw = jnp.tanh(lora_ref[:, 0:2 * LORA]).astype(BF16)
    xa = lora_ref[:, 2 * LORA:4 * LORA].astype(BF16)
    xg = _sigmoid(lora_ref[:, 4 * LORA:4 * LORA + GATE_LORA]).astype(BF16)
    ones = ones_ref[...]

    kk0 = k * kk_ref[...]
    ss = _split_dot(kk0 * kk0, ones)
    kap = kk0 * lax.rsqrt(jnp.maximum(ss, 1e-24))
    r_o[...] = r
    kap_o[...] = kap
    v_o[...] = v
    g_o[...] = jnp.dot(xg, gup_ref[...], preferred_element_type=F32)

    kd_sum = jnp.zeros_like(k)
    for d, (w_o, b_o, kd_o) in enumerate(((w0_o, b0_o, kd0_o), (w1_o, b1_o, kd1_o))):
        wl = w0_ref[d:d + 1, :] + jnp.dot(xw, wup_ref[d], preferred_element_type=F32)
        w_o[...] = jnp.exp(-jnp.exp(-0.5) * _sigmoid(wl))
        a = _sigmoid(a0_ref[d:d + 1, :] + jnp.dot(xa, aup_ref[d], preferred_element_type=F32))
        b_o[...] = kap * a
        kd = k * (1.0 + (a - 1.0) * ka_ref[...])
        kd_o[...] = kd
        kd_sum = kd_sum + kd
    bonus_o[...] = _split_dot(r * kd_sum * rk_ref[...], ones) * v


def _rwkv_prep(rkv, lora, ones_blk, w0, w_up, a0, a_up, g_up, k_k, k_a, r_k):
    n = rkv.shape[0]
    row = lambda c: pl.BlockSpec((ROW_TILE, c), lambda i: (i, 0))
    full = lambda a: pl.BlockSpec(a.shape, lambda i: (0,) * a.ndim)
    args = (rkv, lora, ones_blk, w0, w_up, a0, a_up, g_up, k_k, k_a, r_k)
    return pl.pallas_call(
        _prep_kernel,
        grid=(n // ROW_TILE,),
        in_specs=[row(3 * RWKV_W), row(lora.shape[1])] + [full(a) for a in args[2:]],
        out_specs=[row(RWKV_W)] * 11,
        out_shape=[jax.ShapeDtypeStruct((n, RWKV_W), F32)] * 11,
        compiler_params=_cparams(1),
        name="rwkv_prep",
    )(*args)


def _scan_kernel(r_ref, kap_ref, w_ref, b_ref, kd_ref, v_ref, s0_ref, y_ref, sfin_ref, s_scr, *, vl, reverse):
    i = pl.program_id(0)

    @pl.when(i == 0)
    def _():
        s_scr[...] = s0_ref[...]

    def tstep(tt, carry):
        t = SCAN_TB - 1 - tt if reverse else tt
        kap = kap_ref[t]
        w = w_ref[t]
        bb = b_ref[t]
        kd = kd_ref[t]
        r = r_ref[t]

        def gstep(gi, c2):
            base = pl.multiple_of(gi * 8, 8)
            v8 = v_ref[t, pl.ds(base, 8), :]
            ys = []
            for j in range(8):
                s = s_scr[base + j]
                sa = jnp.sum(s * kap, axis=0, keepdims=True)
                sn = s * w - sa * bb + v8[j:j + 1, :] * kd
                ys.append(jnp.sum(sn * r, axis=0, keepdims=True))
                s_scr[base + j] = sn
            y_ref[t, pl.ds(base, 8), :] = jnp.concatenate(ys, axis=0)
            return c2

        return lax.fori_loop(0, vl // 8, gstep, carry)

    lax.fori_loop(0, SCAN_TB, tstep, 0)

    @pl.when(i == pl.num_programs(0) - 1)
    def _():
        sfin_ref[...] = s_scr[...]


def _scan(r, kap, w, b, kd, v, s0, reverse):
    t_len = r.shape[0]
    vl = v.shape[1]
    nblk = t_len // SCAN_TB
    tmap = (lambda i: (nblk - 1 - i, 0, 0)) if reverse else (lambda i: (i, 0, 0))
    kspec = pl.BlockSpec((SCAN_TB, HEAD, 128), tmap)
    vspec = pl.BlockSpec((SCAN_TB, vl, 128), tmap)
    sspec = pl.BlockSpec((vl, HEAD, 128), lambda i: (0, 0, 0))
    return pl.pallas_call(
        functools.partial(_scan_kernel, vl=vl, reverse=reverse),
        grid=(nblk,),
        in_specs=[kspec] * 5 + [vspec, sspec],
        out_specs=[vspec, sspec],
        out_shape=[jax.ShapeDtypeStruct((t_len, vl, 128), F32),
                   jax.ShapeDtypeStruct((vl, HEAD, 128), F32)],
        scratch_shapes=[pltpu.VMEM((vl, HEAD, 128), F32)],
        compiler_params=_cparams(1),
        name=f"wkv_scan_vl{vl}_{'bwd' if reverse else 'fwd'}",
    )(r, kap, w, b, kd, v, s0)


def _postmix_kernel(yf_ref, yb_ref, bonus_ref, g_ref, conv_ref, x_ref, g1_ref, sh2_ref, sc2_ref, gng_ref, gnb_ref,
                    ones_ref, wout_ref, n2g_ref, rhi_ref, rlo_ref, x1_o, h2_o, logit_o):
    ones = ones_ref[...]
    y = yf_ref[...] + yb_ref[...]
    mu = _split_dot(y, ones) * (1.0 / HEAD)
    d = y - mu
    var = _split_dot(d * d, ones) * (1.0 / HEAD)
    yn = d * lax.rsqrt(var + GN_EPS) * gng_ref[...] + gnb_ref[...]
    rw = ((yn + bonus_ref[...]) * g_ref[...]).astype(BF16)
    mix = (jnp.dot(conv_ref[...].astype(BF16), wout_ref[0:CONV_W, :], preferred_element_type=F32)
           + jnp.dot(rw, wout_ref[CONV_W:CONV_W + RWKV_W, :], preferred_element_type=F32))
    x1 = x_ref[...] + g1_ref[0] * mix
    x1_o[...] = x1
    ms = jnp.mean(x1 * x1, axis=-1, keepdims=True)
    h2 = x1 * lax.rsqrt(ms + RMS_EPS) * n2g_ref[...]
    h2 = h2 * (1.0 + sc2_ref[0]) + sh2_ref[0]
    h2_o[...] = h2
    hi = h2.astype(BF16)
    lo = (h2 - hi.astype(F32)).astype(BF16)
    logit_o[...] = (jnp.dot(hi, rhi_ref[...], preferred_element_type=F32)
                    + jnp.dot(lo, rhi_ref[...], preferred_element_type=F32)
                    + jnp.dot(hi, rlo_ref[...], preferred_element_type=F32))


def _postmix(yf, yb, bonus, g, conv, x, mod3, gn_g, gn_b, ones_blk, w_out_bf16, norm2_g, r_hi, r_lo):
    n = x.shape[0]
    row = lambda c: pl.BlockSpec((ROW_TILE, c), lambda i: (i, 0))
    full = lambda a: pl.BlockSpec(a.shape, lambda i: (0,) * a.ndim)
    modspec = lambda j: pl.BlockSpec((1, 1, D_MODEL), lambda i: (_cond_index(i), 0, j))
    return pl.pallas_call(
        _postmix_kernel,
        grid=(n // ROW_TILE,),
        in_specs=[row(RWKV_W)] * 4 + [row(CONV_W), row(D_MODEL), modspec(2), modspec(3), modspec(4),
                                      full(gn_g), full(gn_b), full(ones_blk), full(w_out_bf16), full(norm2_g),
                                      full(r_hi), full(r_lo)],
        out_specs=[row(D_MODEL), row(D_MODEL), row(128)],
        out_shape=[jax.ShapeDtypeStruct((n, D_MODEL), F32), jax.ShapeDtypeStruct((n, D_MODEL), F32),
                   jax.ShapeDtypeStruct((n, 128), F32)],
        compiler_params=_cparams(1),
        name="postmix",
    )(yf, yb, bonus, g, conv, x, mod3, mod3, mod3, gn_g, gn_b, ones_blk, w_out_bf16, norm2_g, r_hi, r_lo)


def _prefix_lanes(m, utri):
    rows = m.shape[0]
    off = jnp.zeros((rows, 1), F32)
    out = []
    for c in range(N_TOK // 128):
        blk = m[:, c * 128:(c + 1) * 128]
        out.append(jnp.dot(blk.astype(BF16), utri, preferred_element_type=F32) + off)
        off = off + jnp.sum(blk, axis=1, keepdims=True)
    return jnp.concatenate(out, axis=1)


def _select_kernel(lt_ref, utri_ref, pos_o, aff_o):
    utri = utri_ref[...]
    for grp in range(2):
        rows = slice(grp * N_EXPERTS, (grp + 1) * N_EXPERTS)
        x = lt_ref[rows, :]
        e = jnp.exp(x - jnp.max(x, axis=0, keepdims=True))
        aff = e / jnp.sum(e, axis=0, keepdims=True)
        aff_o[rows, :] = aff

        def body(_, lh):
            lo, hi = lh
            mid = 0.5 * (lo + hi)
            cnt = jnp.sum(jnp.where(aff > mid, 1.0, 0.0), axis=1, keepdims=True)
            ge = cnt >= CAP
            return jnp.where(ge, mid, lo), jnp.where(ge, hi, mid)

        lo, hi = lax.fori_loop(0, BISECT_ITERS, body,
                               (jnp.full((N_EXPERTS, 1), -1.0, F32), jnp.full((N_EXPERTS, 1), 1.0, F32)))
        gt = jnp.where(aff > hi, 1.0, 0.0)
        tie = jnp.where(aff > lo, 1.0, 0.0) - gt
        need = CAP - jnp.sum(gt, axis=1, keepdims=True)
        sel = gt + tie * jnp.where(_prefix_lanes(tie, utri) < need, 1.0, 0.0)
        pos = _prefix_lanes(sel, utri)
        pos_o[rows, :] = jnp.where(sel > 0.5, pos, -1.0)


def _select(logits_t, utri):
    shp = jax.ShapeDtypeStruct((2 * N_EXPERTS, N_TOK), F32)
    full = lambda a: pl.BlockSpec(a.shape, lambda i: (0,) * a.ndim)
    return pl.pallas_call(
        _select_kernel,
        grid=(1,),
        in_specs=[full(logits_t), full(utri)],
        out_specs=[pl.BlockSpec(shp.shape, lambda i: (0, 0))] * 2,
        out_shape=[shp, shp],
        compiler_params=_cparams(1),
        name="ec_select",
    )(logits_t, utri)


def _compact_kernel(pos_ref, aff_ref, idx_o, gate_o):
    slot = lax.broadcasted_iota(jnp.int32, (CAP, 1), 0).astype(F32)
    idx = jnp.zeros((CAP, 128), F32)
    gate = jnp.zeros((CAP, 128), F32)
    for c in range(N_TOK // 128):
        p = pos_ref[0, :, c * 128:(c + 1) * 128]
        a = aff_ref[0, :, c * 128:(c + 1) * 128]
        tok = (lax.broadcasted_iota(jnp.int32, (1, 128), 1) + c * 128).astype(F32)
        hit = p == slot
        idx = idx + jnp.where(hit, tok, 0.0)
        gate = gate + jnp.where(hit, a, 0.0)
    idx_o[0] = jnp.broadcast_to(jnp.sum(idx, axis=1, keepdims=True), (CAP, 128))
    gate_o[0] = jnp.broadcast_to(jnp.sum(gate, axis=1, keepdims=True), (CAP, 128))


def _compact(pos3, aff3):
    rows = pos3.shape[0]
    shp = jax.ShapeDtypeStruct((rows, CAP, 128), F32)
    return pl.pallas_call(
        _compact_kernel,
        grid=(rows,),
        in_specs=[pl.BlockSpec((1, 1, N_TOK), lambda i: (i, 0, 0))] * 2,
        out_specs=[pl.BlockSpec((1, CAP, 128), lambda i: (i, 0, 0))] * 2,
        out_shape=[shp, shp],
        compiler_params=_cparams(1),
        name="ec_compact",
    )(pos3, aff3)


def _gather_kernel(idx_ref, h_ref, o_ref, tile):
    grp = pl.program_id(0)
    e = pl.program_id(1)
    base = (grp * N_EXPERTS + e) * CAP

    def tok(j8, c):
        for u in range(8):
            j = j8 * 8 + u
            row = pl.multiple_of(idx_ref[base + j] * 8, 8)
            tile[pl.ds(j, 8, stride=TOK_STRIDE), :] = h_ref[0, pl.ds(row, 8), :]
        return c

    lax.fori_loop(0, CAP // 8, tok, 0)
    for c in range(8):
        o_ref[0, :, c * 128:(c + 1) * 128] = tile[c * TOK_STRIDE:c * TOK_STRIDE + CAP, :].astype(BF16)


def _gather(idx_flat, h_rows):
    return pl.pallas_call(
        _gather_kernel,
        grid_spec=pltpu.PrefetchScalarGridSpec(
            num_scalar_prefetch=1,
            grid=(2, N_EXPERTS),
            in_specs=[pl.BlockSpec((1, N_TOK * 8, 128), lambda g, e, idx: (g, 0, 0))],
            out_specs=pl.BlockSpec((1, CAP, D_MODEL), lambda g, e, idx: (e, g, 0)),
            scratch_shapes=[pltpu.VMEM((8 * TOK_STRIDE, 128), F32)]),
        out_shape=jax.ShapeDtypeStruct((N_EXPERTS, 2 * CAP, D_MODEL), BF16),
        compiler_params=_cparams(2),
        name="ec_gather",
    )(idx_flat, h_rows)


def _expert_kernel(xe_ref, w1_ref, w3_ref, w2_ref, gate_ref, o_ref):
    f = pl.program_id(1)
    x = xe_ref[0]
    h1 = jnp.dot(x, w1_ref[0].astype(BF16), preferred_element_type=F32)
    h3 = jnp.dot(x, w3_ref[0].astype(BF16), preferred_element_type=F32)
    hid = (h1 * _sigmoid(h1) * h3).astype(BF16)
    part = jnp.dot(hid, w2_ref[0].astype(BF16), preferred_element_type=F32)

    @pl.when(f == 0)
    def _():
        o_ref[0] = part

    @pl.when(f > 0)
    def _():
        o_ref[0] = o_ref[0] + part

    @pl.when(f == pl.num_programs(1) - 1)
    def _():
        o_ref[0] = o_ref[0] * gate_ref[0]


def _experts(xe, w1, w3, w2, gate_col):
    m = xe.shape[1]
    return pl.pallas_call(
        _expert_kernel,
        grid=(N_EXPERTS, D_FF // FF_TILE),
        in_specs=[pl.BlockSpec((1, m, D_MODEL), lambda e, f: (e, 0, 0)),
                  pl.BlockSpec((1, D_MODEL, FF_TILE), lambda e, f: (e, 0, f)),
                  pl.BlockSpec((1, D_MODEL, FF_TILE), lambda e, f: (e, 0, f)),
                  pl.BlockSpec((1, FF_TILE, D_MODEL), lambda e, f: (e, f, 0)),
                  pl.BlockSpec((1, m, 1), lambda e, f: (e, 0, 0))],
        out_specs=pl.BlockSpec((1, m, D_MODEL), lambda e, f: (e, 0, 0)),
        out_shape=jax.ShapeDtypeStruct((N_EXPERTS, m, D_MODEL), F32),
        compiler_params=_cparams(2),
        name="ec_experts",
    )(xe, w1, w3, w2, gate_col)


def _combine_kernel(idx_ref, ye_ref, o_ref, tile):
    grp = pl.program_id(0)
    e = pl.program_id(1)
    base = (grp * N_EXPERTS + e) * CAP

    @pl.when(e == 0)
    def _():
        o_ref[...] = jnp.zeros_like(o_ref)

    for c in range(8):
        tile[c * TOK_STRIDE:c * TOK_STRIDE + CAP, :] = ye_ref[0, :, c * 128:(c + 1) * 128]

    def tok(j4, c):
        rows, vals = [], []
        for u in range(4):
            j = j4 * 4 + u
            row = pl.multiple_of(idx_ref[base + j] * 8, 8)
            rows.append(row)
            vals.append(o_ref[0, pl.ds(row, 8), :] + tile[pl.ds(j, 8, stride=TOK_STRIDE), :])
        for row, val in zip(rows, vals):
            o_ref[0, pl.ds(row, 8), :] = val
        return c

    lax.fori_loop(0, CAP // 4, tok, 0)


def _combine(idx_flat, ye):
    return pl.pallas_call(
        _combine_kernel,
        grid_spec=pltpu.PrefetchScalarGridSpec(
            num_scalar_prefetch=1,
            grid=(2, N_EXPERTS),
            in_specs=[pl.BlockSpec((1, CAP, D_MODEL), lambda g, e, idx: (e, g, 0))],
            out_specs=pl.BlockSpec((1, N_TOK * 8, 128), lambda g, e, idx: (g, 0, 0)),
            scratch_shapes=[pltpu.VMEM((8 * TOK_STRIDE, 128), F32)]),
        out_shape=jax.ShapeDtypeStruct((2, N_TOK * 8, 128), F32),
        compiler_params=_cparams(2),
        name="ec_combine",
    )(idx_flat, ye)


def _final_kernel(x1_ref, y_ref, g2_ref, fg_ref, o_ref):
    x = x1_ref[...] + g2_ref[0] * y_ref[...]
    ms = jnp.mean(x * x, axis=-1, keepdims=True)
    o_ref[...] = x * lax.rsqrt(ms + RMS_EPS) * fg_ref[...]


def _final(x1, y, mod3, final_g):
    n = x1.shape[0]
    row = pl.BlockSpec((ROW_TILE, D_MODEL), lambda i: (i, 0))
    return pl.pallas_call(
        _final_kernel,
        grid=(n // ROW_TILE,),
        in_specs=[row, row, pl.BlockSpec((1, 1, D_MODEL), lambda i: (_cond_index(i), 0, 5)),
                  pl.BlockSpec((1, D_MODEL), lambda i: (0, 0))],
        out_specs=row,
        out_shape=jax.ShapeDtypeStruct((n, D_MODEL), F32),
        compiler_params=_cparams(1),
        name="final_norm",
    )(x1, y, mod3, final_g)


def _prompt_lanes(a):
    return a[:N_TOK].reshape(BATCH, SEQ, N_HEADS, HEAD).transpose(1, 3, 0, 2).reshape(SEQ, HEAD, BATCH * N_HEADS)


def _sample_lanes(a_fwd, a_bwd):
    def one(a):
        return a[N_TOK:].reshape(DEC_BATCH, DEC_SEQ, N_HEADS, HEAD).transpose(1, 3, 0, 2).reshape(
            DEC_SEQ, HEAD, DEC_BATCH * N_HEADS)
    return jnp.concatenate([one(a_fwd), one(a_bwd)[::-1]], axis=-1)


def _dup_lanes(a):
    return jnp.concatenate([a, a], axis=-1)


def _split_v(a):
    t = a.shape[0]
    return a.reshape(t, 2, HEAD // 2, 64).transpose(0, 2, 1, 3).reshape(t, HEAD // 2, 128)


def _unsplit_v(a):
    t = a.shape[0]
    return a.reshape(t, HEAD // 2, 2, 64).transpose(0, 2, 1, 3).reshape(t, HEAD, 64)


def kernel(x_prompt, x_sample, state_wkv, c, c_ctx, norm1_g, norm2_g, w_mod, b_mod, w_in, conv_dw, conv_b, conv_ln_g,
           conv_ln_b, rwkv_w0, rwkv_w_up, rwkv_a0, rwkv_a_up, rwkv_g_up, rwkv_k_k, rwkv_k_a, rwkv_r_k, rwkv_gn_g,
           rwkv_gn_b, w_out, router, exp_w1, exp_w3, exp_w2, final_norm_g):
    row2 = lambda a: a.reshape(1, -1)
    x = jnp.concatenate([x_prompt.reshape(N_TOK, D_MODEL), x_sample.reshape(N_TOK, D_MODEL)], axis=0)
    cond8 = jnp.concatenate([c_ctx[None, :], c, jnp.zeros((8 - 1 - DEC_BATCH, D_MODEL), F32)], axis=0)

    mod = _modulation(cond8, w_mod[0], b_mod)
    mod3 = mod.reshape(8, 1, 6 * D_MODEL)

    u, rkv, lora = _inproj(x, mod3, norm1_g, w_in[0].astype(BF16))

    conv_args = (conv_dw[0], conv_b, conv_ln_g, conv_ln_b)
    conv = jnp.concatenate([_conv_branch(u[:N_TOK], *conv_args, seg=SEQ),
                            _conv_branch(u[N_TOK:], *conv_args, seg=GRID_W)], axis=0)

    lane_head = jnp.arange(RWKV_W, dtype=jnp.int32) // HEAD
    ones_blk = (lane_head[:, None] == lane_head[None, :]).astype(BF16)
    def per_dir_rows(up):
        z = jnp.zeros_like(up[0])
        return jnp.stack([jnp.concatenate([up[0], z], axis=0), jnp.concatenate([z, up[1]], axis=0)]).astype(BF16)
    (r, kap, v, w0, w1, b0, b1, kd0, kd1, g, bonus) = _rwkv_prep(
        rkv, lora, ones_blk, rwkv_w0[0], per_dir_rows(rwkv_w_up[0]), rwkv_a0[0], per_dir_rows(rwkv_a_up[0]),
        rwkv_g_up[0].astype(BF16), rwkv_k_k, rwkv_k_a, row2(rwkv_r_k[0]))

    zero_state = jnp.zeros((HEAD, HEAD, 128), F32)
    pr, pk, pv = _prompt_lanes(r), _prompt_lanes(kap), _prompt_lanes(v)
    yp_f, sp_f = _scan(pr, pk, _prompt_lanes(w0), _prompt_lanes(b0), _prompt_lanes(kd0), pv, zero_state, False)
    yp_b, sp_b = _scan(pr, pk, _prompt_lanes(w1), _prompt_lanes(b1), _prompt_lanes(kd1), pv, zero_state, True)

    s0 = state_wkv[:, 0].transpose(3, 4, 1, 0, 2).reshape(HEAD, HEAD, 64)
    s0 = s0.reshape(2, HEAD // 2, HEAD, 64).transpose(1, 2, 0, 3).reshape(HEAD // 2, HEAD, 128)
    ys, _ = _scan(_dup_lanes(_sample_lanes(r, r)), _dup_lanes(_sample_lanes(kap, kap)),
                  _dup_lanes(_sample_lanes(w0, w1)), _dup_lanes(_sample_lanes(b0, b1)),
                  _dup_lanes(_sample_lanes(kd0, kd1)), _split_v(_sample_lanes(v, v)), s0, False)

    def prompt_rows(y):
        return y.reshape(SEQ, HEAD, BATCH, N_HEADS).transpose(2, 0, 3, 1).reshape(N_TOK, RWKV_W)

    def sample_rows(y):
        return y.reshape(DEC_SEQ, HEAD, DEC_BATCH, N_HEADS).transpose(2, 0, 3, 1).reshape(N_TOK, RWKV_W)

    ys = _unsplit_v(ys)
    y_f = jnp.concatenate([prompt_rows(yp_f), sample_rows(ys[:, :, :32])], axis=0)
    y_b = jnp.concatenate([prompt_rows(yp_b), sample_rows(ys[::-1, :, 32:])], axis=0)

    def state_rows(s):
        return s.reshape(HEAD, HEAD, BATCH, N_HEADS).transpose(2, 3, 0, 1)
    new_state = jnp.stack([state_rows(sp_f), state_rows(sp_b)], axis=1)[:, None]

    r_pad = jnp.pad(router[0], ((0, 0), (0, 128 - N_EXPERTS)))
    r_hi = r_pad.astype(BF16)
    r_lo = (r_pad - r_hi.astype(F32)).astype(BF16)
    x1, h2, logits = _postmix(y_f, y_b, bonus, g, conv, x, mod3, rwkv_gn_g, rwkv_gn_b, ones_blk,
                              w_out[0].astype(BF16), norm2_g, r_hi, r_lo)

    logits_t = logits[:, :N_EXPERTS].reshape(2, N_TOK, N_EXPERTS).transpose(0, 2, 1).reshape(2 * N_EXPERTS, N_TOK)
    lane = jnp.arange(128, dtype=jnp.int32)
    utri = (lane[:, None] < lane[None, :]).astype(BF16)
    pos, aff = _select(logits_t, utri)
    idx_c, gate_c = _compact(pos.reshape(2 * N_EXPERTS, 1, N_TOK), aff.reshape(2 * N_EXPERTS, 1, N_TOK))
    idx_flat = idx_c[:, :, 0].astype(jnp.int32).reshape(-1)
    gate_col = gate_c[:, :, 0].reshape(2, N_EXPERTS, CAP).transpose(1, 0, 2).reshape(N_EXPERTS, 2 * CAP, 1)

    xe = _gather(idx_flat, h2.reshape(2, N_TOK * 8, 128))
    ye = _experts(xe, exp_w1[0], exp_w3[0], exp_w2[0], gate_col)
    y_moe = _combine(idx_flat, ye).reshape(2 * N_TOK, D_MODEL)

    out = _final(x1, y_moe, mod3, row2(final_norm_g))
    return (out[:N_TOK].reshape(BATCH, SEQ, D_MODEL), out[N_TOK:].reshape(DEC_BATCH, DEC_SEQ, D_MODEL), new_state)
```

```python
import functools

import jax
import jax.numpy as jnp
from jax import lax
from jax.experimental import pallas as pl
from jax.experimental.pallas import tpu as pltpu

F32 = jnp.float32
BF16 = jnp.bfloat16

D_MODEL = 1024
BATCH = 16
SEQ = 256
DEC_BATCH = 4
DEC_SEQ = 1024
GRID_W = 64
CONV_W = 512
CONV_K = 31
CONV_PAD = 15
RWKV_W = 512
HEAD = 64
N_HEADS = 8
LORA = 64
GATE_LORA = 128
N_EXPERTS = 16
D_FF = 2816
D_IN = 2944
CAP = 512
N_TOK = 4096
RMS_EPS = 1e-6
LN_EPS = 1e-5
GN_EPS = 64e-5

ROW_TILE = 512
CONV_TILE = 256
FF_TILE = 256
CHUNK = 64
HEADS_PER_GROUP = 4
GROUP_W = HEADS_PER_GROUP * HEAD
PASSES_A = 3
PASSES_INV = 1
PASSES_STATE = 1
TOK_STRIDE = CAP + 8
BISECT_ITERS = 160
VMEM_LIMIT = 56 * 1024 * 1024


def _cparams(n_axes, vmem=None):
    return pltpu.CompilerParams(dimension_semantics=("arbitrary",) * n_axes,
                                vmem_limit_bytes=vmem or VMEM_LIMIT)


def _cond_index(i):
    n_prompt = (BATCH * SEQ) // ROW_TILE
    per_batch = DEC_SEQ // ROW_TILE
    return jnp.where(i < n_prompt, 0, 1 + (i - n_prompt) // per_batch)


def _split_dot(x, w_bf16):
    hi = x.astype(BF16)
    lo = (x - hi.astype(F32)).astype(BF16)
    return (jnp.dot(hi, w_bf16, preferred_element_type=F32)
            + jnp.dot(lo, w_bf16, preferred_element_type=F32))


def _sigmoid(x):
    return 1.0 / (1.0 + jnp.exp(-x))


def _mod_kernel(c_ref, w_ref, b_ref, o_ref):
    c = c_ref[...]
    s = (c * _sigmoid(c)).astype(BF16)
    o_ref[...] = jnp.dot(s, w_ref[...].astype(BF16), preferred_element_type=F32) + b_ref[...]


def _modulation(cond8, w_mod, b_mod):
    n = 6 * D_MODEL
    tn = D_MODEL
    return pl.pallas_call(
        _mod_kernel,
        grid=(n // tn,),
        in_specs=[pl.BlockSpec((8, D_MODEL), lambda j: (0, 0)),
                  pl.BlockSpec((D_MODEL, tn), lambda j: (0, j)),
                  pl.BlockSpec((1, tn), lambda j: (0, j))],
        out_specs=pl.BlockSpec((8, tn), lambda j: (0, j)),
        out_shape=jax.ShapeDtypeStruct((8, n), F32),
        compiler_params=_cparams(1),
        name="modulation",
    )(cond8, w_mod, b_mod)


def _inproj_kernel(x_ref, sh_ref, sc_ref, g_ref, w_ref, u_ref, rkv_ref, lora_ref):
    x = x_ref[...]
    ms = jnp.mean(x * x, axis=-1, keepdims=True)
    h = x * lax.rsqrt(ms + RMS_EPS) * g_ref[...]
    h = (h * (1.0 + sc_ref[0]) + sh_ref[0]).astype(BF16)
    uv = jnp.dot(h, w_ref[:, 0:CONV_W], preferred_element_type=F32)
    ug = jnp.dot(h, w_ref[:, CONV_W:2 * CONV_W], preferred_element_type=F32)
    u_ref[...] = uv * _sigmoid(ug)
    rkv_ref[...] = jnp.dot(h, w_ref[:, 2 * CONV_W:2 * CONV_W + 3 * RWKV_W], preferred_element_type=F32)
    lora_ref[...] = jnp.dot(h, w_ref[:, 2 * CONV_W + 3 * RWKV_W:D_IN], preferred_element_type=F32)


def _inproj(x, mod3, norm1_g, w_in_bf16):
    n = x.shape[0]
    n_lora = D_IN - 2 * CONV_W - 3 * RWKV_W
    return pl.pallas_call(
        _inproj_kernel,
        grid=(n // ROW_TILE,),
        in_specs=[pl.BlockSpec((ROW_TILE, D_MODEL), lambda i: (i, 0)),
                  pl.BlockSpec((1, 1, D_MODEL), lambda i: (_cond_index(i), 0, 0)),
                  pl.BlockSpec((1, 1, D_MODEL), lambda i: (_cond_index(i), 0, 1)),
                  pl.BlockSpec((1, D_MODEL), lambda i: (0, 0)),
                  pl.BlockSpec((D_MODEL, D_IN), lambda i: (0, 0))],
        out_specs=[pl.BlockSpec((ROW_TILE, CONV_W), lambda i: (i, 0)),
                   pl.BlockSpec((ROW_TILE, 3 * RWKV_W), lambda i: (i, 0)),
                   pl.BlockSpec((ROW_TILE, n_lora), lambda i: (i, 0))],
        out_shape=[jax.ShapeDtypeStruct((n, CONV_W), F32),
                   jax.ShapeDtypeStruct((n, 3 * RWKV_W), F32),
                   jax.ShapeDtypeStruct((n, n_lora), F32)],
        compiler_params=_cparams(1),
        name="inproj",
    )(x, mod3, mod3, norm1_g, w_in_bf16)


def _conv_kernel(u_ref, dw_ref, db_ref, g_ref, b_ref, o_ref, *, seg):
    u = u_ref[...]
    pos = lax.broadcasted_iota(jnp.int32, (CONV_TILE, 1), 0) % seg
    acc = jnp.zeros_like(u)
    for k in range(CONV_K):
        s = k - CONV_PAD
        shifted = u if s == 0 else pltpu.roll(u, (-s) % CONV_TILE, 0)
        valid = (pos + s >= 0) & (pos + s < seg)
        acc = acc + jnp.where(valid, shifted, 0.0) * dw_ref[k:k + 1, :]
    out = acc + db_ref[...]
    mu = jnp.mean(out, axis=-1, keepdims=True)
    d = out - mu
    var = jnp.mean(d * d, axis=-1, keepdims=True)
    out = d * lax.rsqrt(var + LN_EPS) * g_ref[...] + b_ref[...]
    o_ref[...] = out * _sigmoid(out)


def _conv_branch(u, dw, db, ln_g, ln_b, seg):
    n = u.shape[0]
    return pl.pallas_call(
        functools.partial(_conv_kernel, seg=seg),
        grid=(n // CONV_TILE,),
        in_specs=[pl.BlockSpec((CONV_TILE, CONV_W), lambda i: (i, 0)),
                  pl.BlockSpec((CONV_K, CONV_W), lambda i: (0, 0)),
                  pl.BlockSpec((1, CONV_W), lambda i: (0, 0)),
                  pl.BlockSpec((1, CONV_W), lambda i: (0, 0)),
                  pl.BlockSpec((1, CONV_W), lambda i: (0, 0))],
        out_specs=pl.BlockSpec((CONV_TILE, CONV_W), lambda i: (i, 0)),
        out_shape=jax.ShapeDtypeStruct((n, CONV_W), F32),
        compiler_params=_cparams(1),
        name=f"conv_seg{seg}",
    )(u, dw, db, ln_g, ln_b)


def _prep_kernel(rkv_ref, lora_ref, ones_ref, w0_ref, wup_ref, a0_ref, aup_ref, gup_ref, kk_ref, ka_ref, rk_ref,
                 r_o, kap_o, v_o, w0_o, w1_o, b0_o, b1_o, kd0_o, kd1_o, g_o, bonus_o):
    r = rkv_ref[:, 0:RWKV_W]
    k = rkv_ref[:, RWKV_W:2 * RWKV_W]
    v = rkv_ref[:, 2 * RWKV_W:3 * RWKV_W]
    xw = jnp.tanh(lora_ref[:, 0:2 * LORA]).astype(BF16)
    xa = lora_ref[:, 2 * LORA:4 * LORA].astype(BF16)
    xg = _sigmoid(lora_ref[:, 4 * LORA:4 * LORA + GATE_LORA]).astype(BF16)
    ones = ones_ref[...]

    kk0 = k * kk_ref[...]
    ss = _split_dot(kk0 * kk0, ones)
    kap = kk0 * lax.rsqrt(jnp.maximum(ss, 1e-24))
    r_o[...] = r
    kap_o[...] = kap
    v_o[...] = v
    g_o[...] = jnp.dot(xg, gup_ref[...], preferred_element_type=F32)

    kd_sum = jnp.zeros_like(k)
    for d, (w_o, b_o, kd_o) in enumerate(((w0_o, b0_o, kd0_o), (w1_o, b1_o, kd1_o))):
        wl = w0_ref[d:d + 1, :] + jnp.dot(xw, wup_ref[d], preferred_element_type=F32)
        w_o[...] = -jnp.exp(-0.5) * _sigmoid(wl)
        a = _sigmoid(a0_ref[d:d + 1, :] + jnp.dot(xa, aup_ref[d], preferred_element_type=F32))
        b_o[...] = kap * a
        kd = k * (1.0 + (a - 1.0) * ka_ref[...])
        kd_o[...] = kd
        kd_sum = kd_sum + kd
    bonus_o[...] = _split_dot(r * kd_sum * rk_ref[...], ones) * v


def _rwkv_prep(rkv, lora, ones_blk, w0, w_up, a0, a_up, g_up, k_k, k_a, r_k):
    n = rkv.shape[0]
    row = lambda c: pl.BlockSpec((ROW_TILE, c), lambda i: (i, 0))
    full = lambda a: pl.BlockSpec(a.shape, lambda i: (0,) * a.ndim)
    args = (rkv, lora, ones_blk, w0, w_up, a0, a_up, g_up, k_k, k_a, r_k)
    return pl.pallas_call(
        _prep_kernel,
        grid=(n // ROW_TILE,),
        in_specs=[row(3 * RWKV_W), row(lora.shape[1])] + [full(a) for a in args[2:]],
        out_specs=[row(RWKV_W)] * 11,
        out_shape=[jax.ShapeDtypeStruct((n, RWKV_W), F32)] * 11,
        compiler_params=_cparams(1),
        name="rwkv_prep",
    )(*args)


def _split(x):
    hi = x.astype(BF16)
    return hi, (x - hi.astype(F32)).astype(BF16)


def _dg(a, b, dims):
    return lax.dot_general(a, b, (dims, ((), ())), preferred_element_type=F32)


_NN = ((1,), (0,))
_NT = ((1,), (1,))
_TN = ((0,), (0,))


def _mm(a, b, dims, passes):
    if passes == 1:
        return _dg(a.astype(BF16), b.astype(BF16), dims)
    ah, al = _split(a)
    bh, bl = _split(b)
    return _dg(ah, bh, dims) + _dg(al, bh, dims) + _dg(ah, bl, dims)


def _block_diag(x):
    lane_blk = lax.broadcasted_iota(jnp.int32, x.shape, 1) // HEAD
    return jnp.concatenate([jnp.where(lane_blk == h, x, 0.0) for h in range(HEADS_PER_GROUP)], axis=0)


def _chunk_masks(reverse):
    row = lax.broadcasted_iota(jnp.int32, (CHUNK, GROUP_W), 0)
    col = lax.broadcasted_iota(jnp.int32, (CHUNK, GROUP_W), 1) % CHUNK
    strict = (col > row) if reverse else (col < row)
    incl = (col >= row) if reverse else (col <= row)
    eye_side = jnp.where(col == row, 1.0, 0.0)
    tri = jnp.where(incl[:, :CHUNK], 1.0, 0.0).astype(BF16)
    return strict, incl, eye_side, tri, (0 if reverse else CHUNK - 1)


def _chunk_steps(chains, same_head):
    ln = CHUNK
    each = lambda f, *cols: [f(*args) for args in zip(*cols)]
    r, kap, v, lw, b, kd, n, masks = (list(col) for col in zip(*chains))
    strict = [m[0] for m in masks]
    incl = [m[1] for m in masks]

    def running_sum(lw_c, m):
        hi, lo = _split(lw_c)
        return jnp.dot(m[3], hi, preferred_element_type=F32) + jnp.dot(m[3], lo, preferred_element_type=F32)

    cw = each(running_sum, lw, masks)
    tot = each(lambda cw_c, m: cw_c[m[4]:m[4] + 1, :], cw, masks)
    e_neg = each(lambda cw_c: jnp.exp(-cw_c), cw)
    kr = each(lambda kap_c, r_c, cw_c, lw_c: jnp.concatenate([kap_c * jnp.exp(cw_c - lw_c), r_c * jnp.exp(cw_c)], axis=0),
              kap, r, cw, lw)
    ab = each(lambda kr_c, b_c, e: _mm(kr_c, _block_diag(b_c * e), _NT, PASSES_A), kr, b, e_neg)
    ak = each(lambda kr_c, kd_c, e: _mm(kr_c, _block_diag(kd_c * e), _NT, PASSES_A), kr, kd, e_neg)

    q = each(lambda ab_c, s: jnp.where(s, -ab_c[:ln], 0.0), ab, strict)
    x = each(lambda q_c, m: m[2] + q_c, q, masks)
    for _ in range(CHUNK.bit_length() - 2):
        q = each(lambda q_c: _mm(q_c, _block_diag(q_c), _NN, PASSES_INV), q)
        x = each(lambda x_c, q_c: x_c + _mm(x_c, _block_diag(q_c), _NN, PASSES_INV), x, q)

    a3 = each(lambda ab_c, i: jnp.where(i, ab_c[ln:], 0.0), ab, incl)
    a24 = each(lambda ak_c, s, i: jnp.concatenate([jnp.where(s, ak_c[:ln], 0.0), jnp.where(i, ak_c[ln:], 0.0)], axis=0),
               ak, strict, incl)
    av = each(lambda a_c, v_c: _mm(a_c, _block_diag(v_c), _NN, PASSES_A), a24, v)
    krn = each(lambda kr_c, n_c: _mm(kr_c, n_c, _NT, PASSES_STATE), kr, n)
    u = each(lambda x_c, krn_c, av_c: -_mm(x_c, _block_diag(krn_c[:ln] + av_c[:ln]), _NN, PASSES_INV), x, krn, av)
    y = each(lambda krn_c, a3_c, u_c, av_c: krn_c[ln:] + _mm(a3_c, _block_diag(u_c), _NN, PASSES_STATE) + av_c[ln:],
             krn, a3, u, av)
    e_rest = each(lambda tot_c, cw_c: jnp.exp(tot_c - cw_c), tot, cw)
    upd = each(lambda u_c, v_c, b_c, kd_c, e: _mm(jnp.concatenate([u_c, v_c], axis=0),
                                                   jnp.concatenate([b_c * e, kd_c * e], axis=0), _TN, PASSES_A),
               u, v, b, kd, e_rest)
    n_new = each(lambda n_c, tot_c, upd_c: n_c * jnp.exp(tot_c) + jnp.where(same_head, upd_c, 0.0), n, tot, upd)
    return list(zip(y, n_new))


def _chunk_kernel(*refs, has_init, has_final):
    rf, kf, vf, rb, kb, vb, lwf, bf, kdf, lwb, bb, kdb = refs[:12]
    refs = refs[12:]
    s0 = refs[:2] if has_init else None
    refs = refs[2:] if has_init else refs
    y_refs = refs[:2]
    sfin = refs[2:4] if has_final else None
    n_scr = refs[-1]
    c = pl.program_id(1)

    @pl.when(c == 0)
    def _():
        for d in range(2):
            n_scr[d] = s0[d][0] if has_init else jnp.zeros(n_scr.shape[1:], F32)

    r2 = lax.broadcasted_iota(jnp.int32, (GROUP_W, GROUP_W), 0) // HEAD
    c2 = lax.broadcasted_iota(jnp.int32, (GROUP_W, GROUP_W), 1) // HEAD
    same_head = r2 == c2
    operands = ((rf, kf, vf, lwf, bf, kdf), (rb, kb, vb, lwb, bb, kdb))
    slots = [(d, g) for d in range(2) for g in range(N_HEADS // HEADS_PER_GROUP)]
    lanes = lambda g: slice(g * GROUP_W, (g + 1) * GROUP_W)
    masks = [_chunk_masks(reverse=False), _chunk_masks(reverse=True)]
    chains = [tuple(ref[:, lanes(g)] for ref in operands[d]) + (n_scr[d, g], masks[d]) for d, g in slots]
    for (d, g), (y, n_new) in zip(slots, _chunk_steps(chains, same_head)):
        y_refs[d][:, lanes(g)] = y
        n_scr[d, g] = n_new

    if has_final:
        @pl.when(c == pl.num_programs(1) - 1)
        def _():
            for d in range(2):
                sfin[d][0] = n_scr[d]


def _chunk_scan(r, kap, v, dir_ops, s0, *, row0, n_batch, t_len, want_final):
    n_chunks = t_len // CHUNK
    blk0 = row0 // CHUNK
    n_grp = N_HEADS // HEADS_PER_GROUP
    fwd = lambda bi, ci: bi * n_chunks + ci
    bwd = lambda bi, ci: bi * n_chunks + n_chunks - 1 - ci
    tok_f = pl.BlockSpec((CHUNK, RWKV_W), lambda bi, ci: (blk0 + fwd(bi, ci), 0))
    tok_b = pl.BlockSpec((CHUNK, RWKV_W), lambda bi, ci: (blk0 + bwd(bi, ci), 0))
    state = pl.BlockSpec((1, n_grp, GROUP_W, GROUP_W), lambda bi, ci: (bi, 0, 0, 0))
    has_init = s0 is not None
    y_shape = jax.ShapeDtypeStruct((n_batch * t_len, RWKV_W), F32)
    out_specs = [pl.BlockSpec((CHUNK, RWKV_W), lambda bi, ci: (fwd(bi, ci), 0)),
                 pl.BlockSpec((CHUNK, RWKV_W), lambda bi, ci: (bwd(bi, ci), 0))]
    out_shape = [y_shape, y_shape]
    if want_final:
        out_specs += [state, state]
        out_shape += [jax.ShapeDtypeStruct((n_batch, n_grp, GROUP_W, GROUP_W), F32)] * 2
    args = (r, kap, v, r, kap, v) + tuple(dir_ops[0]) + tuple(dir_ops[1]) + (tuple(s0) if has_init else ())
    return pl.pallas_call(
        functools.partial(_chunk_kernel, has_init=has_init, has_final=want_final),
        grid=(n_batch, n_chunks),
        in_specs=[tok_f] * 3 + [tok_b] * 3 + [tok_f] * 3 + [tok_b] * 3 + ([state, state] if has_init else []),
        out_specs=out_specs,
        out_shape=out_shape,
        scratch_shapes=[pltpu.VMEM((2, n_grp, GROUP_W, GROUP_W), F32)],
        compiler_params=_cparams(2),
        name=f"wkv_chunk_t{t_len}",
    )(*args)


def _postmix_kernel(yf_ref, yb_ref, bonus_ref, g_ref, conv_ref, x_ref, g1_ref, sh2_ref, sc2_ref, gng_ref, gnb_ref,
                    ones_ref, wout_ref, n2g_ref, rhi_ref, rlo_ref, x1_o, h2_o, logit_o):
    ones = ones_ref[...]
    y = yf_ref[...] + yb_ref[...]
    mu = _split_dot(y, ones) * (1.0 / HEAD)
    d = y - mu
    var = _split_dot(d * d, ones) * (1.0 / HEAD)
    yn = d * lax.rsqrt(var + GN_EPS) * gng_ref[...] + gnb_ref[...]
    rw = ((yn + bonus_ref[...]) * g_ref[...]).astype(BF16)
    mix = (jnp.dot(conv_ref[...].astype(BF16), wout_ref[0:CONV_W, :], preferred_element_type=F32)
           + jnp.dot(rw, wout_ref[CONV_W:CONV_W + RWKV_W, :], preferred_element_type=F32))
    x1 = x_ref[...] + g1_ref[0] * mix
    x1_o[...] = x1
    ms = jnp.mean(x1 * x1, axis=-1, keepdims=True)
    h2 = x1 * lax.rsqrt(ms + RMS_EPS) * n2g_ref[...]
    h2 = h2 * (1.0 + sc2_ref[0]) + sh2_ref[0]
    h2_o[...] = h2
    hi = h2.astype(BF16)
    lo = (h2 - hi.astype(F32)).astype(BF16)
    logit_o[...] = (jnp.dot(hi, rhi_ref[...], preferred_element_type=F32)
                    + jnp.dot(lo, rhi_ref[...], preferred_element_type=F32)
                    + jnp.dot(hi, rlo_ref[...], preferred_element_type=F32))


def _postmix(yf, yb, bonus, g, conv, x, mod3, gn_g, gn_b, ones_blk, w_out_bf16, norm2_g, r_hi, r_lo):
    n = x.shape[0]
    row = lambda c: pl.BlockSpec((ROW_TILE, c), lambda i: (i, 0))
    full = lambda a: pl.BlockSpec(a.shape, lambda i: (0,) * a.ndim)
    modspec = lambda j: pl.BlockSpec((1, 1, D_MODEL), lambda i: (_cond_index(i), 0, j))
    return pl.pallas_call(
        _postmix_kernel,
        grid=(n // ROW_TILE,),
        in_specs=[row(RWKV_W)] * 4 + [row(CONV_W), row(D_MODEL), modspec(2), modspec(3), modspec(4),
                                      full(gn_g), full(gn_b), full(ones_blk), full(w_out_bf16), full(norm2_g),
                                      full(r_hi), full(r_lo)],
        out_specs=[row(D_MODEL), row(D_MODEL), row(128)],
        out_shape=[jax.ShapeDtypeStruct((n, D_MODEL), F32), jax.ShapeDtypeStruct((n, D_MODEL), F32),
                   jax.ShapeDtypeStruct((n, 128), F32)],
        compiler_params=_cparams(1),
        name="postmix",
    )(yf, yb, bonus, g, conv, x, mod3, mod3, mod3, gn_g, gn_b, ones_blk, w_out_bf16, norm2_g, r_hi, r_lo)


def _prefix_lanes(m, utri):
    rows = m.shape[0]
    off = jnp.zeros((rows, 1), F32)
    out = []
    for c in range(N_TOK // 128):
        blk = m[:, c * 128:(c + 1) * 128]
        out.append(jnp.dot(blk.astype(BF16), utri, preferred_element_type=F32) + off)
        off = off + jnp.sum(blk, axis=1, keepdims=True)
    return jnp.concatenate(out, axis=1)


def _select_kernel(lt_ref, utri_ref, pos_o, aff_o):
    utri = utri_ref[...]
    for grp in range(2):
        rows = slice(grp * N_EXPERTS, (grp + 1) * N_EXPERTS)
        x = lt_ref[rows, :]
        e = jnp.exp(x - jnp.max(x, axis=0, keepdims=True))
        aff = e / jnp.sum(e, axis=0, keepdims=True)
        aff_o[rows, :] = aff

        def body(_, lh):
            lo, hi = lh
            mid = 0.5 * (lo + hi)
            cnt = jnp.sum(jnp.where(aff > mid, 1.0, 0.0), axis=1, keepdims=True)
            ge = cnt >= CAP
            return jnp.where(ge, mid, lo), jnp.where(ge, hi, mid)

        lo, hi = lax.fori_loop(0, BISECT_ITERS, body,
                               (jnp.full((N_EXPERTS, 1), -1.0, F32), jnp.full((N_EXPERTS, 1), 1.0, F32)))
        gt = jnp.where(aff > hi, 1.0, 0.0)
        tie = jnp.where(aff > lo, 1.0, 0.0) - gt
        need = CAP - jnp.sum(gt, axis=1, keepdims=True)
        sel = gt + tie * jnp.where(_prefix_lanes(tie, utri) < need, 1.0, 0.0)
        pos = _prefix_lanes(sel, utri)
        pos_o[rows, :] = jnp.where(sel > 0.5, pos, -1.0)


def _select(logits_t, utri):
    shp = jax.ShapeDtypeStruct((2 * N_EXPERTS, N_TOK), F32)
    full = lambda a: pl.BlockSpec(a.shape, lambda i: (0,) * a.ndim)
    return pl.pallas_call(
        _select_kernel,
        grid=(1,),
        in_specs=[full(logits_t), full(utri)],
        out_specs=[pl.BlockSpec(shp.shape, lambda i: (0, 0))] * 2,
        out_shape=[shp, shp],
        compiler_params=_cparams(1),
        name="ec_select",
    )(logits_t, utri)


def _compact_kernel(pos_ref, aff_ref, idx_o, gate_o):
    slot = lax.broadcasted_iota(jnp.int32, (CAP, 1), 0).astype(F32)
    idx = jnp.zeros((CAP, 128), F32)
    gate = jnp.zeros((CAP, 128), F32)
    for c in range(N_TOK // 128):
        p = pos_ref[0, :, c * 128:(c + 1) * 128]
        a = aff_ref[0, :, c * 128:(c + 1) * 128]
        tok = (lax.broadcasted_iota(jnp.int32, (1, 128), 1) + c * 128).astype(F32)
        hit = p == slot
        idx = idx + jnp.where(hit, tok, 0.0)
        gate = gate + jnp.where(hit, a, 0.0)
    idx_o[0] = jnp.broadcast_to(jnp.sum(idx, axis=1, keepdims=True), (CAP, 128))
    gate_o[0] = jnp.broadcast_to(jnp.sum(gate, axis=1, keepdims=True), (CAP, 128))


def _compact(pos3, aff3):
    rows = pos3.shape[0]
    shp = jax.ShapeDtypeStruct((rows, CAP, 128), F32)
    return pl.pallas_call(
        _compact_kernel,
        grid=(rows,),
        in_specs=[pl.BlockSpec((1, 1, N_TOK), lambda i: (i, 0, 0))] * 2,
        out_specs=[pl.BlockSpec((1, CAP, 128), lambda i: (i, 0, 0))] * 2,
        out_shape=[shp, shp],
        compiler_params=_cparams(1),
        name="ec_compact",
    )(pos3, aff3)


def _gather_kernel(idx_ref, h_ref, o_ref, tile):
    grp = pl.program_id(0)
    e = pl.program_id(1)
    base = (grp * N_EXPERTS + e) * CAP

    def tok(j8, c):
        for u in range(8):
            j = j8 * 8 + u
            row = pl.multiple_of(idx_ref[base + j] * 8, 8)
            tile[pl.ds(j, 8, stride=TOK_STRIDE), :] = h_ref[0, pl.ds(row, 8), :]
        return c

    lax.fori_loop(0, CAP // 8, tok, 0)
    for c in range(8):
        o_ref[0, :, c * 128:(c + 1) * 128] = tile[c * TOK_STRIDE:c * TOK_STRIDE + CAP, :].astype(BF16)


def _gather(idx_flat, h_rows):
    return pl.pallas_call(
        _gather_kernel,
        grid_spec=pltpu.PrefetchScalarGridSpec(
            num_scalar_prefetch=1,
            grid=(2, N_EXPERTS),
            in_specs=[pl.BlockSpec((1, N_TOK * 8, 128), lambda g, e, idx: (g, 0, 0))],
            out_specs=pl.BlockSpec((1, CAP, D_MODEL), lambda g, e, idx: (e, g, 0)),
            scratch_shapes=[pltpu.VMEM((8 * TOK_STRIDE, 128), F32)]),
        out_shape=jax.ShapeDtypeStruct((N_EXPERTS, 2 * CAP, D_MODEL), BF16),
        compiler_params=_cparams(2),
        name="ec_gather",
    )(idx_flat, h_rows)


def _expert_kernel(xe_ref, w1_ref, w3_ref, w2_ref, gate_ref, o_ref):
    f = pl.program_id(1)
    x = xe_ref[0]
    h1 = jnp.dot(x, w1_ref[0].astype(BF16), preferred_element_type=F32)
    h3 = jnp.dot(x, w3_ref[0].astype(BF16), preferred_element_type=F32)
    hid = (h1 * _sigmoid(h1) * h3).astype(BF16)
    part = jnp.dot(hid, w2_ref[0].astype(BF16), preferred_element_type=F32)

    @pl.when(f == 0)
    def _():
        o_ref[0] = part

    @pl.when(f > 0)
    def _():
        o_ref[0] = o_ref[0] + part

    @pl.when(f == pl.num_programs(1) - 1)
    def _():
        o_ref[0] = o_ref[0] * gate_ref[0]


def _experts(xe, w1, w3, w2, gate_col):
    m = xe.shape[1]
    return pl.pallas_call(
        _expert_kernel,
        grid=(N_EXPERTS, D_FF // FF_TILE),
        in_specs=[pl.BlockSpec((1, m, D_MODEL), lambda e, f: (e, 0, 0)),
                  pl.BlockSpec((1, D_MODEL, FF_TILE), lambda e, f: (e, 0, f)),
                  pl.BlockSpec((1, D_MODEL, FF_TILE), lambda e, f: (e, 0, f)),
                  pl.BlockSpec((1, FF_TILE, D_MODEL), lambda e, f: (e, f, 0)),
                  pl.BlockSpec((1, m, 1), lambda e, f: (e, 0, 0))],
        out_specs=pl.BlockSpec((1, m, D_MODEL), lambda e, f: (e, 0, 0)),
        out_shape=jax.ShapeDtypeStruct((N_EXPERTS, m, D_MODEL), F32),
        compiler_params=_cparams(2),
        name="ec_experts",
    )(xe, w1, w3, w2, gate_col)


def _combine_kernel(idx_ref, ye_ref, o_ref, tile):
    grp = pl.program_id(0)
    e = pl.program_id(1)
    base = (grp * N_EXPERTS + e) * CAP

    @pl.when(e == 0)
    def _():
        o_ref[...] = jnp.zeros_like(o_ref)

    for c in range(8):
        tile[c * TOK_STRIDE:c * TOK_STRIDE + CAP, :] = ye_ref[0, :, c * 128:(c + 1) * 128]

    def tok(j4, c):
        rows, vals = [], []
        for u in range(4):
            j = j4 * 4 + u
            row = pl.multiple_of(idx_ref[base + j] * 8, 8)
            rows.append(row)
            vals.append(o_ref[0, pl.ds(row, 8), :] + tile[pl.ds(j, 8, stride=TOK_STRIDE), :])
        for row, val in zip(rows, vals):
            o_ref[0, pl.ds(row, 8), :] = val
        return c

    lax.fori_loop(0, CAP // 4, tok, 0)


def _combine(idx_flat, ye):
    return pl.pallas_call(
        _combine_kernel,
        grid_spec=pltpu.PrefetchScalarGridSpec(
            num_scalar_prefetch=1,
            grid=(2, N_EXPERTS),
            in_specs=[pl.BlockSpec((1, CAP, D_MODEL), lambda g, e, idx: (e, g, 0))],
            out_specs=pl.BlockSpec((1, N_TOK * 8, 128), lambda g, e, idx: (g, 0, 0)),
            scratch_shapes=[pltpu.VMEM((8 * TOK_STRIDE, 128), F32)]),
        out_shape=jax.ShapeDtypeStruct((2, N_TOK * 8, 128), F32),
        compiler_params=_cparams(2),
        name="ec_combine",
    )(idx_flat, ye)


def _final_kernel(x1_ref, y_ref, g2_ref, fg_ref, o_ref):
    x = x1_ref[...] + g2_ref[0] * y_ref[...]
    ms = jnp.mean(x * x, axis=-1, keepdims=True)
    o_ref[...] = x * lax.rsqrt(ms + RMS_EPS) * fg_ref[...]


def _final(x1, y, mod3, final_g):
    n = x1.shape[0]
    row = pl.BlockSpec((ROW_TILE, D_MODEL), lambda i: (i, 0))
    return pl.pallas_call(
        _final_kernel,
        grid=(n // ROW_TILE,),
        in_specs=[row, row, pl.BlockSpec((1, 1, D_MODEL), lambda i: (_cond_index(i), 0, 5)),
                  pl.BlockSpec((1, D_MODEL), lambda i: (0, 0))],
        out_specs=row,
        out_shape=jax.ShapeDtypeStruct((n, D_MODEL), F32),
        compiler_params=_cparams(1),
        name="final_norm",
    )(x1, y, mod3, final_g)


def kernel(x_prompt, x_sample, state_wkv, c, c_ctx, norm1_g, norm2_g, w_mod, b_mod, w_in, conv_dw, conv_b, conv_ln_g,
           conv_ln_b, rwkv_w0, rwkv_w_up, rwkv_a0, rwkv_a_up, rwkv_g_up, rwkv_k_k, rwkv_k_a, rwkv_r_k, rwkv_gn_g,
           rwkv_gn_b, w_out, router, exp_w1, exp_w3, exp_w2, final_norm_g):
    row2 = lambda a: a.reshape(1, -1)
    x = jnp.concatenate([x_prompt.reshape(N_TOK, D_MODEL), x_sample.reshape(N_TOK, D_MODEL)], axis=0)
    cond8 = jnp.concatenate([c_ctx[None, :], c, jnp.zeros((8 - 1 - DEC_BATCH, D_MODEL), F32)], axis=0)

    mod = _modulation(cond8, w_mod[0], b_mod)
    mod3 = mod.reshape(8, 1, 6 * D_MODEL)

    u, rkv, lora = _inproj(x, mod3, norm1_g, w_in[0].astype(BF16))

    conv_args = (conv_dw[0], conv_b, conv_ln_g, conv_ln_b)
    conv = jnp.concatenate([_conv_branch(u[:N_TOK], *conv_args, seg=SEQ),
                            _conv_branch(u[N_TOK:], *conv_args, seg=GRID_W)], axis=0)

    lane_head = jnp.arange(RWKV_W, dtype=jnp.int32) // HEAD
    ones_blk = (lane_head[:, None] == lane_head[None, :]).astype(BF16)
    def per_dir_rows(up):
        z = jnp.zeros_like(up[0])
        return jnp.stack([jnp.concatenate([up[0], z], axis=0), jnp.concatenate([z, up[1]], axis=0)]).astype(BF16)
    (r, kap, v, w0, w1, b0, b1, kd0, kd1, g, bonus) = _rwkv_prep(
        rkv, lora, ones_blk, rwkv_w0[0], per_dir_rows(rwkv_w_up[0]), rwkv_a0[0], per_dir_rows(rwkv_a_up[0]),
        rwkv_g_up[0].astype(BF16), rwkv_k_k, rwkv_k_a, row2(rwkv_r_k[0]))

    n_grp = N_HEADS // HEADS_PER_GROUP

    def embed_states(s):
        s = s.reshape(-1, n_grp, HEADS_PER_GROUP, HEAD, HEAD)
        z = jnp.zeros(s.shape[:2] + (HEADS_PER_GROUP, HEAD, HEADS_PER_GROUP, HEAD), F32)
        for h in range(HEADS_PER_GROUP):
            z = z.at[:, :, h, :, h, :].set(s[:, :, h])
        return z.reshape(-1, n_grp, GROUP_W, GROUP_W)

    def extract_states(z):
        z = z.reshape(-1, n_grp, HEADS_PER_GROUP, HEAD, HEADS_PER_GROUP, HEAD)
        return jnp.stack([z[:, :, h, :, h, :] for h in range(HEADS_PER_GROUP)], axis=2).reshape(
            -1, N_HEADS, HEAD, HEAD)

    dir_ops = ((w0, b0, kd0), (w1, b1, kd1))
    yp_f, yp_b, sp_f, sp_b = _chunk_scan(r, kap, v, dir_ops, None, row0=0, n_batch=BATCH, t_len=SEQ, want_final=True)
    ys_f, ys_b = _chunk_scan(r, kap, v, dir_ops, (embed_states(state_wkv[:, 0, 0]), embed_states(state_wkv[:, 0, 1])),
                             row0=N_TOK, n_batch=DEC_BATCH, t_len=DEC_SEQ, want_final=False)
    y_f = jnp.concatenate([yp_f, ys_f], axis=0)
    y_b = jnp.concatenate([yp_b, ys_b], axis=0)
    new_state = jnp.stack([extract_states(sp_f), extract_states(sp_b)], axis=1)[:, None]

    r_pad = jnp.pad(router[0], ((0, 0), (0, 128 - N_EXPERTS)))
    r_hi = r_pad.astype(BF16)
    r_lo = (r_pad - r_hi.astype(F32)).astype(BF16)
    x1, h2, logits = _postmix(y_f, y_b, bonus, g, conv, x, mod3, rwkv_gn_g, rwkv_gn_b, ones_blk,
                              w_out[0].astype(BF16), norm2_g, r_hi, r_lo)

    logits_t = logits[:, :N_EXPERTS].reshape(2, N_TOK, N_EXPERTS).transpose(0, 2, 1).reshape(2 * N_EXPERTS, N_TOK)
    lane = jnp.arange(128, dtype=jnp.int32)
    utri = (lane[:, None] < lane[None, :]).astype(BF16)
    pos, aff = _select(logits_t, utri)
    idx_c, gate_c = _compact(pos.reshape(2 * N_EXPERTS, 1, N_TOK), aff.reshape(2 * N_EXPERTS, 1, N_TOK))
    idx_flat = idx_c[:, :, 0].astype(jnp.int32).reshape(-1)
    gate_col = gate_c[:, :, 0].reshape(2, N_EXPERTS, CAP).transpose(1, 0, 2).reshape(N_EXPERTS, 2 * CAP, 1)

    xe = _gather(idx_flat, h2.reshape(2, N_TOK * 8, 128))
    ye = _experts(xe, exp_w1[0], exp_w3[0], exp_w2[0], gate_col)
    y_moe = _combine(idx_flat, ye).reshape(2 * N_TOK, D_MODEL)

    out = _final(x1, y_moe, mod3, row2(final_norm_g))
    return (out[:N_TOK].reshape(BATCH, SEQ, D_MODEL), out[N_TOK:].reshape(DEC_BATCH, DEC_SEQ, D_MODEL), new_state)
```

```python
import functools

import jax
import jax.numpy as jnp
from jax import lax
from jax.experimental import pallas as pl
from jax.experimental.pallas import tpu as pltpu

F32 = jnp.float32
BF16 = jnp.bfloat16

D_MODEL = 1024
BATCH = 16
SEQ = 256
DEC_BATCH = 4
DEC_SEQ = 1024
GRID_W = 64
CONV_W = 512
CONV_K = 31
CONV_PAD = 15
RWKV_W = 512
HEAD = 64
N_HEADS = 8
LORA = 64
GATE_LORA = 128
N_EXPERTS = 16
D_FF = 2816
D_IN = 2944
CAP = 512
N_TOK = 4096
RMS_EPS = 1e-6
LN_EPS = 1e-5
GN_EPS = 64e-5

ROW_TILE = 512
CONV_TILE = 256
FF_TILE = 256
N_FF = D_FF // FF_TILE
OUT_TILE = 256
N_OUT = D_MODEL // OUT_TILE
CHUNK = 64
HEADS_PER_GROUP = 4
GROUP_W = HEADS_PER_GROUP * HEAD
PASSES_A = 3
PASSES_INV = 1
PASSES_STATE = 1
SCAN_BATCHES = 2
TOK_STRIDE = CAP + 8
BISECT_ITERS = 160
VMEM_LIMIT = 56 * 1024 * 1024


def _cparams(n_axes, vmem=None):
    return pltpu.CompilerParams(dimension_semantics=("arbitrary",) * n_axes,
                                vmem_limit_bytes=vmem or VMEM_LIMIT)


def _cond_index(i):
    n_prompt = (BATCH * SEQ) // ROW_TILE
    per_batch = DEC_SEQ // ROW_TILE
    return jnp.where(i < n_prompt, 0, 1 + (i - n_prompt) // per_batch)


def _split_dot(x, w_bf16):
    hi = x.astype(BF16)
    lo = (x - hi.astype(F32)).astype(BF16)
    return (jnp.dot(hi, w_bf16, preferred_element_type=F32)
            + jnp.dot(lo, w_bf16, preferred_element_type=F32))


def _sigmoid(x):
    return 1.0 / (1.0 + jnp.exp(-x))


def _mod_kernel(c_ref, w_ref, b_ref, o_ref):
    c = c_ref[...]
    s = (c * _sigmoid(c)).astype(BF16)
    o_ref[...] = jnp.dot(s, w_ref[...].astype(BF16), preferred_element_type=F32) + b_ref[...]


def _modulation(cond8, w_mod, b_mod):
    n = 6 * D_MODEL
    tn = D_MODEL
    return pl.pallas_call(
        _mod_kernel,
        grid=(n // tn,),
        in_specs=[pl.BlockSpec((8, D_MODEL), lambda j: (0, 0)),
                  pl.BlockSpec((D_MODEL, tn), lambda j: (0, j)),
                  pl.BlockSpec((1, tn), lambda j: (0, j))],
        out_specs=pl.BlockSpec((8, tn), lambda j: (0, j)),
        out_shape=jax.ShapeDtypeStruct((8, n), F32),
        compiler_params=_cparams(1),
        name="modulation",
    )(cond8, w_mod, b_mod)


def _inproj_kernel(x_ref, sh_ref, sc_ref, g_ref, w_ref, u_ref, rkv_ref, lora_ref):
    x = x_ref[...]
    ms = jnp.mean(x * x, axis=-1, keepdims=True)
    h = x * lax.rsqrt(ms + RMS_EPS) * g_ref[...]
    h = (h * (1.0 + sc_ref[0]) + sh_ref[0]).astype(BF16)
    uv = jnp.dot(h, w_ref[:, 0:CONV_W], preferred_element_type=F32)
    ug = jnp.dot(h, w_ref[:, CONV_W:2 * CONV_W], preferred_element_type=F32)
    u_ref[...] = uv * _sigmoid(ug)
    rkv_ref[...] = jnp.dot(h, w_ref[:, 2 * CONV_W:2 * CONV_W + 3 * RWKV_W], preferred_element_type=F32)
    lora_ref[...] = jnp.dot(h, w_ref[:, 2 * CONV_W + 3 * RWKV_W:D_IN], preferred_element_type=F32)


def _inproj(x, mod3, norm1_g, w_in_bf16):
    n = x.shape[0]
    n_lora = D_IN - 2 * CONV_W - 3 * RWKV_W
    return pl.pallas_call(
        _inproj_kernel,
        grid=(n // ROW_TILE,),
        in_specs=[pl.BlockSpec((ROW_TILE, D_MODEL), lambda i: (i, 0)),
                  pl.BlockSpec((1, 1, D_MODEL), lambda i: (_cond_index(i), 0, 0)),
                  pl.BlockSpec((1, 1, D_MODEL), lambda i: (_cond_index(i), 0, 1)),
                  pl.BlockSpec((1, D_MODEL), lambda i: (0, 0)),
                  pl.BlockSpec((D_MODEL, D_IN), lambda i: (0, 0))],
        out_specs=[pl.BlockSpec((ROW_TILE, CONV_W), lambda i: (i, 0)),
                   pl.BlockSpec((ROW_TILE, 3 * RWKV_W), lambda i: (i, 0)),
                   pl.BlockSpec((ROW_TILE, n_lora), lambda i: (i, 0))],
        out_shape=[jax.ShapeDtypeStruct((n, CONV_W), F32),
                   jax.ShapeDtypeStruct((n, 3 * RWKV_W), F32),
                   jax.ShapeDtypeStruct((n, n_lora), F32)],
        compiler_params=_cparams(1),
        name="inproj",
    )(x, mod3, mod3, norm1_g, w_in_bf16)


def _conv_kernel(u_ref, dw_ref, db_ref, g_ref, b_ref, o_ref, *, seg):
    u = u_ref[...]
    pos = lax.broadcasted_iota(jnp.int32, (CONV_TILE, 1), 0) % seg
    acc = jnp.zeros_like(u)
    for k in range(CONV_K):
        s = k - CONV_PAD
        shifted = u if s == 0 else pltpu.roll(u, (-s) % CONV_TILE, 0)
        valid = (pos + s >= 0) & (pos + s < seg)
        acc = acc + jnp.where(valid, shifted, 0.0) * dw_ref[k:k + 1, :]
    out = acc + db_ref[...]
    mu = jnp.mean(out, axis=-1, keepdims=True)
    d = out - mu
    var = jnp.mean(d * d, axis=-1, keepdims=True)
    out = d * lax.rsqrt(var + LN_EPS) * g_ref[...] + b_ref[...]
    o_ref[...] = out * _sigmoid(out)


def _conv_branch(u, dw, db, ln_g, ln_b, seg):
    n = u.shape[0]
    return pl.pallas_call(
        functools.partial(_conv_kernel, seg=seg),
        grid=(n // CONV_TILE,),
        in_specs=[pl.BlockSpec((CONV_TILE, CONV_W), lambda i: (i, 0)),
                  pl.BlockSpec((CONV_K, CONV_W), lambda i: (0, 0)),
                  pl.BlockSpec((1, CONV_W), lambda i: (0, 0)),
                  pl.BlockSpec((1, CONV_W), lambda i: (0, 0)),
                  pl.BlockSpec((1, CONV_W), lambda i: (0, 0))],
        out_specs=pl.BlockSpec((CONV_TILE, CONV_W), lambda i: (i, 0)),
        out_shape=jax.ShapeDtypeStruct((n, CONV_W), F32),
        compiler_params=_cparams(1),
        name=f"conv_seg{seg}",
    )(u, dw, db, ln_g, ln_b)


def _prep_kernel(rkv_ref, lora_ref, ones_ref, w0_ref, wup_ref, a0_ref, aup_ref, gup_ref, kk_ref, ka_ref, rk_ref,
                 r_o, kap_o, v_o, w0_o, w1_o, b0_o, b1_o, kd0_o, kd1_o, g_o, bonus_o):
    r = rkv_ref[:, 0:RWKV_W]
    k = rkv_ref[:, RWKV_W:2 * RWKV_W]
    v = rkv_ref[:, 2 * RWKV_W:3 * RWKV_W]
    xw = jnp.tanh(lora_ref[:, 0:2 * LORA]).astype(BF16)
    xa = lora_ref[:, 2 * LORA:4 * LORA].astype(BF16)
    xg = _sigmoid(lora_ref[:, 4 * LORA:4 * LORA + GATE_LORA]).astype(BF16)
    ones = ones_ref[...]

    kk0 = k * kk_ref[...]
    ss = _split_dot(kk0 * kk0, ones)
    kap = kk0 * lax.rsqrt(jnp.maximum(ss, 1e-24))
    r_o[...] = r
    kap_o[...] = kap
    v_o[...] = v
    g_o[...] = jnp.dot(xg, gup_ref[...], preferred_element_type=F32)

    kd_sum = jnp.zeros_like(k)
    for d, (w_o, b_o, kd_o) in enumerate(((w0_o, b0_o, kd0_o), (w1_o, b1_o, kd1_o))):
        wl = w0_ref[d:d + 1, :] + jnp.dot(xw, wup_ref[d], preferred_element_type=F32)
        w_o[...] = -jnp.exp(-0.5) * _sigmoid(wl)
        a = _sigmoid(a0_ref[d:d + 1, :] + jnp.dot(xa, aup_ref[d], preferred_element_type=F32))
        b_o[...] = kap * a
        kd = k * (1.0 + (a - 1.0) * ka_ref[...])
        kd_o[...] = kd
        kd_sum = kd_sum + kd
    bonus_o[...] = _split_dot(r * kd_sum * rk_ref[...], ones) * v


def _rwkv_prep(rkv, lora, ones_blk, w0, w_up, a0, a_up, g_up, k_k, k_a, r_k):
    n = rkv.shape[0]
    row = lambda c: pl.BlockSpec((ROW_TILE, c), lambda i: (i, 0))
    full = lambda a: pl.BlockSpec(a.shape, lambda i: (0,) * a.ndim)
    args = (rkv, lora, ones_blk, w0, w_up, a0, a_up, g_up, k_k, k_a, r_k)
    return pl.pallas_call(
        _prep_kernel,
        grid=(n // ROW_TILE,),
        in_specs=[row(3 * RWKV_W), row(lora.shape[1])] + [full(a) for a in args[2:]],
        out_specs=[row(RWKV_W)] * 11,
        out_shape=[jax.ShapeDtypeStruct((n, RWKV_W), F32)] * 11,
        compiler_params=_cparams(1),
        name="rwkv_prep",
    )(*args)


def _pieces(x, passes):
    hi = x.astype(BF16)
    return (hi,) if passes == 1 else (hi, (x - hi.astype(F32)).astype(BF16))


_NN = ((1,), (0,))
_NT = ((1,), (1,))
_TN = ((0,), (0,))


def _mm(a, b, dims):
    dg = lambda p, q: lax.dot_general(p, q, (dims, ((), ())), preferred_element_type=F32)
    out = dg(a[0], b[0])
    if len(a) > 1:
        out = out + dg(a[1], b[0])
    if len(b) > 1:
        out = out + dg(a[0], b[1])
    return out


def _block_diag(pieces, head_masks):
    return tuple(jnp.concatenate([p * m for m in head_masks], axis=0) for p in pieces)


def _chunk_masks(reverse):
    row = lax.broadcasted_iota(jnp.int32, (CHUNK, GROUP_W), 0)
    col = lax.broadcasted_iota(jnp.int32, (CHUNK, GROUP_W), 1) % CHUNK
    strict = (col > row) if reverse else (col < row)
    incl = (col >= row) if reverse else (col <= row)
    eye_side = jnp.where(col == row, 1.0, 0.0)
    tri = jnp.where(incl[:, :CHUNK], 1.0, 0.0).astype(BF16)
    return strict, incl, eye_side, tri, (0 if reverse else CHUNK - 1)


def _chunk_steps(chains):
    ln = CHUNK
    each = lambda f, *cols: [f(*args) for args in zip(*cols)]
    r, kap, v, lw, b, kd, n, masks = (list(col) for col in zip(*chains))
    strict = [m[0] for m in masks]
    incl = [m[1] for m in masks]
    lane_blk = lax.broadcasted_iota(jnp.int32, (ln, GROUP_W), 1) // HEAD
    head_masks = [jnp.where(lane_blk == h, 1.0, 0.0).astype(BF16) for h in range(HEADS_PER_GROUP)]
    bd = lambda x, passes: _block_diag(_pieces(x, passes), head_masks)
    r2 = lax.broadcasted_iota(jnp.int32, (GROUP_W, GROUP_W), 0) // HEAD
    c2 = lax.broadcasted_iota(jnp.int32, (GROUP_W, GROUP_W), 1) // HEAD
    same_head = r2 == c2

    def running_sum(lw_c, m):
        return _mm((m[3],), _pieces(lw_c, 3), _NN)

    cw = each(running_sum, lw, masks)
    tot = each(lambda cw_c, m: cw_c[m[4]:m[4] + 1, :], cw, masks)
    e_neg = each(lambda cw_c: jnp.exp(-cw_c), cw)
    kr = each(lambda kap_c, r_c, cw_c, lw_c: _pieces(
        jnp.concatenate([kap_c * jnp.exp(cw_c - lw_c), r_c * jnp.exp(cw_c)], axis=0), PASSES_A),
        kap, r, cw, lw)
    ab = each(lambda kr_c, b_c, e: _mm(kr_c, bd(b_c * e, PASSES_A), _NT), kr, b, e_neg)
    ak = each(lambda kr_c, kd_c, e: _mm(kr_c, bd(kd_c * e, PASSES_A), _NT), kr, kd, e_neg)

    q = each(lambda ab_c, s: jnp.where(s, -ab_c[:ln], 0.0), ab, strict)
    x = each(lambda q_c, m: m[2] + q_c, q, masks)
    for _ in range(CHUNK.bit_length() - 2):
        qp = each(lambda q_c: _pieces(q_c, PASSES_INV), q)
        q = each(lambda qp_c: _mm(qp_c, _block_diag(qp_c, head_masks), _NN), qp)
        x = each(lambda x_c, q_c: x_c + _mm(_pieces(x_c, PASSES_INV), bd(q_c, PASSES_INV), _NN), x, q)

    a3 = each(lambda ab_c, i: jnp.where(i, ab_c[ln:], 0.0), ab, incl)
    a24 = each(lambda ak_c, s, i: jnp.concatenate([jnp.where(s, ak_c[:ln], 0.0), jnp.where(i, ak_c[ln:], 0.0)], axis=0),
               ak, strict, incl)
    av = each(lambda a_c, v_c: _mm(_pieces(a_c, PASSES_A), bd(v_c, PASSES_A), _NN), a24, v)
    krn = each(lambda kr_c, n_c: _mm(kr_c[:PASSES_STATE], _pieces(n_c, PASSES_STATE), _NT), kr, n)
    u = each(lambda x_c, krn_c, av_c: -_mm(_pieces(x_c, PASSES_INV), bd(krn_c[:ln] + av_c[:ln], PASSES_INV), _NN),
             x, krn, av)
    y = each(lambda krn_c, a3_c, u_c, av_c: krn_c[ln:] + _mm(_pieces(a3_c, PASSES_STATE), bd(u_c, PASSES_STATE), _NN)
             + av_c[ln:], krn, a3, u, av)
    e_rest = each(lambda tot_c, cw_c: jnp.exp(tot_c - cw_c), tot, cw)
    upd = each(lambda u_c, v_c, b_c, kd_c, e: _mm(_pieces(jnp.concatenate([u_c, v_c], axis=0), PASSES_A),
                                                   _pieces(jnp.concatenate([b_c * e, kd_c * e], axis=0), PASSES_A), _TN),
               u, v, b, kd, e_rest)
    n_new = each(lambda n_c, tot_c, upd_c: n_c * jnp.exp(tot_c) + jnp.where(same_head, upd_c, 0.0), n, tot, upd)
    return list(zip(y, n_new))


def _chunk_kernel(*refs, has_init, has_final):
    n_tok = 12 * SCAN_BATCHES
    tok_refs = [refs[k * 12:(k + 1) * 12] for k in range(SCAN_BATCHES)]
    refs = refs[n_tok:]
    s0 = refs[:2] if has_init else None
    refs = refs[2:] if has_init else refs
    y_refs = refs[:2]
    sfin = refs[2:4] if has_final else None
    n_scr = refs[-1]
    c = pl.program_id(1)

    @pl.when(c == 0)
    def _():
        for d in range(2):
            n_scr[d] = s0[d][...] if has_init else jnp.zeros(n_scr.shape[1:], F32)

    slots = [(k, d, g) for k in range(SCAN_BATCHES) for d in range(2) for g in range(N_HEADS // HEADS_PER_GROUP)]
    lanes = lambda g: slice(g * GROUP_W, (g + 1) * GROUP_W)
    masks = [_chunk_masks(reverse=False), _chunk_masks(reverse=True)]

    def operands(k, d, g):
        rf, kf, vf, rb, kb, vb, lwf, bf, kdf, lwb, bb, kdb = tok_refs[k]
        group = (rb, kb, vb, lwb, bb, kdb) if d else (rf, kf, vf, lwf, bf, kdf)
        return tuple(ref[:, lanes(g)] for ref in group)

    chains = [operands(k, d, g) + (n_scr[d, k, g], masks[d]) for k, d, g in slots]
    for (k, d, g), (y, n_new) in zip(slots, _chunk_steps(chains)):
        y_refs[d][k, :, lanes(g)] = y
        n_scr[d, k, g] = n_new

    if has_final:
        @pl.when(c == pl.num_programs(1) - 1)
        def _():
            for d in range(2):
                sfin[d][...] = n_scr[d]


def _chunk_scan(r, kap, v, dir_ops, s0, *, row0, n_batch, t_len, want_final):
    n_chunks = t_len // CHUNK
    blk0 = row0 // CHUNK
    n_grp = N_HEADS // HEADS_PER_GROUP
    nb = SCAN_BATCHES

    def tok_spec(k, reverse):
        def index(bi, ci):
            return (blk0 + (bi * nb + k) * n_chunks + (n_chunks - 1 - ci if reverse else ci), 0)
        return pl.BlockSpec((CHUNK, RWKV_W), index)

    in_specs, args = [], []
    for k in range(nb):
        in_specs += [tok_spec(k, False)] * 3 + [tok_spec(k, True)] * 3 + [tok_spec(k, False)] * 3 + [tok_spec(k, True)] * 3
        args += [r, kap, v, r, kap, v, *dir_ops[0], *dir_ops[1]]
    state = pl.BlockSpec((nb, n_grp, GROUP_W, GROUP_W), lambda bi, ci: (bi, 0, 0, 0))
    has_init = s0 is not None
    if has_init:
        in_specs += [state, state]
        args += list(s0)
    y_shape = jax.ShapeDtypeStruct((n_batch, t_len, RWKV_W), F32)
    out_specs = [pl.BlockSpec((nb, CHUNK, RWKV_W), lambda bi, ci: (bi, ci, 0)),
                 pl.BlockSpec((nb, CHUNK, RWKV_W), lambda bi, ci: (bi, n_chunks - 1 - ci, 0))]
    out_shape = [y_shape, y_shape]
    if want_final:
        out_specs += [state, state]
        out_shape += [jax.ShapeDtypeStruct((n_batch, n_grp, GROUP_W, GROUP_W), F32)] * 2
    outs = pl.pallas_call(
        functools.partial(_chunk_kernel, has_init=has_init, has_final=want_final),
        grid=(n_batch // nb, n_chunks),
        in_specs=in_specs,
        out_specs=out_specs,
        out_shape=out_shape,
        scratch_shapes=[pltpu.VMEM((2, nb, n_grp, GROUP_W, GROUP_W), F32)],
        compiler_params=_cparams(2),
        name=f"wkv_chunk_t{t_len}",
    )(*args)
    return [outs[0].reshape(n_batch * t_len, RWKV_W), outs[1].reshape(n_batch * t_len, RWKV_W)] + list(outs[2:])


def _postmix_kernel(yf_ref, yb_ref, bonus_ref, g_ref, conv_ref, x_ref, g1_ref, sh2_ref, sc2_ref, gng_ref, gnb_ref,
                    ones_ref, wout_ref, n2g_ref, rhi_ref, rlo_ref, x1_o, h2_o, logit_o):
    ones = ones_ref[...]
    y = yf_ref[...] + yb_ref[...]
    mu = _split_dot(y, ones) * (1.0 / HEAD)
    d = y - mu
    var = _split_dot(d * d, ones) * (1.0 / HEAD)
    yn = d * lax.rsqrt(var + GN_EPS) * gng_ref[...] + gnb_ref[...]
    rw = ((yn + bonus_ref[...]) * g_ref[...]).astype(BF16)
    mix = (jnp.dot(conv_ref[...].astype(BF16), wout_ref[0:CONV_W, :], preferred_element_type=F32)
           + jnp.dot(rw, wout_ref[CONV_W:CONV_W + RWKV_W, :], preferred_element_type=F32))
    x1 = x_ref[...] + g1_ref[0] * mix
    x1_o[...] = x1
    ms = jnp.mean(x1 * x1, axis=-1, keepdims=True)
    h2 = x1 * lax.rsqrt(ms + RMS_EPS) * n2g_ref[...]
    h2 = h2 * (1.0 + sc2_ref[0]) + sh2_ref[0]
    h2_o[...] = h2
    hi = h2.astype(BF16)
    lo = (h2 - hi.astype(F32)).astype(BF16)
    logit_o[...] = (jnp.dot(hi, rhi_ref[...], preferred_element_type=F32)
                    + jnp.dot(lo, rhi_ref[...], preferred_element_type=F32)
                    + jnp.dot(hi, rlo_ref[...], preferred_element_type=F32))


def _postmix(yf, yb, bonus, g, conv, x, mod3, gn_g, gn_b, ones_blk, w_out_bf16, norm2_g, r_hi, r_lo):
    n = x.shape[0]
    row = lambda c: pl.BlockSpec((ROW_TILE, c), lambda i: (i, 0))
    full = lambda a: pl.BlockSpec(a.shape, lambda i: (0,) * a.ndim)
    modspec = lambda j: pl.BlockSpec((1, 1, D_MODEL), lambda i: (_cond_index(i), 0, j))
    return pl.pallas_call(
        _postmix_kernel,
        grid=(n // ROW_TILE,),
        in_specs=[row(RWKV_W)] * 4 + [row(CONV_W), row(D_MODEL), modspec(2), modspec(3), modspec(4),
                                      full(gn_g), full(gn_b), full(ones_blk), full(w_out_bf16), full(norm2_g),
                                      full(r_hi), full(r_lo)],
        out_specs=[row(D_MODEL), row(D_MODEL), row(128)],
        out_shape=[jax.ShapeDtypeStruct((n, D_MODEL), F32), jax.ShapeDtypeStruct((n, D_MODEL), F32),
                   jax.ShapeDtypeStruct((n, 128), F32)],
        compiler_params=_cparams(1),
        name="postmix",
    )(yf, yb, bonus, g, conv, x, mod3, mod3, mod3, gn_g, gn_b, ones_blk, w_out_bf16, norm2_g, r_hi, r_lo)


def _prefix_lanes(m, utri):
    rows = m.shape[0]
    off = jnp.zeros((rows, 1), F32)
    out = []
    for c in range(N_TOK // 128):
        blk = m[:, c * 128:(c + 1) * 128]
        out.append(jnp.dot(blk.astype(BF16), utri, preferred_element_type=F32) + off)
        off = off + jnp.sum(blk, axis=1, keepdims=True)
    return jnp.concatenate(out, axis=1)


def _select_kernel(lt_ref, utri_ref, pos_o, aff_o):
    utri = utri_ref[...]
    affs = []
    for grp in range(2):
        x = lt_ref[grp * N_EXPERTS:(grp + 1) * N_EXPERTS, :]
        e = jnp.exp(x - jnp.max(x, axis=0, keepdims=True))
        affs.append(e / jnp.sum(e, axis=0, keepdims=True))
    aff = jnp.concatenate(affs, axis=0)
    aff_o[...] = aff
    n_rows = 2 * N_EXPERTS

    def body(_, lh):
        lo, hi = lh
        m2 = 0.5 * (lo + hi)
        m1 = 0.5 * (lo + m2)
        m3 = 0.5 * (m2 + hi)
        ge1, ge2, ge3 = (jnp.sum(jnp.where(aff > m, 1.0, 0.0), axis=1, keepdims=True) >= CAP for m in (m1, m2, m3))
        lo = jnp.where(ge3, m3, jnp.where(ge2, m2, jnp.where(ge1, m1, lo)))
        hi = jnp.where(ge3, hi, jnp.where(ge2, m3, jnp.where(ge1, m2, m1)))
        return lo, hi

    lo, hi = lax.fori_loop(0, BISECT_ITERS // 2, body,
                           (jnp.full((n_rows, 1), -1.0, F32), jnp.full((n_rows, 1), 1.0, F32)))
    gt = jnp.where(aff > hi, 1.0, 0.0)
    tie = jnp.where(aff > lo, 1.0, 0.0) - gt
    need = CAP - jnp.sum(gt, axis=1, keepdims=True)
    sel = gt + tie * jnp.where(_prefix_lanes(tie, utri) < need, 1.0, 0.0)
    pos = _prefix_lanes(sel, utri)
    pos_o[...] = jnp.where(sel > 0.5, pos, -1.0)


def _select(logits_t, utri):
    shp = jax.ShapeDtypeStruct((2 * N_EXPERTS, N_TOK), F32)
    full = lambda a: pl.BlockSpec(a.shape, lambda i: (0,) * a.ndim)
    return pl.pallas_call(
        _select_kernel,
        grid=(1,),
        in_specs=[full(logits_t), full(utri)],
        out_specs=[pl.BlockSpec(shp.shape, lambda i: (0, 0))] * 2,
        out_shape=[shp, shp],
        compiler_params=_cparams(1),
        name="ec_select",
    )(logits_t, utri)


def _compact_kernel(pos_ref, aff_ref, idx_o, gate_o):
    slot = lax.broadcasted_iota(jnp.int32, (CAP, 1), 0).astype(F32)
    idx = jnp.zeros((CAP, 128), F32)
    gate = jnp.zeros((CAP, 128), F32)
    for c in range(N_TOK // 128):
        p = pos_ref[0, :, c * 128:(c + 1) * 128]
        a = aff_ref[0, :, c * 128:(c + 1) * 128]
        tok = (lax.broadcasted_iota(jnp.int32, (1, 128), 1) + c * 128).astype(F32)
        hit = p == slot
        idx = idx + jnp.where(hit, tok, 0.0)
        gate = gate + jnp.where(hit, a, 0.0)
    idx_o[0] = jnp.broadcast_to(jnp.sum(idx, axis=1, keepdims=True), (CAP, 128))
    gate_o[0] = jnp.broadcast_to(jnp.sum(gate, axis=1, keepdims=True), (CAP, 128))


def _compact(pos3, aff3):
    rows = pos3.shape[0]
    shp = jax.ShapeDtypeStruct((rows, CAP, 128), F32)
    return pl.pallas_call(
        _compact_kernel,
        grid=(rows,),
        in_specs=[pl.BlockSpec((1, 1, N_TOK), lambda i: (i, 0, 0))] * 2,
        out_specs=[pl.BlockSpec((1, CAP, 128), lambda i: (i, 0, 0))] * 2,
        out_shape=[shp, shp],
        compiler_params=_cparams(1),
        name="ec_compact",
    )(pos3, aff3)


def _gather_kernel(idx_ref, h_ref, o_ref, tile):
    grp = pl.program_id(0)
    e = pl.program_id(1)
    base = (grp * N_EXPERTS + e) * CAP

    def tok(j8, c):
        for u in range(8):
            j = j8 * 8 + u
            row = pl.multiple_of(idx_ref[base + j] * 8, 8)
            tile[pl.ds(j, 8, stride=TOK_STRIDE), :] = h_ref[0, pl.ds(row, 8), :]
        return c

    lax.fori_loop(0, CAP // 8, tok, 0)
    for c in range(8):
        o_ref[0, :, c * 128:(c + 1) * 128] = tile[c * TOK_STRIDE:c * TOK_STRIDE + CAP, :].astype(BF16)


def _gather(idx_flat, h_rows):
    return pl.pallas_call(
        _gather_kernel,
        grid_spec=pltpu.PrefetchScalarGridSpec(
            num_scalar_prefetch=1,
            grid=(2, N_EXPERTS),
            in_specs=[pl.BlockSpec((1, N_TOK * 8, 128), lambda g, e, idx: (g, 0, 0))],
            out_specs=pl.BlockSpec((1, CAP, D_MODEL), lambda g, e, idx: (e, g, 0)),
            scratch_shapes=[pltpu.VMEM((8 * TOK_STRIDE, 128), F32)]),
        out_shape=jax.ShapeDtypeStruct((N_EXPERTS, 2 * CAP, D_MODEL), BF16),
        compiler_params=_cparams(2),
        name="ec_gather",
    )(idx_flat, h_rows)


def _expert_kernel(xe_ref, w1_ref, w3_ref, w2_ref, gate_ref, o_ref, hid_scr):
    s = pl.program_id(1)

    @pl.when(s < N_FF)
    def _():
        x = xe_ref[0]
        h1 = jnp.dot(x, w1_ref[0].astype(BF16), preferred_element_type=F32)
        h3 = jnp.dot(x, w3_ref[0].astype(BF16), preferred_element_type=F32)
        hid_scr[s] = (h1 * _sigmoid(h1) * h3).astype(BF16)

    @pl.when(s >= N_FF)
    def _():
        acc = jnp.dot(hid_scr[0], w2_ref[0, 0:FF_TILE, :].astype(BF16), preferred_element_type=F32)
        for f in range(1, N_FF):
            acc = acc + jnp.dot(hid_scr[f], w2_ref[0, f * FF_TILE:(f + 1) * FF_TILE, :].astype(BF16),
                                preferred_element_type=F32)
        o_ref[0] = acc * gate_ref[0]


def _experts(xe, w1, w3, w2, gate_col):
    m = xe.shape[1]
    up = lambda e, s: (e, 0, jnp.minimum(s, N_FF - 1))
    down = lambda e, s: (e, 0, jnp.maximum(s - N_FF, 0))
    return pl.pallas_call(
        _expert_kernel,
        grid=(N_EXPERTS, N_FF + N_OUT),
        in_specs=[pl.BlockSpec((1, m, D_MODEL), lambda e, s: (e, 0, 0)),
                  pl.BlockSpec((1, D_MODEL, FF_TILE), up),
                  pl.BlockSpec((1, D_MODEL, FF_TILE), up),
                  pl.BlockSpec((1, D_FF, OUT_TILE), down),
                  pl.BlockSpec((1, m, 1), lambda e, s: (e, 0, 0))],
        out_specs=pl.BlockSpec((1, m, OUT_TILE), down),
        out_shape=jax.ShapeDtypeStruct((N_EXPERTS, m, D_MODEL), F32),
        scratch_shapes=[pltpu.VMEM((N_FF, m, FF_TILE), BF16)],
        compiler_params=_cparams(2),
        name="ec_experts",
    )(xe, w1, w3, w2, gate_col)


def _combine_kernel(idx_ref, ye_ref, o_ref, tile):
    grp = pl.program_id(0)
    e = pl.program_id(1)
    base = (grp * N_EXPERTS + e) * CAP

    @pl.when(e == 0)
    def _():
        o_ref[...] = jnp.zeros_like(o_ref)

    for c in range(8):
        tile[c * TOK_STRIDE:c * TOK_STRIDE + CAP, :] = ye_ref[0, :, c * 128:(c + 1) * 128]

    def tok(j4, c):
        rows, vals = [], []
        for u in range(4):
            j = j4 * 4 + u
            row = pl.multiple_of(idx_ref[base + j] * 8, 8)
            rows.append(row)
            vals.append(o_ref[0, pl.ds(row, 8), :] + tile[pl.ds(j, 8, stride=TOK_STRIDE), :])
        for row, val in zip(rows, vals):
            o_ref[0, pl.ds(row, 8), :] = val
        return c

    lax.fori_loop(0, CAP // 4, tok, 0)


def _combine(idx_flat, ye):
    return pl.pallas_call(
        _combine_kernel,
        grid_spec=pltpu.PrefetchScalarGridSpec(
            num_scalar_prefetch=1,
            grid=(2, N_EXPERTS),
            in_specs=[pl.BlockSpec((1, CAP, D_MODEL), lambda g, e, idx: (e, g, 0))],
            out_specs=pl.BlockSpec((1, N_TOK * 8, 128), lambda g, e, idx: (g, 0, 0)),
            scratch_shapes=[pltpu.VMEM((8 * TOK_STRIDE, 128), F32)]),
        out_shape=jax.ShapeDtypeStruct((2, N_TOK * 8, 128), F32),
        compiler_params=_cparams(2),
        name="ec_combine",
    )(idx_flat, ye)


def _final_kernel(x1_ref, y_ref, g2_ref, fg_ref, o_ref):
    x = x1_ref[...] + g2_ref[0] * y_ref[...]
    ms = jnp.mean(x * x, axis=-1, keepdims=True)
    o_ref[...] = x * lax.rsqrt(ms + RMS_EPS) * fg_ref[...]


def _final(x1, y, mod3, final_g):
    n = x1.shape[0]
    row = pl.BlockSpec((ROW_TILE, D_MODEL), lambda i: (i, 0))
    return pl.pallas_call(
        _final_kernel,
        grid=(n // ROW_TILE,),
        in_specs=[row, row, pl.BlockSpec((1, 1, D_MODEL), lambda i: (_cond_index(i), 0, 5)),
                  pl.BlockSpec((1, D_MODEL), lambda i: (0, 0))],
        out_specs=row,
        out_shape=jax.ShapeDtypeStruct((n, D_MODEL), F32),
        compiler_params=_cparams(1),
        name="final_norm",
    )(x1, y, mod3, final_g)


def kernel(x_prompt, x_sample, state_wkv, c, c_ctx, norm1_g, norm2_g, w_mod, b_mod, w_in, conv_dw, conv_b, conv_ln_g,
           conv_ln_b, rwkv_w0, rwkv_w_up, rwkv_a0, rwkv_a_up, rwkv_g_up, rwkv_k_k, rwkv_k_a, rwkv_r_k, rwkv_gn_g,
           rwkv_gn_b, w_out, router, exp_w1, exp_w3, exp_w2, final_norm_g):
    row2 = lambda a: a.reshape(1, -1)
    x = jnp.concatenate([x_prompt.reshape(N_TOK, D_MODEL), x_sample.reshape(N_TOK, D_MODEL)], axis=0)
    cond8 = jnp.concatenate([c_ctx[None, :], c, jnp.zeros((8 - 1 - DEC_BATCH, D_MODEL), F32)], axis=0)

    mod = _modulation(cond8, w_mod[0], b_mod)
    mod3 = mod.reshape(8, 1, 6 * D_MODEL)

    u, rkv, lora = _inproj(x, mod3, norm1_g, w_in[0].astype(BF16))

    conv_args = (conv_dw[0], conv_b, conv_ln_g, conv_ln_b)
    conv = jnp.concatenate([_conv_branch(u[:N_TOK], *conv_args, seg=SEQ),
                            _conv_branch(u[N_TOK:], *conv_args, seg=GRID_W)], axis=0)

    lane_head = jnp.arange(RWKV_W, dtype=jnp.int32) // HEAD
    ones_blk = (lane_head[:, None] == lane_head[None, :]).astype(BF16)
    def per_dir_rows(up):
        z = jnp.zeros_like(up[0])
        return jnp.stack([jnp.concatenate([up[0], z], axis=0), jnp.concatenate([z, up[1]], axis=0)]).astype(BF16)
    (r, kap, v, w0, w1, b0, b1, kd0, kd1, g, bonus) = _rwkv_prep(
        rkv, lora, ones_blk, rwkv_w0[0], per_dir_rows(rwkv_w_up[0]), rwkv_a0[0], per_dir_rows(rwkv_a_up[0]),
        rwkv_g_up[0].astype(BF16), rwkv_k_k, rwkv_k_a, row2(rwkv_r_k[0]))

    n_grp = N_HEADS // HEADS_PER_GROUP

    def embed_states(s):
        s = s.reshape(-1, n_grp, HEADS_PER_GROUP, HEAD, HEAD)
        z = jnp.zeros(s.shape[:2] + (HEADS_PER_GROUP, HEAD, HEADS_PER_GROUP, HEAD), F32)
        for h in range(HEADS_PER_GROUP):
            z = z.at[:, :, h, :, h, :].set(s[:, :, h])
        return z.reshape(-1, n_grp, GROUP_W, GROUP_W)

    def extract_states(z):
        z = z.reshape(-1, n_grp, HEADS_PER_GROUP, HEAD, HEADS_PER_GROUP, HEAD)
        return jnp.stack([z[:, :, h, :, h, :] for h in range(HEADS_PER_GROUP)], axis=2).reshape(
            -1, N_HEADS, HEAD, HEAD)

    dir_ops = ((w0, b0, kd0), (w1, b1, kd1))
    yp_f, yp_b, sp_f, sp_b = _chunk_scan(r, kap, v, dir_ops, None, row0=0, n_batch=BATCH, t_len=SEQ, want_final=True)
    ys_f, ys_b = _chunk_scan(r, kap, v, dir_ops, (embed_states(state_wkv[:, 0, 0]), embed_states(state_wkv[:, 0, 1])),
                             row0=N_TOK, n_batch=DEC_BATCH, t_len=DEC_SEQ, want_final=False)
    y_f = jnp.concatenate([yp_f, ys_f], axis=0)
    y_b = jnp.concatenate([yp_b, ys_b], axis=0)
    new_state = jnp.stack([extract_states(sp_f), extract_states(sp_b)], axis=1)[:, None]

    r_pad = jnp.pad(router[0], ((0, 0), (0, 128 - N_EXPERTS)))
    r_hi = r_pad.astype(BF16)
    r_lo = (r_pad - r_hi.astype(F32)).astype(BF16)
    x1, h2, logits = _postmix(y_f, y_b, bonus, g, conv, x, mod3, rwkv_gn_g, rwkv_gn_b, ones_blk,
                              w_out[0].astype(BF16), norm2_g, r_hi, r_lo)

    logits_t = logits[:, :N_EXPERTS].reshape(2, N_TOK, N_EXPERTS).transpose(0, 2, 1).reshape(2 * N_EXPERTS, N_TOK)
    lane = jnp.arange(128, dtype=jnp.int32)
    utri = (lane[:, None] < lane[None, :]).astype(BF16)
    pos, aff = _select(logits_t, utri)
    idx_c, gate_c = _compact(pos.reshape(2 * N_EXPERTS, 1, N_TOK), aff.reshape(2 * N_EXPERTS, 1, N_TOK))
    idx_flat = idx_c[:, :, 0].astype(jnp.int32).reshape(-1)
    gate_col = gate_c[:, :, 0].reshape(2, N_EXPERTS, CAP).transpose(1, 0, 2).reshape(N_EXPERTS, 2 * CAP, 1)

    xe = _gather(idx_flat, h2.reshape(2, N_TOK * 8, 128))
    ye = _experts(xe, exp_w1[0], exp_w3[0], exp_w2[0], gate_col)
    y_moe = _combine(idx_flat, ye).reshape(2 * N_TOK, D_MODEL)

    out = _final(x1, y_moe, mod3, row2(final_norm_g))
    return (out[:N_TOK].reshape(BATCH, SEQ, D_MODEL), out[N_TOK:].reshape(DEC_BATCH, DEC_SEQ, D_MODEL), new_state)
```

```python
import functools

import jax
import jax.numpy as jnp
from jax import lax
from jax.experimental import pallas as pl
from jax.experimental.pallas import tpu as pltpu

F32 = jnp.float32
BF16 = jnp.bfloat16

D_MODEL = 1024
BATCH = 16
SEQ = 256
DEC_BATCH = 4
DEC_SEQ = 1024
GRID_W = 64
CONV_W = 512
CONV_K = 31
CONV_PAD = 15
RWKV_W = 512
HEAD = 64
N_HEADS = 8
LORA = 64
GATE_LORA = 128
N_EXPERTS = 16
D_FF = 2816
D_IN = 2944
CAP = 512
N_TOK = 4096
RMS_EPS = 1e-6
LN_EPS = 1e-5
GN_EPS = 64e-5

ROW_TILE = 512
CONV_TILE = 256
FF_SPLIT = 2
FF_SLAB = D_FF // FF_SPLIT
FF_SUB = 256
EXPERT_VMEM_LIMIT = 60 * 1024 * 1024
CHUNK = 64
HEADS_PER_GROUP = 4
GROUP_W = HEADS_PER_GROUP * HEAD
PASSES_A = 3
PASSES_INV = 1
PASSES_STATE = 1
SCAN_BATCHES = 2
TOK_ROWS = D_MODEL // 128
TOK_STRIDE = CAP + 8
BISECT_ITERS = 160
VMEM_LIMIT = 56 * 1024 * 1024


def _cparams(n_axes, vmem=None):
    return pltpu.CompilerParams(dimension_semantics=("arbitrary",) * n_axes,
                                vmem_limit_bytes=vmem or VMEM_LIMIT)


def _cond_index(i):
    n_prompt = (BATCH * SEQ) // ROW_TILE
    per_batch = DEC_SEQ // ROW_TILE
    return jnp.where(i < n_prompt, 0, 1 + (i - n_prompt) // per_batch)


def _split_dot(x, w_bf16):
    hi = x.astype(BF16)
    lo = (x - hi.astype(F32)).astype(BF16)
    return (jnp.dot(hi, w_bf16, preferred_element_type=F32)
            + jnp.dot(lo, w_bf16, preferred_element_type=F32))


def _sigmoid(x):
    return 1.0 / (1.0 + jnp.exp(-x))


def _mod_kernel(c_ref, w_ref, b_ref, o_ref):
    c = c_ref[...]
    s = (c * _sigmoid(c)).astype(BF16)
    o_ref[...] = jnp.dot(s, w_ref[...].astype(BF16), preferred_element_type=F32) + b_ref[...]


def _modulation(cond8, w_mod, b_mod):
    n = 6 * D_MODEL
    tn = D_MODEL
    return pl.pallas_call(
        _mod_kernel,
        grid=(n // tn,),
        in_specs=[pl.BlockSpec((8, D_MODEL), lambda j: (0, 0)),
                  pl.BlockSpec((D_MODEL, tn), lambda j: (0, j)),
                  pl.BlockSpec((1, tn), lambda j: (0, j))],
        out_specs=pl.BlockSpec((8, tn), lambda j: (0, j)),
        out_shape=jax.ShapeDtypeStruct((8, n), F32),
        compiler_params=_cparams(1),
        name="modulation",
    )(cond8, w_mod, b_mod)


def _group_tile(n_prompt_tiles):
    return (lambda i: (jnp.minimum(i, n_prompt_tiles - 1), 0)), (lambda i: (jnp.maximum(i - n_prompt_tiles, 0), 0))


def _pick(prompt_ref, sample_ref, n_prompt_tiles):
    return jnp.where(pl.program_id(0) < n_prompt_tiles, prompt_ref[...], sample_ref[...])


def _inproj_kernel(xp_ref, xs_ref, sh_ref, sc_ref, g_ref, w_ref, u_ref, rkv_ref, lora_ref):
    x = _pick(xp_ref, xs_ref, N_TOK // ROW_TILE)
    ms = jnp.mean(x * x, axis=-1, keepdims=True)
    h = x * lax.rsqrt(ms + RMS_EPS) * g_ref[...]
    h = (h * (1.0 + sc_ref[0]) + sh_ref[0]).astype(BF16)
    uv = jnp.dot(h, w_ref[:, 0:CONV_W], preferred_element_type=F32)
    ug = jnp.dot(h, w_ref[:, CONV_W:2 * CONV_W], preferred_element_type=F32)
    u_ref[...] = uv * _sigmoid(ug)
    rkv_ref[...] = jnp.dot(h, w_ref[:, 2 * CONV_W:2 * CONV_W + 3 * RWKV_W], preferred_element_type=F32)
    lora_ref[...] = jnp.dot(h, w_ref[:, 2 * CONV_W + 3 * RWKV_W:D_IN], preferred_element_type=F32)


def _inproj(x_prompt, x_sample, mod3, norm1_g, w_in_bf16):
    n = 2 * N_TOK
    n_lora = D_IN - 2 * CONV_W - 3 * RWKV_W
    first, second = _group_tile(N_TOK // ROW_TILE)
    return pl.pallas_call(
        _inproj_kernel,
        grid=(n // ROW_TILE,),
        in_specs=[pl.BlockSpec((ROW_TILE, D_MODEL), first),
                  pl.BlockSpec((ROW_TILE, D_MODEL), second),
                  pl.BlockSpec((1, 1, D_MODEL), lambda i: (_cond_index(i), 0, 0)),
                  pl.BlockSpec((1, 1, D_MODEL), lambda i: (_cond_index(i), 0, 1)),
                  pl.BlockSpec((1, D_MODEL), lambda i: (0, 0)),
                  pl.BlockSpec((D_MODEL, D_IN), lambda i: (0, 0))],
        out_specs=[pl.BlockSpec((ROW_TILE, CONV_W), lambda i: (i, 0)),
                   pl.BlockSpec((ROW_TILE, 3 * RWKV_W), lambda i: (i, 0)),
                   pl.BlockSpec((ROW_TILE, n_lora), lambda i: (i, 0))],
        out_shape=[jax.ShapeDtypeStruct((n, CONV_W), F32),
                   jax.ShapeDtypeStruct((n, 3 * RWKV_W), F32),
                   jax.ShapeDtypeStruct((n, n_lora), F32)],
        compiler_params=_cparams(1),
        name="inproj",
    )(x_prompt, x_sample, mod3, mod3, norm1_g, w_in_bf16)


def _conv_kernel(u_ref, dw_ref, db_ref, g_ref, b_ref, o_ref, *, seg):
    u = u_ref[...]
    pos = lax.broadcasted_iota(jnp.int32, (CONV_TILE, 1), 0) % seg
    acc = jnp.zeros_like(u)
    for k in range(CONV_K):
        s = k - CONV_PAD
        shifted = u if s == 0 else pltpu.roll(u, (-s) % CONV_TILE, 0)
        valid = (pos + s >= 0) & (pos + s < seg)
        acc = acc + jnp.where(valid, shifted, 0.0) * dw_ref[k:k + 1, :]
    out = acc + db_ref[...]
    mu = jnp.mean(out, axis=-1, keepdims=True)
    d = out - mu
    var = jnp.mean(d * d, axis=-1, keepdims=True)
    out = d * lax.rsqrt(var + LN_EPS) * g_ref[...] + b_ref[...]
    o_ref[...] = out * _sigmoid(out)


def _conv_branch(u, dw, db, ln_g, ln_b, seg, row0):
    n = N_TOK
    tile0 = row0 // CONV_TILE
    return pl.pallas_call(
        functools.partial(_conv_kernel, seg=seg),
        grid=(n // CONV_TILE,),
        in_specs=[pl.BlockSpec((CONV_TILE, CONV_W), lambda i: (tile0 + i, 0)),
                  pl.BlockSpec((CONV_K, CONV_W), lambda i: (0, 0)),
                  pl.BlockSpec((1, CONV_W), lambda i: (0, 0)),
                  pl.BlockSpec((1, CONV_W), lambda i: (0, 0)),
                  pl.BlockSpec((1, CONV_W), lambda i: (0, 0))],
        out_specs=pl.BlockSpec((CONV_TILE, CONV_W), lambda i: (i, 0)),
        out_shape=jax.ShapeDtypeStruct((n, CONV_W), F32),
        compiler_params=_cparams(1),
        name=f"conv_seg{seg}",
    )(u, dw, db, ln_g, ln_b)


def _prep_kernel(rkv_ref, lora_ref, ones_ref, w0_ref, wup_ref, a0_ref, aup_ref, gup_ref, kk_ref, ka_ref, rk_ref,
                 r_o, kap_o, v_o, w0_o, w1_o, b0_o, b1_o, kd0_o, kd1_o, g_o, bonus_o):
    r = rkv_ref[:, 0:RWKV_W]
    k = rkv_ref[:, RWKV_W:2 * RWKV_W]
    v = rkv_ref[:, 2 * RWKV_W:3 * RWKV_W]
    xw = jnp.tanh(lora_ref[:, 0:2 * LORA]).astype(BF16)
    xa = lora_ref[:, 2 * LORA:4 * LORA].astype(BF16)
    xg = _sigmoid(lora_ref[:, 4 * LORA:4 * LORA + GATE_LORA]).astype(BF16)
    ones = ones_ref[...]

    kk0 = k * kk_ref[...]
    ss = _split_dot(kk0 * kk0, ones)
    kap = kk0 * lax.rsqrt(jnp.maximum(ss, 1e-24))
    r_o[...] = r
    kap_o[...] = kap
    v_o[...] = v
    g_o[...] = jnp.dot(xg, gup_ref[...], preferred_element_type=F32)

    kd_sum = jnp.zeros_like(k)
    for d, (w_o, b_o, kd_o) in enumerate(((w0_o, b0_o, kd0_o), (w1_o, b1_o, kd1_o))):
        wl = w0_ref[d:d + 1, :] + jnp.dot(xw, wup_ref[d], preferred_element_type=F32)
        w_o[...] = -jnp.exp(-0.5) * _sigmoid(wl)
        a = _sigmoid(a0_ref[d:d + 1, :] + jnp.dot(xa, aup_ref[d], preferred_element_type=F32))
        b_o[...] = kap * a
        kd = k * (1.0 + (a - 1.0) * ka_ref[...])
        kd_o[...] = kd
        kd_sum = kd_sum + kd
    bonus_o[...] = _split_dot(r * kd_sum * rk_ref[...], ones) * v


def _rwkv_prep(rkv, lora, ones_blk, w0, w_up, a0, a_up, g_up, k_k, k_a, r_k):
    n = rkv.shape[0]
    row = lambda c: pl.BlockSpec((ROW_TILE, c), lambda i: (i, 0))
    full = lambda a: pl.BlockSpec(a.shape, lambda i: (0,) * a.ndim)
    args = (rkv, lora, ones_blk, w0, w_up, a0, a_up, g_up, k_k, k_a, r_k)
    return pl.pallas_call(
        _prep_kernel,
        grid=(n // ROW_TILE,),
        in_specs=[row(3 * RWKV_W), row(lora.shape[1])] + [full(a) for a in args[2:]],
        out_specs=[row(RWKV_W)] * 11,
        out_shape=[jax.ShapeDtypeStruct((n, RWKV_W), F32)] * 11,
        compiler_params=_cparams(1),
        name="rwkv_prep",
    )(*args)


def _pieces(x, passes):
    hi = x.astype(BF16)
    return (hi,) if passes == 1 else (hi, (x - hi.astype(F32)).astype(BF16))


_NN = ((1,), (0,))
_NT = ((1,), (1,))
_TN = ((0,), (0,))


def _mm(a, b, dims):
    dg = lambda p, q: lax.dot_general(p, q, (dims, ((), ())), preferred_element_type=F32)
    out = dg(a[0], b[0])
    if len(a) > 1:
        out = out + dg(a[1], b[0])
    if len(b) > 1:
        out = out + dg(a[0], b[1])
    return out


def _block_diag(pieces, head_masks):
    return tuple(jnp.concatenate([p * m for m in head_masks], axis=0) for p in pieces)


def _chunk_masks(reverse):
    row = lax.broadcasted_iota(jnp.int32, (CHUNK, GROUP_W), 0)
    col = lax.broadcasted_iota(jnp.int32, (CHUNK, GROUP_W), 1) % CHUNK
    strict = (col > row) if reverse else (col < row)
    incl = (col >= row) if reverse else (col <= row)
    eye_side = jnp.where(col == row, 1.0, 0.0)
    tri = jnp.where(incl[:, :CHUNK], 1.0, 0.0).astype(BF16)
    return strict, incl, eye_side, tri, (0 if reverse else CHUNK - 1)


def _chunk_steps(chains):
    ln = CHUNK
    each = lambda f, *cols: [f(*args) for args in zip(*cols)]
    r, kap, v, lw, b, kd, n, masks = (list(col) for col in zip(*chains))
    strict = [m[0] for m in masks]
    incl = [m[1] for m in masks]
    lane_blk = lax.broadcasted_iota(jnp.int32, (ln, GROUP_W), 1) // HEAD
    head_masks = [jnp.where(lane_blk == h, 1.0, 0.0).astype(BF16) for h in range(HEADS_PER_GROUP)]
    bd = lambda x, passes: _block_diag(_pieces(x, passes), head_masks)
    r2 = lax.broadcasted_iota(jnp.int32, (GROUP_W, GROUP_W), 0) // HEAD
    c2 = lax.broadcasted_iota(jnp.int32, (GROUP_W, GROUP_W), 1) // HEAD
    same_head = r2 == c2

    def running_sum(lw_c, m):
        return _mm((m[3],), _pieces(lw_c, 3), _NN)

    cw = each(running_sum, lw, masks)
    tot = each(lambda cw_c, m: cw_c[m[4]:m[4] + 1, :], cw, masks)
    e_neg = each(lambda cw_c: jnp.exp(-cw_c), cw)
    kr = each(lambda kap_c, r_c, cw_c, lw_c: _pieces(
        jnp.concatenate([kap_c * jnp.exp(cw_c - lw_c), r_c * jnp.exp(cw_c)], axis=0), PASSES_A),
        kap, r, cw, lw)
    ab = each(lambda kr_c, b_c, e: _mm(kr_c, bd(b_c * e, PASSES_A), _NT), kr, b, e_neg)
    ak = each(lambda kr_c, kd_c, e: _mm(kr_c, bd(kd_c * e, PASSES_A), _NT), kr, kd, e_neg)

    q = each(lambda ab_c, s: jnp.where(s, -ab_c[:ln], 0.0), ab, strict)
    x = each(lambda q_c, m: m[2] + q_c, q, masks)
    for _ in range(CHUNK.bit_length() - 2):
        qp = each(lambda q_c: _pieces(q_c, PASSES_INV), q)
        q = each(lambda qp_c: _mm(qp_c, _block_diag(qp_c, head_masks), _NN), qp)
        x = each(lambda x_c, q_c: x_c + _mm(_pieces(x_c, PASSES_INV), bd(q_c, PASSES_INV), _NN), x, q)

    a3 = each(lambda ab_c, i: jnp.where(i, ab_c[ln:], 0.0), ab, incl)
    a24 = each(lambda ak_c, s, i: jnp.concatenate([jnp.where(s, ak_c[:ln], 0.0), jnp.where(i, ak_c[ln:], 0.0)], axis=0),
               ak, strict, incl)
    av = each(lambda a_c, v_c: _mm(_pieces(a_c, PASSES_A), bd(v_c, PASSES_A), _NN), a24, v)
    krn = each(lambda kr_c, n_c: _mm(kr_c[:PASSES_STATE], _pieces(n_c, PASSES_STATE), _NT), kr, n)
    u = each(lambda x_c, krn_c, av_c: -_mm(_pieces(x_c, PASSES_INV), bd(krn_c[:ln] + av_c[:ln], PASSES_INV), _NN),
             x, krn, av)
    y = each(lambda krn_c, a3_c, u_c, av_c: krn_c[ln:] + _mm(_pieces(a3_c, PASSES_STATE), bd(u_c, PASSES_STATE), _NN)
             + av_c[ln:], krn, a3, u, av)
    e_rest = each(lambda tot_c, cw_c: jnp.exp(tot_c - cw_c), tot, cw)
    upd = each(lambda u_c, v_c, b_c, kd_c, e: _mm(_pieces(jnp.concatenate([u_c, v_c], axis=0), PASSES_A),
                                                   _pieces(jnp.concatenate([b_c * e, kd_c * e], axis=0), PASSES_A), _TN),
               u, v, b, kd, e_rest)
    n_new = each(lambda n_c, tot_c, upd_c: n_c * jnp.exp(tot_c) + jnp.where(same_head, upd_c, 0.0), n, tot, upd)
    return list(zip(y, n_new))


def _chunk_kernel(*refs, has_init, has_final):
    n_tok = 12 * SCAN_BATCHES
    tok_refs = [refs[k * 12:(k + 1) * 12] for k in range(SCAN_BATCHES)]
    refs = refs[n_tok:]
    s0 = refs[0] if has_init else None
    refs = refs[1:] if has_init else refs
    y_refs = refs[:2]
    sfin = refs[2] if has_final else None
    n_scr = refs[-1]
    c = pl.program_id(1)
    head_blocks = [(k, d, g, h) for k in range(SCAN_BATCHES) for d in range(2)
                   for g in range(N_HEADS // HEADS_PER_GROUP) for h in range(HEADS_PER_GROUP)]
    diag = lambda h: slice(h * HEAD, (h + 1) * HEAD)

    @pl.when(c == 0)
    def _():
        n_scr[...] = jnp.zeros(n_scr.shape, F32)
        if has_init:
            for k, d, g, h in head_blocks:
                n_scr[d, k, g, diag(h), diag(h)] = s0[k, d, g * HEADS_PER_GROUP + h]

    slots = [(k, d, g) for k in range(SCAN_BATCHES) for d in range(2) for g in range(N_HEADS // HEADS_PER_GROUP)]
    lanes = lambda g: slice(g * GROUP_W, (g + 1) * GROUP_W)
    masks = [_chunk_masks(reverse=False), _chunk_masks(reverse=True)]

    def operands(k, d, g):
        rf, kf, vf, rb, kb, vb, lwf, bf, kdf, lwb, bb, kdb = tok_refs[k]
        group = (rb, kb, vb, lwb, bb, kdb) if d else (rf, kf, vf, lwf, bf, kdf)
        return tuple(ref[:, lanes(g)] for ref in group)

    chains = [operands(k, d, g) + (n_scr[d, k, g], masks[d]) for k, d, g in slots]
    for (k, d, g), (y, n_new) in zip(slots, _chunk_steps(chains)):
        y_refs[d][k, :, lanes(g)] = y
        n_scr[d, k, g] = n_new

    if has_final:
        @pl.when(c == pl.num_programs(1) - 1)
        def _():
            for k, d, g, h in head_blocks:
                sfin[k, d, g * HEADS_PER_GROUP + h] = n_scr[d, k, g, diag(h), diag(h)]


def _chunk_scan(r, kap, v, dir_ops, s0, *, row0, n_batch, t_len, want_final):
    n_chunks = t_len // CHUNK
    blk0 = row0 // CHUNK
    n_grp = N_HEADS // HEADS_PER_GROUP
    nb = SCAN_BATCHES

    def tok_spec(k, reverse):
        def index(bi, ci):
            return (blk0 + (bi * nb + k) * n_chunks + (n_chunks - 1 - ci if reverse else ci), 0)
        return pl.BlockSpec((CHUNK, RWKV_W), index)

    in_specs, args = [], []
    for k in range(nb):
        in_specs += [tok_spec(k, False)] * 3 + [tok_spec(k, True)] * 3 + [tok_spec(k, False)] * 3 + [tok_spec(k, True)] * 3
        args += [r, kap, v, r, kap, v, *dir_ops[0], *dir_ops[1]]
    state = pl.BlockSpec((nb, 2, N_HEADS, HEAD, HEAD), lambda bi, ci: (bi, 0, 0, 0, 0))
    has_init = s0 is not None
    if has_init:
        in_specs.append(state)
        args.append(s0)
    y_shape = jax.ShapeDtypeStruct((n_batch, t_len, RWKV_W), F32)
    out_specs = [pl.BlockSpec((nb, CHUNK, RWKV_W), lambda bi, ci: (bi, ci, 0)),
                 pl.BlockSpec((nb, CHUNK, RWKV_W), lambda bi, ci: (bi, n_chunks - 1 - ci, 0))]
    out_shape = [y_shape, y_shape]
    if want_final:
        out_specs.append(state)
        out_shape.append(jax.ShapeDtypeStruct((n_batch, 2, N_HEADS, HEAD, HEAD), F32))
    outs = pl.pallas_call(
        functools.partial(_chunk_kernel, has_init=has_init, has_final=want_final),
        grid=(n_batch // nb, n_chunks),
        in_specs=in_specs,
        out_specs=out_specs,
        out_shape=out_shape,
        scratch_shapes=[pltpu.VMEM((2, nb, n_grp, GROUP_W, GROUP_W), F32)],
        compiler_params=_cparams(2),
        name=f"wkv_chunk_t{t_len}",
    )(*args)
    return [outs[0].reshape(n_batch * t_len, RWKV_W), outs[1].reshape(n_batch * t_len, RWKV_W)] + list(outs[2:])


def _postmix_kernel(yfp_ref, yfs_ref, ybp_ref, ybs_ref, convp_ref, convs_ref, xp_ref, xs_ref, bonus_ref, g_ref,
                    g1_ref, sh2_ref, sc2_ref, gng_ref, gnb_ref, ones_ref, wout_ref, n2g_ref, rhi_ref, rlo_ref,
                    x1_o, h2_o, logit_o):
    n_prompt = N_TOK // ROW_TILE
    ones = ones_ref[...]
    y = _pick(yfp_ref, yfs_ref, n_prompt) + _pick(ybp_ref, ybs_ref, n_prompt)
    mu = _split_dot(y, ones) * (1.0 / HEAD)
    d = y - mu
    var = _split_dot(d * d, ones) * (1.0 / HEAD)
    yn = d * lax.rsqrt(var + GN_EPS) * gng_ref[...] + gnb_ref[...]
    rw = ((yn + bonus_ref[...]) * g_ref[...]).astype(BF16)
    conv = _pick(convp_ref, convs_ref, n_prompt).astype(BF16)
    mix = (jnp.dot(conv, wout_ref[0:CONV_W, :], preferred_element_type=F32)
           + jnp.dot(rw, wout_ref[CONV_W:CONV_W + RWKV_W, :], preferred_element_type=F32))
    x1 = _pick(xp_ref, xs_ref, n_prompt) + g1_ref[0] * mix
    x1_o[...] = x1
    ms = jnp.mean(x1 * x1, axis=-1, keepdims=True)
    h2 = x1 * lax.rsqrt(ms + RMS_EPS) * n2g_ref[...]
    h2 = h2 * (1.0 + sc2_ref[0]) + sh2_ref[0]
    for c in range(TOK_ROWS):
        h2_o[pl.ds(c, ROW_TILE, stride=TOK_ROWS), :] = h2[:, c * 128:(c + 1) * 128]
    hi = h2.astype(BF16)
    lo = (h2 - hi.astype(F32)).astype(BF16)
    logit_o[...] = (jnp.dot(hi, rhi_ref[...], preferred_element_type=F32)
                    + jnp.dot(lo, rhi_ref[...], preferred_element_type=F32)
                    + jnp.dot(hi, rlo_ref[...], preferred_element_type=F32))


def _postmix(yf, yb, conv, x, bonus, g, mod3, gn_g, gn_b, ones_blk, w_out_bf16, norm2_g, r_hi, r_lo):
    n = 2 * N_TOK
    row = lambda c: pl.BlockSpec((ROW_TILE, c), lambda i: (i, 0))
    pair = lambda c: [pl.BlockSpec((ROW_TILE, c), m) for m in _group_tile(N_TOK // ROW_TILE)]
    full = lambda a: pl.BlockSpec(a.shape, lambda i: (0,) * a.ndim)
    modspec = lambda j: pl.BlockSpec((1, 1, D_MODEL), lambda i: (_cond_index(i), 0, j))
    return pl.pallas_call(
        _postmix_kernel,
        grid=(n // ROW_TILE,),
        in_specs=pair(RWKV_W) + pair(RWKV_W) + pair(CONV_W) + pair(D_MODEL) + [
            row(RWKV_W), row(RWKV_W), modspec(2), modspec(3), modspec(4), full(gn_g), full(gn_b), full(ones_blk),
            full(w_out_bf16), full(norm2_g), full(r_hi), full(r_lo)],
        out_specs=[row(D_MODEL), pl.BlockSpec((ROW_TILE * TOK_ROWS, 128), lambda i: (i, 0)), row(128)],
        out_shape=[jax.ShapeDtypeStruct((n, D_MODEL), F32), jax.ShapeDtypeStruct((n * TOK_ROWS, 128), F32),
                   jax.ShapeDtypeStruct((n, 128), F32)],
        compiler_params=_cparams(1),
        name="postmix",
    )(*yf, *yb, *conv, *x, bonus, g, mod3, mod3, mod3, gn_g, gn_b, ones_blk, w_out_bf16, norm2_g, r_hi, r_lo)


def _prefix_lanes(m, utri):
    rows = m.shape[0]
    off = jnp.zeros((rows, 1), F32)
    out = []
    for c in range(N_TOK // 128):
        blk = m[:, c * 128:(c + 1) * 128]
        out.append(jnp.dot(blk.astype(BF16), utri, preferred_element_type=F32) + off)
        off = off + jnp.sum(blk, axis=1, keepdims=True)
    return jnp.concatenate(out, axis=1)


def _select_kernel(lt_ref, utri_ref, pos_o, aff_o):
    utri = utri_ref[...]
    affs = []
    for grp in range(2):
        x = lt_ref[grp * N_EXPERTS:(grp + 1) * N_EXPERTS, :]
        e = jnp.exp(x - jnp.max(x, axis=0, keepdims=True))
        affs.append(e / jnp.sum(e, axis=0, keepdims=True))
    aff = jnp.concatenate(affs, axis=0)
    aff_o[...] = aff
    n_rows = 2 * N_EXPERTS

    def body(_, lh):
        lo, hi = lh
        m2 = 0.5 * (lo + hi)
        m1 = 0.5 * (lo + m2)
        m3 = 0.5 * (m2 + hi)
        ge1, ge2, ge3 = (jnp.sum(jnp.where(aff > m, 1.0, 0.0), axis=1, keepdims=True) >= CAP for m in (m1, m2, m3))
        lo = jnp.where(ge3, m3, jnp.where(ge2, m2, jnp.where(ge1, m1, lo)))
        hi = jnp.where(ge3, hi, jnp.where(ge2, m3, jnp.where(ge1, m2, m1)))
        return lo, hi

    lo, hi = lax.fori_loop(0, BISECT_ITERS // 2, body,
                           (jnp.full((n_rows, 1), -1.0, F32), jnp.full((n_rows, 1), 1.0, F32)))
    gt = jnp.where(aff > hi, 1.0, 0.0)
    tie = jnp.where(aff > lo, 1.0, 0.0) - gt
    need = CAP - jnp.sum(gt, axis=1, keepdims=True)
    sel = gt + tie * jnp.where(_prefix_lanes(tie, utri) < need, 1.0, 0.0)
    pos = _prefix_lanes(sel, utri)
    pos_o[...] = jnp.where(sel > 0.5, pos, -1.0)


def _select(logits_t, utri):
    shp = jax.ShapeDtypeStruct((2 * N_EXPERTS, N_TOK), F32)
    full = lambda a: pl.BlockSpec(a.shape, lambda i: (0,) * a.ndim)
    return pl.pallas_call(
        _select_kernel,
        grid=(1,),
        in_specs=[full(logits_t), full(utri)],
        out_specs=[pl.BlockSpec(shp.shape, lambda i: (0, 0))] * 2,
        out_shape=[shp, shp],
        compiler_params=_cparams(1),
        name="ec_select",
    )(logits_t, utri)


def _compact_kernel(pos_ref, aff_ref, idx_o, gate_o):
    slot = lax.broadcasted_iota(jnp.int32, (CAP, 1), 0).astype(F32)
    idx = jnp.zeros((CAP, 128), F32)
    gate = jnp.zeros((CAP, 128), F32)
    for c in range(N_TOK // 128):
        p = pos_ref[0, :, c * 128:(c + 1) * 128]
        a = aff_ref[0, :, c * 128:(c + 1) * 128]
        tok = (lax.broadcasted_iota(jnp.int32, (1, 128), 1) + c * 128).astype(F32)
        hit = p == slot
        idx = idx + jnp.where(hit, tok, 0.0)
        gate = gate + jnp.where(hit, a, 0.0)
    idx_o[0] = jnp.broadcast_to(jnp.sum(idx, axis=1, keepdims=True), (CAP, 128))
    gate_o[0] = jnp.broadcast_to(jnp.sum(gate, axis=1, keepdims=True), (CAP, 128))


def _compact(pos3, aff3):
    rows = pos3.shape[0]
    shp = jax.ShapeDtypeStruct((rows, CAP, 128), F32)
    return pl.pallas_call(
        _compact_kernel,
        grid=(rows,),
        in_specs=[pl.BlockSpec((1, 1, N_TOK), lambda i: (i, 0, 0))] * 2,
        out_specs=[pl.BlockSpec((1, CAP, 128), lambda i: (i, 0, 0))] * 2,
        out_shape=[shp, shp],
        compiler_params=_cparams(1),
        name="ec_compact",
    )(pos3, aff3)


def _gather_kernel(idx_ref, h_ref, o_ref, tile):
    grp = pl.program_id(0)
    e = pl.program_id(1)
    base = (grp * N_EXPERTS + e) * CAP

    def tok(j8, c):
        for u in range(8):
            j = j8 * 8 + u
            row = pl.multiple_of(idx_ref[base + j] * 8, 8)
            tile[pl.ds(j, 8, stride=TOK_STRIDE), :] = h_ref[0, pl.ds(row, 8), :]
        return c

    lax.fori_loop(0, CAP // 8, tok, 0)
    for c in range(8):
        o_ref[0, :, c * 128:(c + 1) * 128] = tile[c * TOK_STRIDE:c * TOK_STRIDE + CAP, :].astype(BF16)


def _gather(idx_flat, h_rows):
    return pl.pallas_call(
        _gather_kernel,
        grid_spec=pltpu.PrefetchScalarGridSpec(
            num_scalar_prefetch=1,
            grid=(2, N_EXPERTS),
            in_specs=[pl.BlockSpec((1, N_TOK * 8, 128), lambda g, e, idx: (g, 0, 0))],
            out_specs=pl.BlockSpec((1, CAP, D_MODEL), lambda g, e, idx: (e, g, 0)),
            scratch_shapes=[pltpu.VMEM((8 * TOK_STRIDE, 128), F32)]),
        out_shape=jax.ShapeDtypeStruct((N_EXPERTS, 2 * CAP, D_MODEL), BF16),
        compiler_params=_cparams(2),
        name="ec_gather",
    )(idx_flat, h_rows)


def _expert_kernel(xe_ref, w1_ref, w3_ref, w2_ref, gate_ref, o_ref, hid_scr):
    s = pl.program_id(1)
    subs = [(lo, min(FF_SUB, FF_SLAB - lo)) for lo in range(0, FF_SLAB, FF_SUB)]

    @pl.when(s < FF_SPLIT)
    def _():
        x = xe_ref[0]
        for lo, width in subs:
            h1 = jnp.dot(x, w1_ref[0, :, lo:lo + width].astype(BF16), preferred_element_type=F32)
            h3 = jnp.dot(x, w3_ref[0, :, lo:lo + width].astype(BF16), preferred_element_type=F32)
            hid_scr[s, :, lo:lo + width] = (h1 * _sigmoid(h1) * h3).astype(BF16)

    @pl.when(s >= FF_SPLIT)
    def _():
        acc = None
        for lo, width in subs:
            part = jnp.dot(hid_scr[s - FF_SPLIT, :, lo:lo + width], w2_ref[0, lo:lo + width, :].astype(BF16),
                           preferred_element_type=F32)
            acc = part if acc is None else acc + part

        @pl.when(s == FF_SPLIT)
        def _():
            o_ref[0] = acc

        @pl.when(s > FF_SPLIT)
        def _():
            o_ref[0] = o_ref[0] + acc

    @pl.when(s == 2 * FF_SPLIT - 1)
    def _():
        o_ref[0] = o_ref[0] * gate_ref[0]


def _experts(xe, w1, w3, w2, gate_col):
    m = xe.shape[1]
    up = lambda e, s: (e, 0, jnp.minimum(s, FF_SPLIT - 1))
    down = lambda e, s: (e, jnp.maximum(s - FF_SPLIT, 0), 0)
    return pl.pallas_call(
        _expert_kernel,
        grid=(N_EXPERTS, 2 * FF_SPLIT),
        in_specs=[pl.BlockSpec((1, m, D_MODEL), lambda e, s: (e, 0, 0)),
                  pl.BlockSpec((1, D_MODEL, FF_SLAB), up),
                  pl.BlockSpec((1, D_MODEL, FF_SLAB), up),
                  pl.BlockSpec((1, FF_SLAB, D_MODEL), down),
                  pl.BlockSpec((1, m, 1), lambda e, s: (e, 0, 0))],
        out_specs=pl.BlockSpec((1, m, D_MODEL), lambda e, s: (e, 0, 0)),
        out_shape=jax.ShapeDtypeStruct((N_EXPERTS, m, D_MODEL), F32),
        scratch_shapes=[pltpu.VMEM((FF_SPLIT, m, FF_SLAB), BF16)],
        compiler_params=_cparams(2, vmem=EXPERT_VMEM_LIMIT),
        name="ec_experts",
    )(xe, w1, w3, w2, gate_col)


def _combine_kernel(idx_ref, ye_ref, o_ref, tile):
    grp = pl.program_id(0)
    e = pl.program_id(1)
    base = (grp * N_EXPERTS + e) * CAP

    @pl.when(e == 0)
    def _():
        o_ref[...] = jnp.zeros_like(o_ref)

    for c in range(8):
        tile[c * TOK_STRIDE:c * TOK_STRIDE + CAP, :] = ye_ref[0, :, c * 128:(c + 1) * 128]

    def tok(j4, c):
        rows, vals = [], []
        for u in range(4):
            j = j4 * 4 + u
            row = pl.multiple_of(idx_ref[base + j] * 8, 8)
            rows.append(row)
            vals.append(o_ref[0, pl.ds(row, 8), :] + tile[pl.ds(j, 8, stride=TOK_STRIDE), :])
        for row, val in zip(rows, vals):
            o_ref[0, pl.ds(row, 8), :] = val
        return c

    lax.fori_loop(0, CAP // 4, tok, 0)


def _combine(idx_flat, ye):
    return pl.pallas_call(
        _combine_kernel,
        grid_spec=pltpu.PrefetchScalarGridSpec(
            num_scalar_prefetch=1,
            grid=(2, N_EXPERTS),
            in_specs=[pl.BlockSpec((1, CAP, D_MODEL), lambda g, e, idx: (e, g, 0))],
            out_specs=pl.BlockSpec((1, N_TOK * 8, 128), lambda g, e, idx: (g, 0, 0)),
            scratch_shapes=[pltpu.VMEM((8 * TOK_STRIDE, 128), F32)]),
        out_shape=jax.ShapeDtypeStruct((2, N_TOK * 8, 128), F32),
        compiler_params=_cparams(2),
        name="ec_combine",
    )(idx_flat, ye)


def _final_kernel(x1_ref, y_ref, g2_ref, fg_ref, op_ref, os_ref):
    i = pl.program_id(0)
    n_prompt = N_TOK // ROW_TILE
    y = jnp.concatenate([y_ref[pl.ds(c, ROW_TILE, stride=TOK_ROWS), :] for c in range(TOK_ROWS)], axis=1)
    x = x1_ref[...] + g2_ref[0] * y
    ms = jnp.mean(x * x, axis=-1, keepdims=True)
    out = x * lax.rsqrt(ms + RMS_EPS) * fg_ref[...]

    @pl.when(i < n_prompt)
    def _():
        op_ref[...] = out

    @pl.when(i >= n_prompt)
    def _():
        os_ref[...] = out


def _final(x1, y_rows, mod3, final_g):
    n = x1.shape[0]
    row = pl.BlockSpec((ROW_TILE, D_MODEL), lambda i: (i, 0))
    first, second = _group_tile(N_TOK // ROW_TILE)
    shp = jax.ShapeDtypeStruct((N_TOK, D_MODEL), F32)
    return pl.pallas_call(
        _final_kernel,
        grid=(n // ROW_TILE,),
        in_specs=[row, pl.BlockSpec((ROW_TILE * TOK_ROWS, 128), lambda i: (i, 0)),
                  pl.BlockSpec((1, 1, D_MODEL), lambda i: (_cond_index(i), 0, 5)),
                  pl.BlockSpec((1, D_MODEL), lambda i: (0, 0))],
        out_specs=[pl.BlockSpec((ROW_TILE, D_MODEL), first), pl.BlockSpec((ROW_TILE, D_MODEL), second)],
        out_shape=[shp, shp],
        compiler_params=_cparams(1),
        name="final_norm",
    )(x1, y_rows, mod3, final_g)


def kernel(x_prompt, x_sample, state_wkv, c, c_ctx, norm1_g, norm2_g, w_mod, b_mod, w_in, conv_dw, conv_b, conv_ln_g,
           conv_ln_b, rwkv_w0, rwkv_w_up, rwkv_a0, rwkv_a_up, rwkv_g_up, rwkv_k_k, rwkv_k_a, rwkv_r_k, rwkv_gn_g,
           rwkv_gn_b, w_out, router, exp_w1, exp_w3, exp_w2, final_norm_g):
    row2 = lambda a: a.reshape(1, -1)
    x = (x_prompt.reshape(N_TOK, D_MODEL), x_sample.reshape(N_TOK, D_MODEL))
    cond8 = jnp.concatenate([c_ctx[None, :], c, jnp.zeros((8 - 1 - DEC_BATCH, D_MODEL), F32)], axis=0)

    mod = _modulation(cond8, w_mod[0], b_mod)
    mod3 = mod.reshape(8, 1, 6 * D_MODEL)

    u, rkv, lora = _inproj(*x, mod3, norm1_g, w_in[0].astype(BF16))

    conv_args = (conv_dw[0], conv_b, conv_ln_g, conv_ln_b)
    conv = (_conv_branch(u, *conv_args, seg=SEQ, row0=0), _conv_branch(u, *conv_args, seg=GRID_W, row0=N_TOK))

    lane_head = jnp.arange(RWKV_W, dtype=jnp.int32) // HEAD
    ones_blk = (lane_head[:, None] == lane_head[None, :]).astype(BF16)

    def per_dir_rows(up):
        z = jnp.zeros_like(up[0])
        return jnp.stack([jnp.concatenate([up[0], z], axis=0), jnp.concatenate([z, up[1]], axis=0)]).astype(BF16)
    (r, kap, v, w0, w1, b0, b1, kd0, kd1, g, bonus) = _rwkv_prep(
        rkv, lora, ones_blk, rwkv_w0[0], per_dir_rows(rwkv_w_up[0]), rwkv_a0[0], per_dir_rows(rwkv_a_up[0]),
        rwkv_g_up[0].astype(BF16), rwkv_k_k, rwkv_k_a, row2(rwkv_r_k[0]))

    dir_ops = ((w0, b0, kd0), (w1, b1, kd1))
    yp_f, yp_b, new_state = _chunk_scan(r, kap, v, dir_ops, None, row0=0, n_batch=BATCH, t_len=SEQ, want_final=True)
    ys_f, ys_b = _chunk_scan(r, kap, v, dir_ops, state_wkv[:, 0], row0=N_TOK, n_batch=DEC_BATCH, t_len=DEC_SEQ,
                             want_final=False)

    r_pad = jnp.pad(router[0], ((0, 0), (0, 128 - N_EXPERTS)))
    r_hi = r_pad.astype(BF16)
    r_lo = (r_pad - r_hi.astype(F32)).astype(BF16)
    x1, h2_rows, logits = _postmix((yp_f, ys_f), (yp_b, ys_b), conv, x, bonus, g, mod3, rwkv_gn_g, rwkv_gn_b, ones_blk,
                                   w_out[0].astype(BF16), norm2_g, r_hi, r_lo)

    logits_t = logits[:, :N_EXPERTS].reshape(2, N_TOK, N_EXPERTS).transpose(0, 2, 1).reshape(2 * N_EXPERTS, N_TOK)
    lane = jnp.arange(128, dtype=jnp.int32)
    utri = (lane[:, None] < lane[None, :]).astype(BF16)
    pos, aff = _select(logits_t, utri)
    idx_c, gate_c = _compact(pos.reshape(2 * N_EXPERTS, 1, N_TOK), aff.reshape(2 * N_EXPERTS, 1, N_TOK))
    idx_flat = idx_c[:, :, 0].astype(jnp.int32).reshape(-1)
    gate_col = gate_c[:, :, 0].reshape(2, N_EXPERTS, CAP).transpose(1, 0, 2).reshape(N_EXPERTS, 2 * CAP, 1)

    xe = _gather(idx_flat, h2_rows.reshape(2, N_TOK * TOK_ROWS, 128))
    ye = _experts(xe, exp_w1[0], exp_w3[0], exp_w2[0], gate_col)
    y_rows = _combine(idx_flat, ye).reshape(2 * N_TOK * TOK_ROWS, 128)

    out_p, out_s = _final(x1, y_rows, mod3, row2(final_norm_g))
    return (out_p.reshape(BATCH, SEQ, D_MODEL), out_s.reshape(DEC_BATCH, DEC_SEQ, D_MODEL),
            new_state.reshape(BATCH, 1, 2, N_HEADS, HEAD, HEAD))
```

```python
import functools

import jax
import jax.numpy as jnp
from jax import lax
from jax.experimental import pallas as pl
from jax.experimental.pallas import tpu as pltpu

F32 = jnp.float32
BF16 = jnp.bfloat16

D_MODEL = 1024
BATCH = 16
SEQ = 256
DEC_BATCH = 4
DEC_SEQ = 1024
GRID_W = 64
CONV_W = 512
CONV_K = 31
CONV_PAD = 15
RWKV_W = 512
HEAD = 64
N_HEADS = 8
LORA = 64
GATE_LORA = 128
N_EXPERTS = 16
D_FF = 2816
D_IN = 2944
CAP = 512
N_TOK = 4096
RMS_EPS = 1e-6
LN_EPS = 1e-5
GN_EPS = 64e-5

ROW_TILE = 512
CONV_TILE = 256
FF_SPLIT = 2
FF_SLAB = D_FF // FF_SPLIT
FF_SUB = 256
EXPERT_VMEM_LIMIT = 60 * 1024 * 1024
CHUNK = 64
HEADS_PER_GROUP = 4
GROUP_W = HEADS_PER_GROUP * HEAD
PASSES_A = 3
PASSES_INV = 1
PASSES_STATE = 1
SCAN_BATCHES = 2
TOK_ROWS = D_MODEL // 128
TOK_STRIDE = CAP + 8
BISECT_ITERS = 160
VMEM_LIMIT = 56 * 1024 * 1024


def _cparams(n_axes, vmem=None):
    return pltpu.CompilerParams(dimension_semantics=("arbitrary",) * n_axes,
                                vmem_limit_bytes=vmem or VMEM_LIMIT)


def _cond_index(i):
    n_prompt = (BATCH * SEQ) // ROW_TILE
    per_batch = DEC_SEQ // ROW_TILE
    return jnp.where(i < n_prompt, 0, 1 + (i - n_prompt) // per_batch)


def _split_dot(x, w_bf16):
    hi = x.astype(BF16)
    lo = (x - hi.astype(F32)).astype(BF16)
    return (jnp.dot(hi, w_bf16, preferred_element_type=F32)
            + jnp.dot(lo, w_bf16, preferred_element_type=F32))


def _sigmoid(x):
    return 1.0 / (1.0 + jnp.exp(-x))


def _mod_kernel(c_ref, w_ref, b_ref, o_ref):
    c = c_ref[...]
    s = (c * _sigmoid(c)).astype(BF16)
    o_ref[...] = jnp.dot(s, w_ref[...].astype(BF16), preferred_element_type=F32) + b_ref[...]


def _modulation(cond8, w_mod, b_mod):
    n = 6 * D_MODEL
    tn = D_MODEL
    return pl.pallas_call(
        _mod_kernel,
        grid=(n // tn,),
        in_specs=[pl.BlockSpec((8, D_MODEL), lambda j: (0, 0)),
                  pl.BlockSpec((D_MODEL, tn), lambda j: (0, j)),
                  pl.BlockSpec((1, tn), lambda j: (0, j))],
        out_specs=pl.BlockSpec((8, tn), lambda j: (0, j)),
        out_shape=jax.ShapeDtypeStruct((8, n), F32),
        compiler_params=_cparams(1),
        name="modulation",
    )(cond8, w_mod, b_mod)


def _group_tile(n_prompt_tiles):
    return (lambda i: (jnp.minimum(i, n_prompt_tiles - 1), 0)), (lambda i: (jnp.maximum(i - n_prompt_tiles, 0), 0))


def _pick(prompt_ref, sample_ref, n_prompt_tiles):
    return jnp.where(pl.program_id(0) < n_prompt_tiles, prompt_ref[...], sample_ref[...])


def _inproj_kernel(xp_ref, xs_ref, sh_ref, sc_ref, g_ref, w_ref, u_ref, rkv_ref, lora_ref):
    x = _pick(xp_ref, xs_ref, N_TOK // ROW_TILE)
    ms = jnp.mean(x * x, axis=-1, keepdims=True)
    h = x * lax.rsqrt(ms + RMS_EPS) * g_ref[...]
    h = (h * (1.0 + sc_ref[0]) + sh_ref[0]).astype(BF16)
    uv = jnp.dot(h, w_ref[:, 0:CONV_W], preferred_element_type=F32)
    ug = jnp.dot(h, w_ref[:, CONV_W:2 * CONV_W], preferred_element_type=F32)
    u_ref[...] = uv * _sigmoid(ug)
    rkv_ref[...] = jnp.dot(h, w_ref[:, 2 * CONV_W:2 * CONV_W + 3 * RWKV_W], preferred_element_type=F32)
    lora_ref[...] = jnp.dot(h, w_ref[:, 2 * CONV_W + 3 * RWKV_W:D_IN], preferred_element_type=F32)


def _inproj(x_prompt, x_sample, mod3, norm1_g, w_in_bf16):
    n = 2 * N_TOK
    n_lora = D_IN - 2 * CONV_W - 3 * RWKV_W
    first, second = _group_tile(N_TOK // ROW_TILE)
    return pl.pallas_call(
        _inproj_kernel,
        grid=(n // ROW_TILE,),
        in_specs=[pl.BlockSpec((ROW_TILE, D_MODEL), first),
                  pl.BlockSpec((ROW_TILE, D_MODEL), second),
                  pl.BlockSpec((1, 1, D_MODEL), lambda i: (_cond_index(i), 0, 0)),
                  pl.BlockSpec((1, 1, D_MODEL), lambda i: (_cond_index(i), 0, 1)),
                  pl.BlockSpec((1, D_MODEL), lambda i: (0, 0)),
                  pl.BlockSpec((D_MODEL, D_IN), lambda i: (0, 0))],
        out_specs=[pl.BlockSpec((ROW_TILE, CONV_W), lambda i: (i, 0)),
                   pl.BlockSpec((ROW_TILE, 3 * RWKV_W), lambda i: (i, 0)),
                   pl.BlockSpec((ROW_TILE, n_lora), lambda i: (i, 0))],
        out_shape=[jax.ShapeDtypeStruct((n, CONV_W), F32),
                   jax.ShapeDtypeStruct((n, 3 * RWKV_W), F32),
                   jax.ShapeDtypeStruct((n, n_lora), F32)],
        compiler_params=_cparams(1),
        name="inproj",
    )(x_prompt, x_sample, mod3, mod3, norm1_g, w_in_bf16)


def _conv_kernel(u_ref, dw_ref, db_ref, g_ref, b_ref, o_ref, *, seg):
    u = u_ref[...]
    pos = lax.broadcasted_iota(jnp.int32, (CONV_TILE, 1), 0) % seg
    acc = jnp.zeros_like(u)
    for k in range(CONV_K):
        s = k - CONV_PAD
        shifted = u if s == 0 else pltpu.roll(u, (-s) % CONV_TILE, 0)
        valid = (pos + s >= 0) & (pos + s < seg)
        acc = acc + jnp.where(valid, shifted, 0.0) * dw_ref[k:k + 1, :]
    out = acc + db_ref[...]
    mu = jnp.mean(out, axis=-1, keepdims=True)
    d = out - mu
    var = jnp.mean(d * d, axis=-1, keepdims=True)
    out = d * lax.rsqrt(var + LN_EPS) * g_ref[...] + b_ref[...]
    o_ref[...] = out * _sigmoid(out)


def _conv_branch(u, dw, db, ln_g, ln_b, seg, row0):
    n = N_TOK
    tile0 = row0 // CONV_TILE
    return pl.pallas_call(
        functools.partial(_conv_kernel, seg=seg),
        grid=(n // CONV_TILE,),
        in_specs=[pl.BlockSpec((CONV_TILE, CONV_W), lambda i: (tile0 + i, 0)),
                  pl.BlockSpec((CONV_K, CONV_W), lambda i: (0, 0)),
                  pl.BlockSpec((1, CONV_W), lambda i: (0, 0)),
                  pl.BlockSpec((1, CONV_W), lambda i: (0, 0)),
                  pl.BlockSpec((1, CONV_W), lambda i: (0, 0))],
        out_specs=pl.BlockSpec((CONV_TILE, CONV_W), lambda i: (i, 0)),
        out_shape=jax.ShapeDtypeStruct((n, CONV_W), F32),
        compiler_params=_cparams(1),
        name=f"conv_seg{seg}",
    )(u, dw, db, ln_g, ln_b)


def _prep_kernel(rkv_ref, lora_ref, ones_ref, w0_ref, wup_ref, a0_ref, aup_ref, gup_ref, kk_ref, ka_ref, rk_ref,
                 kap_o, w0_o, w1_o, b0_o, b1_o, kd0_o, kd1_o, g_o, bonus_o):
    r = rkv_ref[:, 0:RWKV_W]
    k = rkv_ref[:, RWKV_W:2 * RWKV_W]
    v = rkv_ref[:, 2 * RWKV_W:3 * RWKV_W]
    xw = jnp.tanh(lora_ref[:, 0:2 * LORA]).astype(BF16)
    xa = lora_ref[:, 2 * LORA:4 * LORA].astype(BF16)
    xg = _sigmoid(lora_ref[:, 4 * LORA:4 * LORA + GATE_LORA]).astype(BF16)
    ones = ones_ref[...]

    kk0 = k * kk_ref[...]
    ss = _split_dot(kk0 * kk0, ones)
    kap = kk0 * lax.rsqrt(jnp.maximum(ss, 1e-24))
    kap_o[...] = kap
    g_o[...] = jnp.dot(xg, gup_ref[...], preferred_element_type=F32)

    kd_sum = jnp.zeros_like(k)
    for d, (w_o, b_o, kd_o) in enumerate(((w0_o, b0_o, kd0_o), (w1_o, b1_o, kd1_o))):
        wl = w0_ref[d:d + 1, :] + jnp.dot(xw, wup_ref[d], preferred_element_type=F32)
        w_o[...] = -jnp.exp(-0.5) * _sigmoid(wl)
        a = _sigmoid(a0_ref[d:d + 1, :] + jnp.dot(xa, aup_ref[d], preferred_element_type=F32))
        b_o[...] = kap * a
        kd = k * (1.0 + (a - 1.0) * ka_ref[...])
        kd_o[...] = kd
        kd_sum = kd_sum + kd
    bonus_o[...] = _split_dot(r * kd_sum * rk_ref[...], ones) * v


def _rwkv_prep(rkv, lora, ones_blk, w0, w_up, a0, a_up, g_up, k_k, k_a, r_k):
    n = rkv.shape[0]
    row = lambda c: pl.BlockSpec((ROW_TILE, c), lambda i: (i, 0))
    full = lambda a: pl.BlockSpec(a.shape, lambda i: (0,) * a.ndim)
    args = (rkv, lora, ones_blk, w0, w_up, a0, a_up, g_up, k_k, k_a, r_k)
    return pl.pallas_call(
        _prep_kernel,
        grid=(n // ROW_TILE,),
        in_specs=[row(3 * RWKV_W), row(lora.shape[1])] + [full(a) for a in args[2:]],
        out_specs=[row(RWKV_W)] * 9,
        out_shape=[jax.ShapeDtypeStruct((n, RWKV_W), F32)] * 9,
        compiler_params=_cparams(1),
        name="rwkv_prep",
    )(*args)


def _pieces(x, passes):
    hi = x.astype(BF16)
    return (hi,) if passes == 1 else (hi, (x - hi.astype(F32)).astype(BF16))


_NN = ((1,), (0,))
_NT = ((1,), (1,))
_TN = ((0,), (0,))


def _mm(a, b, dims):
    dg = lambda p, q: lax.dot_general(p, q, (dims, ((), ())), preferred_element_type=F32)
    if len(a) == 1:
        out = dg(a[0], b[0])
    else:
        free = 1 - dims[0][0]
        both = dg(jnp.concatenate(a, axis=free), b[0])
        m = a[0].shape[free]
        out = both[:m] + both[m:]
    if len(b) > 1:
        out = out + dg(a[0], b[1])
    return out


def _block_diag(pieces, head_masks):
    return tuple(jnp.concatenate([p * m for m in head_masks], axis=0) for p in pieces)


def _chunk_masks(reverse):
    row = lax.broadcasted_iota(jnp.int32, (CHUNK, GROUP_W), 0)
    col = lax.broadcasted_iota(jnp.int32, (CHUNK, GROUP_W), 1) % CHUNK
    strict = (col > row) if reverse else (col < row)
    incl = (col >= row) if reverse else (col <= row)
    eye_side = jnp.where(col == row, 1.0, 0.0)
    tri = jnp.where(incl[:, :CHUNK], 1.0, 0.0).astype(BF16)
    return strict, incl, eye_side, tri, (0 if reverse else CHUNK - 1)


def _chunk_steps(chains):
    ln = CHUNK
    each = lambda f, *cols: [f(*args) for args in zip(*cols)]
    r, kap, v, lw, b, kd, n, masks = (list(col) for col in zip(*chains))
    strict = [m[0] for m in masks]
    incl = [m[1] for m in masks]
    lane_blk = lax.broadcasted_iota(jnp.int32, (ln, GROUP_W), 1) // HEAD
    head_masks = [jnp.where(lane_blk == h, 1.0, 0.0).astype(BF16) for h in range(HEADS_PER_GROUP)]
    bd = lambda x, passes: _block_diag(_pieces(x, passes), head_masks)
    r2 = lax.broadcasted_iota(jnp.int32, (GROUP_W, GROUP_W), 0) // HEAD
    c2 = lax.broadcasted_iota(jnp.int32, (GROUP_W, GROUP_W), 1) // HEAD
    same_head = r2 == c2

    def running_sum(lw_c, m):
        return _mm((m[3],), _pieces(lw_c, 3), _NN)

    cw = each(running_sum, lw, masks)
    tot = each(lambda cw_c, m: cw_c[m[4]:m[4] + 1, :], cw, masks)
    e_neg = each(lambda cw_c: jnp.exp(-cw_c), cw)
    kr = each(lambda kap_c, r_c, cw_c, lw_c: _pieces(
        jnp.concatenate([kap_c * jnp.exp(cw_c - lw_c), r_c * jnp.exp(cw_c)], axis=0), PASSES_A),
        kap, r, cw, lw)
    ab = each(lambda kr_c, b_c, e: _mm(kr_c, bd(b_c * e, PASSES_A), _NT), kr, b, e_neg)
    ak = each(lambda kr_c, kd_c, e: _mm(kr_c, bd(kd_c * e, PASSES_A), _NT), kr, kd, e_neg)

    q = each(lambda ab_c, s: jnp.where(s, -ab_c[:ln], 0.0), ab, strict)
    x = each(lambda q_c, m: m[2] + q_c, q, masks)
    q = each(lambda q_c: _mm(_pieces(q_c, PASSES_INV), bd(q_c, PASSES_INV), _NN), q)
    n_rounds = CHUNK.bit_length() - 2
    for j in range(1, n_rounds + 1):
        last_round = j == n_rounds
        lhs = x if last_round else each(lambda q_c, x_c: jnp.concatenate([q_c, x_c], axis=0), q, x)
        prod = each(lambda l_c, q_c: _mm(_pieces(l_c, PASSES_INV), bd(q_c, PASSES_INV), _NN), lhs, q)
        if last_round:
            x = each(lambda x_c, p_c: x_c + p_c, x, prod)
        else:
            q = each(lambda p_c: p_c[:ln], prod)
            x = each(lambda x_c, p_c: x_c + p_c[ln:], x, prod)

    a3 = each(lambda ab_c, i: jnp.where(i, ab_c[ln:], 0.0), ab, incl)
    a24 = each(lambda ak_c, s, i: jnp.concatenate([jnp.where(s, ak_c[:ln], 0.0), jnp.where(i, ak_c[ln:], 0.0)], axis=0),
               ak, strict, incl)
    av = each(lambda a_c, v_c: _mm(_pieces(a_c, PASSES_A), bd(v_c, PASSES_A), _NN), a24, v)
    krn = each(lambda kr_c, n_c: _mm(kr_c[:PASSES_STATE], _pieces(n_c, PASSES_STATE), _NT), kr, n)
    u = each(lambda x_c, krn_c, av_c: -_mm(_pieces(x_c, PASSES_INV), bd(krn_c[:ln] + av_c[:ln], PASSES_INV), _NN),
             x, krn, av)
    y = each(lambda krn_c, a3_c, u_c, av_c: krn_c[ln:] + _mm(_pieces(a3_c, PASSES_STATE), bd(u_c, PASSES_STATE), _NN)
             + av_c[ln:], krn, a3, u, av)
    e_rest = each(lambda tot_c, cw_c: jnp.exp(tot_c - cw_c), tot, cw)
    upd = each(lambda u_c, v_c, b_c, kd_c, e: _mm(_pieces(jnp.concatenate([u_c, v_c], axis=0), PASSES_A),
                                                   _pieces(jnp.concatenate([b_c * e, kd_c * e], axis=0), PASSES_A), _TN),
               u, v, b, kd, e_rest)
    n_new = each(lambda n_c, tot_c, upd_c: n_c * jnp.exp(tot_c) + jnp.where(same_head, upd_c, 0.0), n, tot, upd)
    return list(zip(y, n_new))


def _chunk_kernel(*refs, has_init, has_final):
    n_tok = 12 * SCAN_BATCHES
    tok_refs = [refs[k * 12:(k + 1) * 12] for k in range(SCAN_BATCHES)]
    refs = refs[n_tok:]
    s0 = refs[0] if has_init else None
    refs = refs[1:] if has_init else refs
    y_refs = refs[:2]
    sfin = refs[2] if has_final else None
    n_scr = refs[-1]
    c = pl.program_id(1)
    head_blocks = [(k, d, g, h) for k in range(SCAN_BATCHES) for d in range(2)
                   for g in range(N_HEADS // HEADS_PER_GROUP) for h in range(HEADS_PER_GROUP)]
    diag = lambda h: slice(h * HEAD, (h + 1) * HEAD)

    @pl.when(c == 0)
    def _():
        n_scr[...] = jnp.zeros(n_scr.shape, F32)
        if has_init:
            for k, d, g, h in head_blocks:
                n_scr[d, k, g, diag(h), diag(h)] = s0[k, d, g * HEADS_PER_GROUP + h]

    slots = [(k, d, g) for k in range(SCAN_BATCHES) for d in range(2) for g in range(N_HEADS // HEADS_PER_GROUP)]
    lanes = lambda g: slice(g * GROUP_W, (g + 1) * GROUP_W)
    masks = [_chunk_masks(reverse=False), _chunk_masks(reverse=True)]

    def operands(k, d, g):
        rf, kf, vf, rb, kb, vb, lwf, bf, kdf, lwb, bb, kdb = tok_refs[k]
        group = (rb, kb, vb, lwb, bb, kdb) if d else (rf, kf, vf, lwf, bf, kdf)
        return tuple(ref[:, lanes(g)] for ref in group)

    chains = [operands(k, d, g) + (n_scr[d, k, g], masks[d]) for k, d, g in slots]
    for (k, d, g), (y, n_new) in zip(slots, _chunk_steps(chains)):
        y_refs[d][k, :, lanes(g)] = y
        n_scr[d, k, g] = n_new

    if has_final:
        @pl.when(c == pl.num_programs(1) - 1)
        def _():
            for k, d, g, h in head_blocks:
                sfin[k, d, g * HEADS_PER_GROUP + h] = n_scr[d, k, g, diag(h), diag(h)]


def _chunk_scan(rkv, kap, dir_ops, s0, *, row0, n_batch, t_len, want_final):
    n_chunks = t_len // CHUNK
    blk0 = row0 // CHUNK
    n_grp = N_HEADS // HEADS_PER_GROUP
    nb = SCAN_BATCHES

    def tok_spec(k, reverse, col=0):
        def index(bi, ci):
            return (blk0 + (bi * nb + k) * n_chunks + (n_chunks - 1 - ci if reverse else ci), col)
        return pl.BlockSpec((CHUNK, RWKV_W), index)

    in_specs, args = [], []
    for k in range(nb):
        for rev in (False, True):
            in_specs += [tok_spec(k, rev, 0), tok_spec(k, rev), tok_spec(k, rev, 2)]
        in_specs += [tok_spec(k, False)] * 3 + [tok_spec(k, True)] * 3
        args += [rkv, kap, rkv, rkv, kap, rkv, *dir_ops[0], *dir_ops[1]]
    state = pl.BlockSpec((nb, 2, N_HEADS, HEAD, HEAD), lambda bi, ci: (bi, 0, 0, 0, 0))
    has_init = s0 is not None
    if has_init:
        in_specs.append(state)
        args.append(s0)
    y_shape = jax.ShapeDtypeStruct((n_batch, t_len, RWKV_W), F32)
    out_specs = [pl.BlockSpec((nb, CHUNK, RWKV_W), lambda bi, ci: (bi, ci, 0)),
                 pl.BlockSpec((nb, CHUNK, RWKV_W), lambda bi, ci: (bi, n_chunks - 1 - ci, 0))]
    out_shape = [y_shape, y_shape]
    if want_final:
        out_specs.append(state)
        out_shape.append(jax.ShapeDtypeStruct((n_batch, 2, N_HEADS, HEAD, HEAD), F32))
    outs = pl.pallas_call(
        functools.partial(_chunk_kernel, has_init=has_init, has_final=want_final),
        grid=(n_batch // nb, n_chunks),
        in_specs=in_specs,
        out_specs=out_specs,
        out_shape=out_shape,
        scratch_shapes=[pltpu.VMEM((2, nb, n_grp, GROUP_W, GROUP_W), F32)],
        compiler_params=_cparams(2),
        name=f"wkv_chunk_t{t_len}",
    )(*args)
    return [outs[0].reshape(n_batch * t_len, RWKV_W), outs[1].reshape(n_batch * t_len, RWKV_W)] + list(outs[2:])


def _postmix_kernel(yfp_ref, yfs_ref, ybp_ref, ybs_ref, convp_ref, convs_ref, xp_ref, xs_ref, bonus_ref, g_ref,
                    g1_ref, sh2_ref, sc2_ref, gng_ref, gnb_ref, ones_ref, wout_ref, n2g_ref, rhi_ref, rlo_ref,
                    x1_o, h2_o, logit_o):
    n_prompt = N_TOK // ROW_TILE
    ones = ones_ref[...]
    y = _pick(yfp_ref, yfs_ref, n_prompt) + _pick(ybp_ref, ybs_ref, n_prompt)
    mu = _split_dot(y, ones) * (1.0 / HEAD)
    d = y - mu
    var = _split_dot(d * d, ones) * (1.0 / HEAD)
    yn = d * lax.rsqrt(var + GN_EPS) * gng_ref[...] + gnb_ref[...]
    rw = ((yn + bonus_ref[...]) * g_ref[...]).astype(BF16)
    conv = _pick(convp_ref, convs_ref, n_prompt).astype(BF16)
    mix = (jnp.dot(conv, wout_ref[0:CONV_W, :], preferred_element_type=F32)
           + jnp.dot(rw, wout_ref[CONV_W:CONV_W + RWKV_W, :], preferred_element_type=F32))
    x1 = _pick(xp_ref, xs_ref, n_prompt) + g1_ref[0] * mix
    x1_o[...] = x1
    ms = jnp.mean(x1 * x1, axis=-1, keepdims=True)
    h2 = x1 * lax.rsqrt(ms + RMS_EPS) * n2g_ref[...]
    h2 = h2 * (1.0 + sc2_ref[0]) + sh2_ref[0]
    for c in range(TOK_ROWS):
        h2_o[pl.ds(c, ROW_TILE, stride=TOK_ROWS), :] = h2[:, c * 128:(c + 1) * 128]
    hi = h2.astype(BF16)
    lo = (h2 - hi.astype(F32)).astype(BF16)
    logit_o[...] = (jnp.dot(hi, rhi_ref[...], preferred_element_type=F32)
                    + jnp.dot(lo, rhi_ref[...], preferred_element_type=F32)
                    + jnp.dot(hi, rlo_ref[...], preferred_element_type=F32))


def _postmix(yf, yb, conv, x, bonus, g, mod3, gn_g, gn_b, ones_blk, w_out_bf16, norm2_g, r_hi, r_lo):
    n = 2 * N_TOK
    row = lambda c: pl.BlockSpec((ROW_TILE, c), lambda i: (i, 0))
    pair = lambda c: [pl.BlockSpec((ROW_TILE, c), m) for m in _group_tile(N_TOK // ROW_TILE)]
    full = lambda a: pl.BlockSpec(a.shape, lambda i: (0,) * a.ndim)
    modspec = lambda j: pl.BlockSpec((1, 1, D_MODEL), lambda i: (_cond_index(i), 0, j))
    return pl.pallas_call(
        _postmix_kernel,
        grid=(n // ROW_TILE,),
        in_specs=pair(RWKV_W) + pair(RWKV_W) + pair(CONV_W) + pair(D_MODEL) + [
            row(RWKV_W), row(RWKV_W), modspec(2), modspec(3), modspec(4), full(gn_g), full(gn_b), full(ones_blk),
            full(w_out_bf16), full(norm2_g), full(r_hi), full(r_lo)],
        out_specs=[row(D_MODEL), pl.BlockSpec((ROW_TILE * TOK_ROWS, 128), lambda i: (i, 0)), row(128)],
        out_shape=[jax.ShapeDtypeStruct((n, D_MODEL), F32), jax.ShapeDtypeStruct((n * TOK_ROWS, 128), F32),
                   jax.ShapeDtypeStruct((n, 128), F32)],
        compiler_params=_cparams(1),
        name="postmix",
    )(*yf, *yb, *conv, *x, bonus, g, mod3, mod3, mod3, gn_g, gn_b, ones_blk, w_out_bf16, norm2_g, r_hi, r_lo)


def _prefix_lanes(m, utri):
    rows = m.shape[0]
    off = jnp.zeros((rows, 1), F32)
    out = []
    for c in range(N_TOK // 128):
        blk = m[:, c * 128:(c + 1) * 128]
        out.append(jnp.dot(blk.astype(BF16), utri, preferred_element_type=F32) + off)
        off = off + jnp.sum(blk, axis=1, keepdims=True)
    return jnp.concatenate(out, axis=1)


def _select_kernel(lt_ref, utri_ref, pos_o, aff_o):
    utri = utri_ref[...]
    affs = []
    for grp in range(2):
        x = lt_ref[grp * N_EXPERTS:(grp + 1) * N_EXPERTS, :]
        e = jnp.exp(x - jnp.max(x, axis=0, keepdims=True))
        affs.append(e / jnp.sum(e, axis=0, keepdims=True))
    aff = jnp.concatenate(affs, axis=0)
    aff_o[...] = aff
    n_rows = 2 * N_EXPERTS

    def body(_, lh):
        lo, hi = lh
        m2 = 0.5 * (lo + hi)
        m1 = 0.5 * (lo + m2)
        m3 = 0.5 * (m2 + hi)
        ge1, ge2, ge3 = (jnp.sum(jnp.where(aff > m, 1.0, 0.0), axis=1, keepdims=True) >= CAP for m in (m1, m2, m3))
        lo = jnp.where(ge3, m3, jnp.where(ge2, m2, jnp.where(ge1, m1, lo)))
        hi = jnp.where(ge3, hi, jnp.where(ge2, m3, jnp.where(ge1, m2, m1)))
        return lo, hi

    lo, hi = lax.fori_loop(0, BISECT_ITERS // 2, body,
                           (jnp.full((n_rows, 1), -1.0, F32), jnp.full((n_rows, 1), 1.0, F32)))
    gt = jnp.where(aff > hi, 1.0, 0.0)
    tie = jnp.where(aff > lo, 1.0, 0.0) - gt
    need = CAP - jnp.sum(gt, axis=1, keepdims=True)
    sel = gt + tie * jnp.where(_prefix_lanes(tie, utri) < need, 1.0, 0.0)
    pos = _prefix_lanes(sel, utri)
    pos_o[...] = jnp.where(sel > 0.5, pos, -1.0)


def _select(logits_t, utri):
    shp = jax.ShapeDtypeStruct((2 * N_EXPERTS, N_TOK), F32)
    full = lambda a: pl.BlockSpec(a.shape, lambda i: (0,) * a.ndim)
    return pl.pallas_call(
        _select_kernel,
        grid=(1,),
        in_specs=[full(logits_t), full(utri)],
        out_specs=[pl.BlockSpec(shp.shape, lambda i: (0, 0))] * 2,
        out_shape=[shp, shp],
        compiler_params=_cparams(1),
        name="ec_select",
    )(logits_t, utri)


def _compact_kernel(pos_ref, aff_ref, idx_o, gate_o):
    slot = lax.broadcasted_iota(jnp.int32, (CAP, 1), 0).astype(F32)
    idx = jnp.zeros((CAP, 128), F32)
    gate = jnp.zeros((CAP, 128), F32)
    for c in range(N_TOK // 128):
        p = pos_ref[0, :, c * 128:(c + 1) * 128]
        a = aff_ref[0, :, c * 128:(c + 1) * 128]
        tok = (lax.broadcasted_iota(jnp.int32, (1, 128), 1) + c * 128).astype(F32)
        hit = p == slot
        idx = idx + jnp.where(hit, tok, 0.0)
        gate = gate + jnp.where(hit, a, 0.0)
    idx_o[0] = jnp.broadcast_to(jnp.sum(idx, axis=1, keepdims=True), (CAP, 128))
    gate_o[0] = jnp.broadcast_to(jnp.sum(gate, axis=1, keepdims=True), (CAP, 128))


def _compact(pos3, aff3):
    rows = pos3.shape[0]
    shp = jax.ShapeDtypeStruct((rows, CAP, 128), F32)
    return pl.pallas_call(
        _compact_kernel,
        grid=(rows,),
        in_specs=[pl.BlockSpec((1, 1, N_TOK), lambda i: (i, 0, 0))] * 2,
        out_specs=[pl.BlockSpec((1, CAP, 128), lambda i: (i, 0, 0))] * 2,
        out_shape=[shp, shp],
        compiler_params=_cparams(1),
        name="ec_compact",
    )(pos3, aff3)


def _gather_kernel(idx_ref, h_ref, o_ref, tile):
    grp = pl.program_id(0)
    e = pl.program_id(1)
    base = (grp * N_EXPERTS + e) * CAP

    def tok(j8, c):
        for u in range(8):
            j = j8 * 8 + u
            row = pl.multiple_of(idx_ref[base + j] * 8, 8)
            tile[pl.ds(j, 8, stride=TOK_STRIDE), :] = h_ref[0, pl.ds(row, 8), :]
        return c

    lax.fori_loop(0, CAP // 8, tok, 0)
    for c in range(8):
        o_ref[0, :, c * 128:(c + 1) * 128] = tile[c * TOK_STRIDE:c * TOK_STRIDE + CAP, :].astype(BF16)


def _gather(idx_flat, h_rows):
    return pl.pallas_call(
        _gather_kernel,
        grid_spec=pltpu.PrefetchScalarGridSpec(
            num_scalar_prefetch=1,
            grid=(2, N_EXPERTS),
            in_specs=[pl.BlockSpec((1, N_TOK * 8, 128), lambda g, e, idx: (g, 0, 0))],
            out_specs=pl.BlockSpec((1, CAP, D_MODEL), lambda g, e, idx: (e, g, 0)),
            scratch_shapes=[pltpu.VMEM((8 * TOK_STRIDE, 128), F32)]),
        out_shape=jax.ShapeDtypeStruct((N_EXPERTS, 2 * CAP, D_MODEL), BF16),
        compiler_params=_cparams(2),
        name="ec_gather",
    )(idx_flat, h_rows)


def _expert_kernel(xe_ref, w1_ref, w3_ref, w2_ref, gate_ref, o_ref, hid_scr):
    s = pl.program_id(1)
    subs = [(lo, min(FF_SUB, FF_SLAB - lo)) for lo in range(0, FF_SLAB, FF_SUB)]

    @pl.when(s < FF_SPLIT)
    def _():
        x = xe_ref[0]
        for lo, width in subs:
            h1 = jnp.dot(x, w1_ref[0, :, lo:lo + width].astype(BF16), preferred_element_type=F32)
            h3 = jnp.dot(x, w3_ref[0, :, lo:lo + width].astype(BF16), preferred_element_type=F32)
            hid_scr[s, :, lo:lo + width] = (h1 * _sigmoid(h1) * h3).astype(BF16)

    @pl.when(s >= FF_SPLIT)
    def _():
        acc = None
        for lo, width in subs:
            part = jnp.dot(hid_scr[s - FF_SPLIT, :, lo:lo + width], w2_ref[0, lo:lo + width, :].astype(BF16),
                           preferred_element_type=F32)
            acc = part if acc is None else acc + part

        @pl.when(s == FF_SPLIT)
        def _():
            o_ref[0] = acc

        @pl.when(s > FF_SPLIT)
        def _():
            o_ref[0] = o_ref[0] + acc

    @pl.when(s == 2 * FF_SPLIT - 1)
    def _():
        o_ref[0] = o_ref[0] * gate_ref[0]


def _experts(xe, w1, w3, w2, gate_col):
    m = xe.shape[1]
    up = lambda e, s: (e, 0, jnp.minimum(s, FF_SPLIT - 1))
    down = lambda e, s: (e, jnp.maximum(s - FF_SPLIT, 0), 0)
    return pl.pallas_call(
        _expert_kernel,
        grid=(N_EXPERTS, 2 * FF_SPLIT),
        in_specs=[pl.BlockSpec((1, m, D_MODEL), lambda e, s: (e, 0, 0)),
                  pl.BlockSpec((1, D_MODEL, FF_SLAB), up),
                  pl.BlockSpec((1, D_MODEL, FF_SLAB), up),
                  pl.BlockSpec((1, FF_SLAB, D_MODEL), down),
                  pl.BlockSpec((1, m, 1), lambda e, s: (e, 0, 0))],
        out_specs=pl.BlockSpec((1, m, D_MODEL), lambda e, s: (e, 0, 0)),
        out_shape=jax.ShapeDtypeStruct((N_EXPERTS, m, D_MODEL), F32),
        scratch_shapes=[pltpu.VMEM((FF_SPLIT, m, FF_SLAB), BF16)],
        compiler_params=_cparams(2, vmem=EXPERT_VMEM_LIMIT),
        name="ec_experts",
    )(xe, w1, w3, w2, gate_col)


def _combine_kernel(idx_ref, ye_ref, o_ref, tile):
    grp = pl.program_id(0)
    e = pl.program_id(1)
    base = (grp * N_EXPERTS + e) * CAP

    @pl.when(e == 0)
    def _():
        o_ref[...] = jnp.zeros_like(o_ref)

    for c in range(8):
        tile[c * TOK_STRIDE:c * TOK_STRIDE + CAP, :] = ye_ref[0, :, c * 128:(c + 1) * 128]

    def tok(j4, c):
        rows, vals = [], []
        for u in range(4):
            j = j4 * 4 + u
            row = pl.multiple_of(idx_ref[base + j] * 8, 8)
            rows.append(row)
            vals.append(o_ref[0, pl.ds(row, 8), :] + tile[pl.ds(j, 8, stride=TOK_STRIDE), :])
        for row, val in zip(rows, vals):
            o_ref[0, pl.ds(row, 8), :] = val
        return c

    lax.fori_loop(0, CAP // 4, tok, 0)


def _combine(idx_flat, ye):
    return pl.pallas_call(
        _combine_kernel,
        grid_spec=pltpu.PrefetchScalarGridSpec(
            num_scalar_prefetch=1,
            grid=(2, N_EXPERTS),
            in_specs=[pl.BlockSpec((1, CAP, D_MODEL), lambda g, e, idx: (e, g, 0))],
            out_specs=pl.BlockSpec((1, N_TOK * 8, 128), lambda g, e, idx: (g, 0, 0)),
            scratch_shapes=[pltpu.VMEM((8 * TOK_STRIDE, 128), F32)]),
        out_shape=jax.ShapeDtypeStruct((2, N_TOK * 8, 128), F32),
        compiler_params=_cparams(2),
        name="ec_combine",
    )(idx_flat, ye)


def _final_kernel(x1_ref, y_ref, g2_ref, fg_ref, op_ref, os_ref):
    i = pl.program_id(0)
    n_prompt = N_TOK // ROW_TILE
    y = jnp.concatenate([y_ref[pl.ds(c, ROW_TILE, stride=TOK_ROWS), :] for c in range(TOK_ROWS)], axis=1)
    x = x1_ref[...] + g2_ref[0] * y
    ms = jnp.mean(x * x, axis=-1, keepdims=True)
    out = x * lax.rsqrt(ms + RMS_EPS) * fg_ref[...]

    @pl.when(i < n_prompt)
    def _():
        op_ref[...] = out

    @pl.when(i >= n_prompt)
    def _():
        os_ref[...] = out


def _final(x1, y_rows, mod3, final_g):
    n = x1.shape[0]
    row = pl.BlockSpec((ROW_TILE, D_MODEL), lambda i: (i, 0))
    first, second = _group_tile(N_TOK // ROW_TILE)
    shp = jax.ShapeDtypeStruct((N_TOK, D_MODEL), F32)
    return pl.pallas_call(
        _final_kernel,
        grid=(n // ROW_TILE,),
        in_specs=[row, pl.BlockSpec((ROW_TILE * TOK_ROWS, 128), lambda i: (i, 0)),
                  pl.BlockSpec((1, 1, D_MODEL), lambda i: (_cond_index(i), 0, 5)),
                  pl.BlockSpec((1, D_MODEL), lambda i: (0, 0))],
        out_specs=[pl.BlockSpec((ROW_TILE, D_MODEL), first), pl.BlockSpec((ROW_TILE, D_MODEL), second)],
        out_shape=[shp, shp],
        compiler_params=_cparams(1),
        name="final_norm",
    )(x1, y_rows, mod3, final_g)


def kernel(x_prompt, x_sample, state_wkv, c, c_ctx, norm1_g, norm2_g, w_mod, b_mod, w_in, conv_dw, conv_b, conv_ln_g,
           conv_ln_b, rwkv_w0, rwkv_w_up, rwkv_a0, rwkv_a_up, rwkv_g_up, rwkv_k_k, rwkv_k_a, rwkv_r_k, rwkv_gn_g,
           rwkv_gn_b, w_out, router, exp_w1, exp_w3, exp_w2, final_norm_g):
    row2 = lambda a: a.reshape(1, -1)
    x = (x_prompt.reshape(N_TOK, D_MODEL), x_sample.reshape(N_TOK, D_MODEL))
    cond8 = jnp.concatenate([c_ctx[None, :], c, jnp.zeros((8 - 1 - DEC_BATCH, D_MODEL), F32)], axis=0)

    mod = _modulation(cond8, w_mod[0], b_mod)
    mod3 = mod.reshape(8, 1, 6 * D_MODEL)

    u, rkv, lora = _inproj(*x, mod3, norm1_g, w_in[0].astype(BF16))

    conv_args = (conv_dw[0], conv_b, conv_ln_g, conv_ln_b)
    conv = (_conv_branch(u, *conv_args, seg=SEQ, row0=0), _conv_branch(u, *conv_args, seg=GRID_W, row0=N_TOK))

    lane_head = jnp.arange(RWKV_W, dtype=jnp.int32) // HEAD
    ones_blk = (lane_head[:, None] == lane_head[None, :]).astype(BF16)

    def per_dir_rows(up):
        z = jnp.zeros_like(up[0])
        return jnp.stack([jnp.concatenate([up[0], z], axis=0), jnp.concatenate([z, up[1]], axis=0)]).astype(BF16)
    (kap, w0, w1, b0, b1, kd0, kd1, g, bonus) = _rwkv_prep(
        rkv, lora, ones_blk, rwkv_w0[0], per_dir_rows(rwkv_w_up[0]), rwkv_a0[0], per_dir_rows(rwkv_a_up[0]),
        rwkv_g_up[0].astype(BF16), rwkv_k_k, rwkv_k_a, row2(rwkv_r_k[0]))

    dir_ops = ((w0, b0, kd0), (w1, b1, kd1))
    yp_f, yp_b, new_state = _chunk_scan(rkv, kap, dir_ops, None, row0=0, n_batch=BATCH, t_len=SEQ, want_final=True)
    ys_f, ys_b = _chunk_scan(rkv, kap, dir_ops, state_wkv[:, 0], row0=N_TOK, n_batch=DEC_BATCH, t_len=DEC_SEQ,
                             want_final=False)

    r_pad = jnp.pad(router[0], ((0, 0), (0, 128 - N_EXPERTS)))
    r_hi = r_pad.astype(BF16)
    r_lo = (r_pad - r_hi.astype(F32)).astype(BF16)
    x1, h2_rows, logits = _postmix((yp_f, ys_f), (yp_b, ys_b), conv, x, bonus, g, mod3, rwkv_gn_g, rwkv_gn_b, ones_blk,
                                   w_out[0].astype(BF16), norm2_g, r_hi, r_lo)

    logits_t = logits[:, :N_EXPERTS].reshape(2, N_TOK, N_EXPERTS).transpose(0, 2, 1).reshape(2 * N_EXPERTS, N_TOK)
    lane = jnp.arange(128, dtype=jnp.int32)
    utri = (lane[:, None] < lane[None, :]).astype(BF16)
    pos, aff = _select(logits_t, utri)
    idx_c, gate_c = _compact(pos.reshape(2 * N_EXPERTS, 1, N_TOK), aff.reshape(2 * N_EXPERTS, 1, N_TOK))
    idx_flat = idx_c[:, :, 0].astype(jnp.int32).reshape(-1)
    gate_col = gate_c[:, :, 0].reshape(2, N_EXPERTS, CAP).transpose(1, 0, 2).reshape(N_EXPERTS, 2 * CAP, 1)

    xe = _gather(idx_flat, h2_rows.reshape(2, N_TOK * TOK_ROWS, 128))
    ye = _experts(xe, exp_w1[0], exp_w3[0], exp_w2[0], gate_col)
    y_rows = _combine(idx_flat, ye).reshape(2 * N_TOK * TOK_ROWS, 128)

    out_p, out_s = _final(x1, y_rows, mod3, row2(final_norm_g))
    return (out_p.reshape(BATCH, SEQ, D_MODEL), out_s.reshape(DEC_BATCH, DEC_SEQ, D_MODEL),
            new_state.reshape(BATCH, 1, 2, N_HEADS, HEAD, HEAD))
```

```python
import functools

import jax
import jax.numpy as jnp
from jax import lax
from jax.experimental import pallas as pl
from jax.experimental.pallas import tpu as pltpu

F32 = jnp.float32
BF16 = jnp.bfloat16

D_MODEL = 1024
BATCH = 16
SEQ = 256
DEC_BATCH = 4
DEC_SEQ = 1024
GRID_W = 64
CONV_W = 512
CONV_K = 31
CONV_PAD = 15
RWKV_W = 512
HEAD = 64
N_HEADS = 8
LORA = 64
GATE_LORA = 128
N_EXPERTS = 16
D_FF = 2816
D_IN = 2944
CAP = 512
N_TOK = 4096
RMS_EPS = 1e-6
LN_EPS = 1e-5
GN_EPS = 64e-5

ROW_TILE = 512
CONV_TILE = 256
FF_SPLIT = 2
FF_SLAB = D_FF // FF_SPLIT
FF_SUB = 512
EXPERT_VMEM_LIMIT = 60 * 1024 * 1024
CHUNK = 64
HEADS_PER_GROUP = 4
GROUP_W = HEADS_PER_GROUP * HEAD
PASSES_A = 3
PASSES_INV = 1
PASSES_STATE = 1
SCAN_BATCHES = 4
TOK_ROWS = D_MODEL // 128
TOK_STRIDE = CAP + 8
ROUTE_EXPERTS = 4
BISECT_ITERS = 160
VMEM_LIMIT = 56 * 1024 * 1024


def _cparams(n_axes, vmem=None):
    return pltpu.CompilerParams(dimension_semantics=("arbitrary",) * n_axes,
                                vmem_limit_bytes=vmem or VMEM_LIMIT)


def _cond_index(i):
    n_prompt = (BATCH * SEQ) // ROW_TILE
    per_batch = DEC_SEQ // ROW_TILE
    return jnp.where(i < n_prompt, 0, 1 + (i - n_prompt) // per_batch)


def _split_dot(x, w_bf16):
    hi = x.astype(BF16)
    lo = (x - hi.astype(F32)).astype(BF16)
    return (jnp.dot(hi, w_bf16, preferred_element_type=F32)
            + jnp.dot(lo, w_bf16, preferred_element_type=F32))


def _sigmoid(x):
    return 1.0 / (1.0 + jnp.exp(-x))


def _mod_kernel(c_ref, w_ref, b_ref, o_ref):
    c = c_ref[...]
    s = (c * _sigmoid(c)).astype(BF16)
    o_ref[...] = jnp.dot(s, w_ref[...].astype(BF16), preferred_element_type=F32) + b_ref[...]


def _modulation(cond8, w_mod, b_mod):
    n = 6 * D_MODEL
    tn = D_MODEL
    return pl.pallas_call(
        _mod_kernel,
        grid=(n // tn,),
        in_specs=[pl.BlockSpec((8, D_MODEL), lambda j: (0, 0)),
                  pl.BlockSpec((D_MODEL, tn), lambda j: (0, j)),
                  pl.BlockSpec((1, tn), lambda j: (0, j))],
        out_specs=pl.BlockSpec((8, tn), lambda j: (0, j)),
        out_shape=jax.ShapeDtypeStruct((8, n), F32),
        compiler_params=_cparams(1),
        name="modulation",
    )(cond8, w_mod, b_mod)


def _group_tile(n_prompt_tiles):
    return (lambda i: (jnp.minimum(i, n_prompt_tiles - 1), 0)), (lambda i: (jnp.maximum(i - n_prompt_tiles, 0), 0))


def _pick(prompt_ref, sample_ref, n_prompt_tiles):
    return jnp.where(pl.program_id(0) < n_prompt_tiles, prompt_ref[...], sample_ref[...])


def _inproj_kernel(xp_ref, xs_ref, sh_ref, sc_ref, g_ref, w_ref, u_ref, rkv_ref, lora_ref):
    x = _pick(xp_ref, xs_ref, N_TOK // ROW_TILE)
    ms = jnp.mean(x * x, axis=-1, keepdims=True)
    h = x * lax.rsqrt(ms + RMS_EPS) * g_ref[...]
    h = (h * (1.0 + sc_ref[0]) + sh_ref[0]).astype(BF16)
    uv = jnp.dot(h, w_ref[:, 0:CONV_W], preferred_element_type=F32)
    ug = jnp.dot(h, w_ref[:, CONV_W:2 * CONV_W], preferred_element_type=F32)
    u_ref[...] = uv * _sigmoid(ug)
    rkv_ref[...] = jnp.dot(h, w_ref[:, 2 * CONV_W:2 * CONV_W + 3 * RWKV_W], preferred_element_type=F32)
    lora_ref[...] = jnp.dot(h, w_ref[:, 2 * CONV_W + 3 * RWKV_W:D_IN], preferred_element_type=F32)


def _inproj(x_prompt, x_sample, mod3, norm1_g, w_in_bf16):
    n = 2 * N_TOK
    n_lora = D_IN - 2 * CONV_W - 3 * RWKV_W
    first, second = _group_tile(N_TOK // ROW_TILE)
    return pl.pallas_call(
        _inproj_kernel,
        grid=(n // ROW_TILE,),
        in_specs=[pl.BlockSpec((ROW_TILE, D_MODEL), first),
                  pl.BlockSpec((ROW_TILE, D_MODEL), second),
                  pl.BlockSpec((1, 1, D_MODEL), lambda i: (_cond_index(i), 0, 0)),
                  pl.BlockSpec((1, 1, D_MODEL), lambda i: (_cond_index(i), 0, 1)),
                  pl.BlockSpec((1, D_MODEL), lambda i: (0, 0)),
                  pl.BlockSpec((D_MODEL, D_IN), lambda i: (0, 0))],
        out_specs=[pl.BlockSpec((ROW_TILE, CONV_W), lambda i: (i, 0)),
                   pl.BlockSpec((ROW_TILE, 3 * RWKV_W), lambda i: (i, 0)),
                   pl.BlockSpec((ROW_TILE, n_lora), lambda i: (i, 0))],
        out_shape=[jax.ShapeDtypeStruct((n, CONV_W), F32),
                   jax.ShapeDtypeStruct((n, 3 * RWKV_W), F32),
                   jax.ShapeDtypeStruct((n, n_lora), F32)],
        compiler_params=_cparams(1),
        name="inproj",
    )(x_prompt, x_sample, mod3, mod3, norm1_g, w_in_bf16)


def _conv_kernel(u_ref, dw_ref, db_ref, g_ref, b_ref, o_ref, *, seg):
    u = u_ref[...]
    pos = lax.broadcasted_iota(jnp.int32, (CONV_TILE, 1), 0) % seg
    acc = jnp.zeros_like(u)
    for k in range(CONV_K):
        s = k - CONV_PAD
        shifted = u if s == 0 else pltpu.roll(u, (-s) % CONV_TILE, 0)
        valid = (pos + s >= 0) & (pos + s < seg)
        acc = acc + jnp.where(valid, shifted, 0.0) * dw_ref[k:k + 1, :]
    out = acc + db_ref[...]
    mu = jnp.mean(out, axis=-1, keepdims=True)
    d = out - mu
    var = jnp.mean(d * d, axis=-1, keepdims=True)
    out = d * lax.rsqrt(var + LN_EPS) * g_ref[...] + b_ref[...]
    o_ref[...] = out * _sigmoid(out)


def _conv_branch(u, dw, db, ln_g, ln_b, seg, row0):
    n = N_TOK
    tile0 = row0 // CONV_TILE
    return pl.pallas_call(
        functools.partial(_conv_kernel, seg=seg),
        grid=(n // CONV_TILE,),
        in_specs=[pl.BlockSpec((CONV_TILE, CONV_W), lambda i: (tile0 + i, 0)),
                  pl.BlockSpec((CONV_K, CONV_W), lambda i: (0, 0)),
                  pl.BlockSpec((1, CONV_W), lambda i: (0, 0)),
                  pl.BlockSpec((1, CONV_W), lambda i: (0, 0)),
                  pl.BlockSpec((1, CONV_W), lambda i: (0, 0))],
        out_specs=pl.BlockSpec((CONV_TILE, CONV_W), lambda i: (i, 0)),
        out_shape=jax.ShapeDtypeStruct((n, CONV_W), F32),
        compiler_params=_cparams(1),
        name=f"conv_seg{seg}",
    )(u, dw, db, ln_g, ln_b)


def _prep_kernel(rkv_ref, lora_ref, ones_ref, w0_ref, wup_ref, a0_ref, aup_ref, gup_ref, kk_ref, ka_ref, rk_ref,
                 kap_o, w0_o, w1_o, b0_o, b1_o, kd0_o, kd1_o, g_o, bonus_o):
    r = rkv_ref[:, 0:RWKV_W]
    k = rkv_ref[:, RWKV_W:2 * RWKV_W]
    v = rkv_ref[:, 2 * RWKV_W:3 * RWKV_W]
    xw = jnp.tanh(lora_ref[:, 0:2 * LORA]).astype(BF16)
    xa = lora_ref[:, 2 * LORA:4 * LORA].astype(BF16)
    xg = _sigmoid(lora_ref[:, 4 * LORA:4 * LORA + GATE_LORA]).astype(BF16)
    ones = ones_ref[...]

    kk0 = k * kk_ref[...]
    ss = _split_dot(kk0 * kk0, ones)
    kap = kk0 * lax.rsqrt(jnp.maximum(ss, 1e-24))
    kap_o[...] = kap
    g_o[...] = jnp.dot(xg, gup_ref[...], preferred_element_type=F32)

    kd_sum = jnp.zeros_like(k)
    for d, (w_o, b_o, kd_o) in enumerate(((w0_o, b0_o, kd0_o), (w1_o, b1_o, kd1_o))):
        wl = w0_ref[d:d + 1, :] + jnp.dot(xw, wup_ref[d], preferred_element_type=F32)
        w_o[...] = -jnp.exp(-0.5) * _sigmoid(wl)
        a = _sigmoid(a0_ref[d:d + 1, :] + jnp.dot(xa, aup_ref[d], preferred_element_type=F32))
        b_o[...] = kap * a
        kd = k * (1.0 + (a - 1.0) * ka_ref[...])
        kd_o[...] = kd
        kd_sum = kd_sum + kd
    bonus_o[...] = _split_dot(r * kd_sum * rk_ref[...], ones) * v


def _rwkv_prep(rkv, lora, ones_blk, w0, w_up, a0, a_up, g_up, k_k, k_a, r_k):
    n = rkv.shape[0]
    row = lambda c: pl.BlockSpec((ROW_TILE, c), lambda i: (i, 0))
    full = lambda a: pl.BlockSpec(a.shape, lambda i: (0,) * a.ndim)
    args = (rkv, lora, ones_blk, w0, w_up, a0, a_up, g_up, k_k, k_a, r_k)
    return pl.pallas_call(
        _prep_kernel,
        grid=(n // ROW_TILE,),
        in_specs=[row(3 * RWKV_W), row(lora.shape[1])] + [full(a) for a in args[2:]],
        out_specs=[row(RWKV_W)] * 9,
        out_shape=[jax.ShapeDtypeStruct((n, RWKV_W), F32)] * 9,
        compiler_params=_cparams(1),
        name="rwkv_prep",
    )(*args)


def _pieces(x, passes):
    hi = x.astype(BF16)
    return (hi,) if passes == 1 else (hi, (x - hi.astype(F32)).astype(BF16))


_NN = ((1,), (0,))
_NT = ((1,), (1,))
_TN = ((0,), (0,))


def _mm(a, b, dims):
    dg = lambda p, q: lax.dot_general(p, q, (dims, ((), ())), preferred_element_type=F32)
    if len(a) == 1:
        out = dg(a[0], b[0])
    else:
        free = 1 - dims[0][0]
        both = dg(jnp.concatenate(a, axis=free), b[0])
        m = a[0].shape[free]
        out = both[:m] + both[m:]
    if len(b) > 1:
        out = out + dg(a[0], b[1])
    return out


def _block_diag(pieces, head_masks):
    return tuple(jnp.concatenate([p * m for m in head_masks], axis=0) for p in pieces)


def _chunk_masks(reverse):
    row = lax.broadcasted_iota(jnp.int32, (CHUNK, GROUP_W), 0)
    col = lax.broadcasted_iota(jnp.int32, (CHUNK, GROUP_W), 1) % CHUNK
    strict = (col > row) if reverse else (col < row)
    incl = (col >= row) if reverse else (col <= row)
    eye_side = jnp.where(col == row, 1.0, 0.0)
    tri = jnp.where(incl[:, :CHUNK], 1.0, 0.0).astype(BF16)
    return strict, incl, eye_side, tri, (0 if reverse else CHUNK - 1)


def _chunk_steps(chains):
    ln = CHUNK
    each = lambda f, *cols: [f(*args) for args in zip(*cols)]
    r, kap, v, lw, b, kd, n, masks = (list(col) for col in zip(*chains))
    strict = [m[0] for m in masks]
    incl = [m[1] for m in masks]
    lane_blk = lax.broadcasted_iota(jnp.int32, (ln, GROUP_W), 1) // HEAD
    head_masks = [jnp.where(lane_blk == h, 1.0, 0.0).astype(BF16) for h in range(HEADS_PER_GROUP)]
    bd = lambda x, passes: _block_diag(_pieces(x, passes), head_masks)
    r2 = lax.broadcasted_iota(jnp.int32, (GROUP_W, GROUP_W), 0) // HEAD
    c2 = lax.broadcasted_iota(jnp.int32, (GROUP_W, GROUP_W), 1) // HEAD
    same_head = r2 == c2

    def running_sum(lw_c, m):
        return _mm((m[3],), _pieces(lw_c, 3), _NN)

    cw = each(running_sum, lw, masks)
    tot = each(lambda cw_c, m: cw_c[m[4]:m[4] + 1, :], cw, masks)
    e_neg = each(lambda cw_c: jnp.exp(-cw_c), cw)
    kr = each(lambda kap_c, r_c, cw_c, lw_c: _pieces(
        jnp.concatenate([kap_c * jnp.exp(cw_c - lw_c), r_c * jnp.exp(cw_c)], axis=0), PASSES_A),
        kap, r, cw, lw)
    ab = each(lambda kr_c, b_c, e: _mm(kr_c, bd(b_c * e, PASSES_A), _NT), kr, b, e_neg)
    ak = each(lambda kr_c, kd_c, e: _mm(kr_c, bd(kd_c * e, PASSES_A), _NT), kr, kd, e_neg)

    q = each(lambda ab_c, s: jnp.where(s, -ab_c[:ln], 0.0), ab, strict)
    x = each(lambda q_c, m: m[2] + q_c, q, masks)
    q = each(lambda q_c: _mm(_pieces(q_c, PASSES_INV), bd(q_c, PASSES_INV), _NN), q)
    n_rounds = CHUNK.bit_length() - 2
    for j in range(1, n_rounds + 1):
        last_round = j == n_rounds
        lhs = x if last_round else each(lambda q_c, x_c: jnp.concatenate([q_c, x_c], axis=0), q, x)
        prod = each(lambda l_c, q_c: _mm(_pieces(l_c, PASSES_INV), bd(q_c, PASSES_INV), _NN), lhs, q)
        if last_round:
            x = each(lambda x_c, p_c: x_c + p_c, x, prod)
        else:
            q = each(lambda p_c: p_c[:ln], prod)
            x = each(lambda x_c, p_c: x_c + p_c[ln:], x, prod)

    a3 = each(lambda ab_c, i: jnp.where(i, ab_c[ln:], 0.0), ab, incl)
    a24 = each(lambda ak_c, s, i: jnp.concatenate([jnp.where(s, ak_c[:ln], 0.0), jnp.where(i, ak_c[ln:], 0.0)], axis=0),
               ak, strict, incl)
    av = each(lambda a_c, v_c: _mm(_pieces(a_c, PASSES_A), bd(v_c, PASSES_A), _NN), a24, v)
    krn = each(lambda kr_c, n_c: _mm(kr_c[:PASSES_STATE], _pieces(n_c, PASSES_STATE), _NT), kr, n)
    u = each(lambda x_c, krn_c, av_c: -_mm(_pieces(x_c, PASSES_INV), bd(krn_c[:ln] + av_c[:ln], PASSES_INV), _NN),
             x, krn, av)
    y = each(lambda krn_c, a3_c, u_c, av_c: krn_c[ln:] + _mm(_pieces(a3_c, PASSES_STATE), bd(u_c, PASSES_STATE), _NN)
             + av_c[ln:], krn, a3, u, av)
    e_rest = each(lambda tot_c, cw_c: jnp.exp(tot_c - cw_c), tot, cw)
    upd = each(lambda u_c, v_c, b_c, kd_c, e: _mm(_pieces(jnp.concatenate([u_c, v_c], axis=0), PASSES_A),
                                                   _pieces(jnp.concatenate([b_c * e, kd_c * e], axis=0), PASSES_A), _TN),
               u, v, b, kd, e_rest)
    n_new = each(lambda n_c, tot_c, upd_c: n_c * jnp.exp(tot_c) + jnp.where(same_head, upd_c, 0.0), n, tot, upd)
    return list(zip(y, n_new))


def _chunk_kernel(*refs, has_init, has_final):
    n_tok = 12 * SCAN_BATCHES
    tok_refs = [refs[k * 12:(k + 1) * 12] for k in range(SCAN_BATCHES)]
    refs = refs[n_tok:]
    s0 = refs[0] if has_init else None
    refs = refs[1:] if has_init else refs
    y_refs = refs[:2]
    sfin = refs[2] if has_final else None
    n_scr = refs[-1]
    c = pl.program_id(1)
    head_blocks = [(k, d, g, h) for k in range(SCAN_BATCHES) for d in range(2)
                   for g in range(N_HEADS // HEADS_PER_GROUP) for h in range(HEADS_PER_GROUP)]
    diag = lambda h: slice(h * HEAD, (h + 1) * HEAD)

    @pl.when(c == 0)
    def _():
        n_scr[...] = jnp.zeros(n_scr.shape, F32)
        if has_init:
            for k, d, g, h in head_blocks:
                n_scr[d, k, g, diag(h), diag(h)] = s0[k, d, g * HEADS_PER_GROUP + h]

    slots = [(k, d, g) for k in range(SCAN_BATCHES) for d in range(2) for g in range(N_HEADS // HEADS_PER_GROUP)]
    lanes = lambda g: slice(g * GROUP_W, (g + 1) * GROUP_W)
    masks = [_chunk_masks(reverse=False), _chunk_masks(reverse=True)]

    def operands(k, d, g):
        rf, kf, vf, rb, kb, vb, lwf, bf, kdf, lwb, bb, kdb = tok_refs[k]
        group = (rb, kb, vb, lwb, bb, kdb) if d else (rf, kf, vf, lwf, bf, kdf)
        return tuple(ref[:, lanes(g)] for ref in group)

    chains = [operands(k, d, g) + (n_scr[d, k, g], masks[d]) for k, d, g in slots]
    for (k, d, g), (y, n_new) in zip(slots, _chunk_steps(chains)):
        y_refs[d][k, :, lanes(g)] = y
        n_scr[d, k, g] = n_new

    if has_final:
        @pl.when(c == pl.num_programs(1) - 1)
        def _():
            for k, d, g, h in head_blocks:
                sfin[k, d, g * HEADS_PER_GROUP + h] = n_scr[d, k, g, diag(h), diag(h)]


def _chunk_scan(rkv, kap, dir_ops, s0, *, row0, n_batch, t_len, want_final):
    n_chunks = t_len // CHUNK
    blk0 = row0 // CHUNK
    n_grp = N_HEADS // HEADS_PER_GROUP
    nb = SCAN_BATCHES

    def tok_spec(k, reverse, col=0):
        def index(bi, ci):
            return (blk0 + (bi * nb + k) * n_chunks + (n_chunks - 1 - ci if reverse else ci), col)
        return pl.BlockSpec((CHUNK, RWKV_W), index)

    in_specs, args = [], []
    for k in range(nb):
        for rev in (False, True):
            in_specs += [tok_spec(k, rev, 0), tok_spec(k, rev), tok_spec(k, rev, 2)]
        in_specs += [tok_spec(k, False)] * 3 + [tok_spec(k, True)] * 3
        args += [rkv, kap, rkv, rkv, kap, rkv, *dir_ops[0], *dir_ops[1]]
    state = pl.BlockSpec((nb, 2, N_HEADS, HEAD, HEAD), lambda bi, ci: (bi, 0, 0, 0, 0))
    has_init = s0 is not None
    if has_init:
        in_specs.append(state)
        args.append(s0)
    y_shape = jax.ShapeDtypeStruct((n_batch, t_len, RWKV_W), F32)
    out_specs = [pl.BlockSpec((nb, CHUNK, RWKV_W), lambda bi, ci: (bi, ci, 0)),
                 pl.BlockSpec((nb, CHUNK, RWKV_W), lambda bi, ci: (bi, n_chunks - 1 - ci, 0))]
    out_shape = [y_shape, y_shape]
    if want_final:
        out_specs.append(state)
        out_shape.append(jax.ShapeDtypeStruct((n_batch, 2, N_HEADS, HEAD, HEAD), F32))
    outs = pl.pallas_call(
        functools.partial(_chunk_kernel, has_init=has_init, has_final=want_final),
        grid=(n_batch // nb, n_chunks),
        in_specs=in_specs,
        out_specs=out_specs,
        out_shape=out_shape,
        scratch_shapes=[pltpu.VMEM((2, nb, n_grp, GROUP_W, GROUP_W), F32)],
        compiler_params=_cparams(2),
        name=f"wkv_chunk_t{t_len}",
    )(*args)
    return [outs[0].reshape(n_batch * t_len, RWKV_W), outs[1].reshape(n_batch * t_len, RWKV_W)] + list(outs[2:])


def _postmix_kernel(yfp_ref, yfs_ref, ybp_ref, ybs_ref, convp_ref, convs_ref, xp_ref, xs_ref, bonus_ref, g_ref,
                    g1_ref, sh2_ref, sc2_ref, gng_ref, gnb_ref, ones_ref, wout_ref, n2g_ref, rhi_ref, rlo_ref,
                    x1_o, h2_o, logit_o):
    n_prompt = N_TOK // ROW_TILE
    ones = ones_ref[...]
    y = _pick(yfp_ref, yfs_ref, n_prompt) + _pick(ybp_ref, ybs_ref, n_prompt)
    mu = _split_dot(y, ones) * (1.0 / HEAD)
    d = y - mu
    var = _split_dot(d * d, ones) * (1.0 / HEAD)
    yn = d * lax.rsqrt(var + GN_EPS) * gng_ref[...] + gnb_ref[...]
    rw = ((yn + bonus_ref[...]) * g_ref[...]).astype(BF16)
    conv = _pick(convp_ref, convs_ref, n_prompt).astype(BF16)
    mix = (jnp.dot(conv, wout_ref[0:CONV_W, :], preferred_element_type=F32)
           + jnp.dot(rw, wout_ref[CONV_W:CONV_W + RWKV_W, :], preferred_element_type=F32))
    x1 = _pick(xp_ref, xs_ref, n_prompt) + g1_ref[0] * mix
    x1_o[...] = x1
    ms = jnp.mean(x1 * x1, axis=-1, keepdims=True)
    h2 = x1 * lax.rsqrt(ms + RMS_EPS) * n2g_ref[...]
    h2 = h2 * (1.0 + sc2_ref[0]) + sh2_ref[0]
    for c in range(TOK_ROWS):
        h2_o[pl.ds(c, ROW_TILE, stride=TOK_ROWS), :] = h2[:, c * 128:(c + 1) * 128]
    hi = h2.astype(BF16)
    lo = (h2 - hi.astype(F32)).astype(BF16)
    logit_o[...] = (jnp.dot(hi, rhi_ref[...], preferred_element_type=F32)
                    + jnp.dot(lo, rhi_ref[...], preferred_element_type=F32)
                    + jnp.dot(hi, rlo_ref[...], preferred_element_type=F32))


def _postmix(yf, yb, conv, x, bonus, g, mod3, gn_g, gn_b, ones_blk, w_out_bf16, norm2_g, r_hi, r_lo):
    n = 2 * N_TOK
    row = lambda c: pl.BlockSpec((ROW_TILE, c), lambda i: (i, 0))
    pair = lambda c: [pl.BlockSpec((ROW_TILE, c), m) for m in _group_tile(N_TOK // ROW_TILE)]
    full = lambda a: pl.BlockSpec(a.shape, lambda i: (0,) * a.ndim)
    modspec = lambda j: pl.BlockSpec((1, 1, D_MODEL), lambda i: (_cond_index(i), 0, j))
    return pl.pallas_call(
        _postmix_kernel,
        grid=(n // ROW_TILE,),
        in_specs=pair(RWKV_W) + pair(RWKV_W) + pair(CONV_W) + pair(D_MODEL) + [
            row(RWKV_W), row(RWKV_W), modspec(2), modspec(3), modspec(4), full(gn_g), full(gn_b), full(ones_blk),
            full(w_out_bf16), full(norm2_g), full(r_hi), full(r_lo)],
        out_specs=[row(D_MODEL), pl.BlockSpec((ROW_TILE * TOK_ROWS, 128), lambda i: (i, 0)), row(128)],
        out_shape=[jax.ShapeDtypeStruct((n, D_MODEL), F32), jax.ShapeDtypeStruct((n * TOK_ROWS, 128), F32),
                   jax.ShapeDtypeStruct((n, 128), F32)],
        compiler_params=_cparams(1),
        name="postmix",
    )(*yf, *yb, *conv, *x, bonus, g, mod3, mod3, mod3, gn_g, gn_b, ones_blk, w_out_bf16, norm2_g, r_hi, r_lo)


def _prefix_lanes(m, utri):
    rows = m.shape[0]
    off = jnp.zeros((rows, 1), F32)
    out = []
    for c in range(N_TOK // 128):
        blk = m[:, c * 128:(c + 1) * 128]
        out.append(jnp.dot(blk.astype(BF16), utri, preferred_element_type=F32) + off)
        off = off + jnp.sum(blk, axis=1, keepdims=True)
    return jnp.concatenate(out, axis=1)


def _select_kernel(lt_ref, utri_ref, pos_o, aff_o):
    utri = utri_ref[...]
    affs = []
    for grp in range(2):
        x = lt_ref[grp * N_EXPERTS:(grp + 1) * N_EXPERTS, :]
        e = jnp.exp(x - jnp.max(x, axis=0, keepdims=True))
        affs.append(e / jnp.sum(e, axis=0, keepdims=True))
    aff = jnp.concatenate(affs, axis=0)
    aff_o[...] = aff
    n_rows = 2 * N_EXPERTS

    def body(_, lh):
        lo, hi = lh
        m2 = 0.5 * (lo + hi)
        m1 = 0.5 * (lo + m2)
        m3 = 0.5 * (m2 + hi)
        ge1, ge2, ge3 = (jnp.sum(jnp.where(aff > m, 1.0, 0.0), axis=1, keepdims=True) >= CAP for m in (m1, m2, m3))
        lo = jnp.where(ge3, m3, jnp.where(ge2, m2, jnp.where(ge1, m1, lo)))
        hi = jnp.where(ge3, hi, jnp.where(ge2, m3, jnp.where(ge1, m2, m1)))
        return lo, hi

    lo, hi = lax.fori_loop(0, BISECT_ITERS // 2, body,
                           (jnp.full((n_rows, 1), -1.0, F32), jnp.full((n_rows, 1), 1.0, F32)))
    gt = jnp.where(aff > hi, 1.0, 0.0)
    tie = jnp.where(aff > lo, 1.0, 0.0) - gt
    need = CAP - jnp.sum(gt, axis=1, keepdims=True)
    sel = gt + tie * jnp.where(_prefix_lanes(tie, utri) < need, 1.0, 0.0)
    pos = _prefix_lanes(sel, utri)
    pos_o[...] = jnp.where(sel > 0.5, pos, -1.0)


def _select(logits_t, utri):
    shp = jax.ShapeDtypeStruct((2 * N_EXPERTS, N_TOK), F32)
    full = lambda a: pl.BlockSpec(a.shape, lambda i: (0,) * a.ndim)
    return pl.pallas_call(
        _select_kernel,
        grid=(1,),
        in_specs=[full(logits_t), full(utri)],
        out_specs=[pl.BlockSpec(shp.shape, lambda i: (0, 0))] * 2,
        out_shape=[shp, shp],
        compiler_params=_cparams(1),
        name="ec_select",
    )(logits_t, utri)


def _compact_kernel(pos_ref, aff_ref, idx_o, gate_o):
    slot = lax.broadcasted_iota(jnp.int32, (CAP, 1), 0).astype(F32)
    idx = jnp.zeros((CAP, 128), F32)
    gate = jnp.zeros((CAP, 128), F32)
    for c in range(N_TOK // 128):
        p = pos_ref[0, :, c * 128:(c + 1) * 128]
        a = aff_ref[0, :, c * 128:(c + 1) * 128]
        tok = (lax.broadcasted_iota(jnp.int32, (1, 128), 1) + c * 128).astype(F32)
        hit = p == slot
        idx = idx + jnp.where(hit, tok, 0.0)
        gate = gate + jnp.where(hit, a, 0.0)
    idx_o[0] = jnp.broadcast_to(jnp.sum(idx, axis=1, keepdims=True), (CAP, 128))
    gate_o[0] = jnp.broadcast_to(jnp.sum(gate, axis=1, keepdims=True), (CAP, 128))


def _compact(pos3, aff3):
    rows = pos3.shape[0]
    shp = jax.ShapeDtypeStruct((rows, CAP, 128), F32)
    return pl.pallas_call(
        _compact_kernel,
        grid=(rows,),
        in_specs=[pl.BlockSpec((1, 1, N_TOK), lambda i: (i, 0, 0))] * 2,
        out_specs=[pl.BlockSpec((1, CAP, 128), lambda i: (i, 0, 0))] * 2,
        out_shape=[shp, shp],
        compiler_params=_cparams(1),
        name="ec_compact",
    )(pos3, aff3)


def _gather_kernel(idx_ref, h_ref, o_ref, tile):
    grp = pl.program_id(0)
    eb = pl.program_id(1)
    for ei in range(ROUTE_EXPERTS):
        base = (grp * N_EXPERTS + eb * ROUTE_EXPERTS + ei) * CAP

        def tok(j8, c, base=base):
            for u in range(8):
                j = j8 * 8 + u
                row = pl.multiple_of(idx_ref[base + j] * 8, 8)
                tile[pl.ds(j, 8, stride=TOK_STRIDE), :] = h_ref[0, pl.ds(row, 8), :]
            return c

        lax.fori_loop(0, CAP // 8, tok, 0)
        for c in range(TOK_ROWS):
            o_ref[ei, :, c * 128:(c + 1) * 128] = tile[c * TOK_STRIDE:c * TOK_STRIDE + CAP, :].astype(BF16)


def _gather(idx_flat, h_rows):
    return pl.pallas_call(
        _gather_kernel,
        grid_spec=pltpu.PrefetchScalarGridSpec(
            num_scalar_prefetch=1,
            grid=(2, N_EXPERTS // ROUTE_EXPERTS),
            in_specs=[pl.BlockSpec((1, N_TOK * 8, 128), lambda g, e, idx: (g, 0, 0))],
            out_specs=pl.BlockSpec((ROUTE_EXPERTS, CAP, D_MODEL), lambda g, e, idx: (e, g, 0)),
            scratch_shapes=[pltpu.VMEM((8 * TOK_STRIDE, 128), F32)]),
        out_shape=jax.ShapeDtypeStruct((N_EXPERTS, 2 * CAP, D_MODEL), BF16),
        compiler_params=_cparams(2),
        name="ec_gather",
    )(idx_flat, h_rows)


def _expert_kernel(xe_ref, w1_ref, w3_ref, w2_ref, gate_ref, o_ref, hid_scr):
    s = pl.program_id(1)
    subs = [(lo, min(FF_SUB, FF_SLAB - lo)) for lo in range(0, FF_SLAB, FF_SUB)]

    @pl.when(s < FF_SPLIT)
    def _():
        x = xe_ref[0]
        for lo, width in subs:
            h1 = jnp.dot(x, w1_ref[0, :, lo:lo + width].astype(BF16), preferred_element_type=F32)
            h3 = jnp.dot(x, w3_ref[0, :, lo:lo + width].astype(BF16), preferred_element_type=F32)
            hid_scr[s, :, lo:lo + width] = (h1 * _sigmoid(h1) * h3).astype(BF16)

    @pl.when(s >= FF_SPLIT)
    def _():
        acc = None
        for lo, width in subs:
            part = jnp.dot(hid_scr[s - FF_SPLIT, :, lo:lo + width], w2_ref[0, lo:lo + width, :].astype(BF16),
                           preferred_element_type=F32)
            acc = part if acc is None else acc + part

        @pl.when(s == FF_SPLIT)
        def _():
            o_ref[0] = acc

        @pl.when(s > FF_SPLIT)
        def _():
            o_ref[0] = o_ref[0] + acc

    @pl.when(s == 2 * FF_SPLIT - 1)
    def _():
        o_ref[0] = o_ref[0] * gate_ref[0]


def _experts(xe, w1, w3, w2, gate_col):
    m = xe.shape[1]
    up = lambda e, s: (e, 0, jnp.minimum(s, FF_SPLIT - 1))
    down = lambda e, s: (e, jnp.maximum(s - FF_SPLIT, 0), 0)
    return pl.pallas_call(
        _expert_kernel,
        grid=(N_EXPERTS, 2 * FF_SPLIT),
        in_specs=[pl.BlockSpec((1, m, D_MODEL), lambda e, s: (e, 0, 0)),
                  pl.BlockSpec((1, D_MODEL, FF_SLAB), up),
                  pl.BlockSpec((1, D_MODEL, FF_SLAB), up),
                  pl.BlockSpec((1, FF_SLAB, D_MODEL), down),
                  pl.BlockSpec((1, m, 1), lambda e, s: (e, 0, 0))],
        out_specs=pl.BlockSpec((1, m, D_MODEL), lambda e, s: (e, 0, 0)),
        out_shape=jax.ShapeDtypeStruct((N_EXPERTS, m, D_MODEL), F32),
        scratch_shapes=[pltpu.VMEM((FF_SPLIT, m, FF_SLAB), BF16)],
        compiler_params=_cparams(2, vmem=EXPERT_VMEM_LIMIT),
        name="ec_experts",
    )(xe, w1, w3, w2, gate_col)


def _combine_kernel(idx_ref, ye_ref, x1_ref, g2_ref, fg_ref, op_ref, os_ref, acc, tile):
    grp = pl.program_id(0)
    s = pl.program_id(1)
    n_scatter = N_EXPERTS // ROUTE_EXPERTS

    @pl.when(s == 0)
    def _():
        acc[...] = jnp.zeros_like(acc)

    @pl.when(s < n_scatter)
    def _():
        for ei in range(ROUTE_EXPERTS):
            base = (grp * N_EXPERTS + s * ROUTE_EXPERTS + ei) * CAP
            for c in range(TOK_ROWS):
                tile[c * TOK_STRIDE:c * TOK_STRIDE + CAP, :] = ye_ref[ei, :, c * 128:(c + 1) * 128]

            def tok(j4, c, base=base):
                rows, vals = [], []
                for u in range(4):
                    j = j4 * 4 + u
                    row = pl.multiple_of(idx_ref[base + j] * 8, 8)
                    rows.append(row)
                    vals.append(acc[pl.ds(row, 8), :] + tile[pl.ds(j, 8, stride=TOK_STRIDE), :])
                for row, val in zip(rows, vals):
                    acc[pl.ds(row, 8), :] = val
                return c

            lax.fori_loop(0, CAP // 4, tok, 0)

    @pl.when(s >= n_scatter)
    def _():
        row0 = (s - n_scatter) * (ROW_TILE * TOK_ROWS)
        y = jnp.concatenate([acc[pl.ds(row0 + c, ROW_TILE, stride=TOK_ROWS), :] for c in range(TOK_ROWS)], axis=1)
        x = x1_ref[...] + g2_ref[0] * y
        ms = jnp.mean(x * x, axis=-1, keepdims=True)
        out = x * lax.rsqrt(ms + RMS_EPS) * fg_ref[...]

        @pl.when(grp == 0)
        def _():
            op_ref[...] = out

        @pl.when(grp == 1)
        def _():
            os_ref[...] = out


def _combine_final(idx_flat, ye, x1, mod3, final_g):
    n_scatter = N_EXPERTS // ROUTE_EXPERTS
    tiles = N_TOK // ROW_TILE
    tile_of = lambda g, s: g * tiles + jnp.maximum(s - n_scatter, 0)
    shp = jax.ShapeDtypeStruct((N_TOK, D_MODEL), F32)
    return pl.pallas_call(
        _combine_kernel,
        grid_spec=pltpu.PrefetchScalarGridSpec(
            num_scalar_prefetch=1,
            grid=(2, n_scatter + tiles),
            in_specs=[pl.BlockSpec((ROUTE_EXPERTS, CAP, D_MODEL), lambda g, s, idx: (jnp.minimum(s, n_scatter - 1), g, 0)),
                      pl.BlockSpec((ROW_TILE, D_MODEL), lambda g, s, idx: (tile_of(g, s), 0)),
                      pl.BlockSpec((1, 1, D_MODEL), lambda g, s, idx: (_cond_index(tile_of(g, s)), 0, 5)),
                      pl.BlockSpec((1, D_MODEL), lambda g, s, idx: (0, 0))],
            out_specs=[pl.BlockSpec((ROW_TILE, D_MODEL),
                                    lambda g, s, idx: (jnp.where(g == 0, jnp.maximum(s - n_scatter, 0), tiles - 1), 0)),
                       pl.BlockSpec((ROW_TILE, D_MODEL),
                                    lambda g, s, idx: (jnp.where(g == 1, jnp.maximum(s - n_scatter, 0), 0), 0))],
            scratch_shapes=[pltpu.VMEM((N_TOK * TOK_ROWS, 128), F32), pltpu.VMEM((TOK_ROWS * TOK_STRIDE, 128), F32)]),
        out_shape=[shp, shp],
        compiler_params=_cparams(2),
        name="ec_combine_final",
    )(idx_flat, ye, x1, mod3, final_g)


def kernel(x_prompt, x_sample, state_wkv, c, c_ctx, norm1_g, norm2_g, w_mod, b_mod, w_in, conv_dw, conv_b, conv_ln_g,
           conv_ln_b, rwkv_w0, rwkv_w_up, rwkv_a0, rwkv_a_up, rwkv_g_up, rwkv_k_k, rwkv_k_a, rwkv_r_k, rwkv_gn_g,
           rwkv_gn_b, w_out, router, exp_w1, exp_w3, exp_w2, final_norm_g):
    row2 = lambda a: a.reshape(1, -1)
    x = (x_prompt.reshape(N_TOK, D_MODEL), x_sample.reshape(N_TOK, D_MODEL))
    cond8 = jnp.concatenate([c_ctx[None, :], c, jnp.zeros((8 - 1 - DEC_BATCH, D_MODEL), F32)], axis=0)

    mod = _modulation(cond8, w_mod[0], b_mod)
    mod3 = mod.reshape(8, 1, 6 * D_MODEL)

    u, rkv, lora = _inproj(*x, mod3, norm1_g, w_in[0].astype(BF16))

    conv_args = (conv_dw[0], conv_b, conv_ln_g, conv_ln_b)
    conv = (_conv_branch(u, *conv_args, seg=SEQ, row0=0), _conv_branch(u, *conv_args, seg=GRID_W, row0=N_TOK))

    lane_head = jnp.arange(RWKV_W, dtype=jnp.int32) // HEAD
    ones_blk = (lane_head[:, None] == lane_head[None, :]).astype(BF16)

    def per_dir_rows(up):
        z = jnp.zeros_like(up[0])
        return jnp.stack([jnp.concatenate([up[0], z], axis=0), jnp.concatenate([z, up[1]], axis=0)]).astype(BF16)
    (kap, w0, w1, b0, b1, kd0, kd1, g, bonus) = _rwkv_prep(
        rkv, lora, ones_blk, rwkv_w0[0], per_dir_rows(rwkv_w_up[0]), rwkv_a0[0], per_dir_rows(rwkv_a_up[0]),
        rwkv_g_up[0].astype(BF16), rwkv_k_k, rwkv_k_a, row2(rwkv_r_k[0]))

    dir_ops = ((w0, b0, kd0), (w1, b1, kd1))
    yp_f, yp_b, new_state = _chunk_scan(rkv, kap, dir_ops, None, row0=0, n_batch=BATCH, t_len=SEQ, want_final=True)
    ys_f, ys_b = _chunk_scan(rkv, kap, dir_ops, state_wkv[:, 0], row0=N_TOK, n_batch=DEC_BATCH, t_len=DEC_SEQ,
                             want_final=False)

    r_pad = jnp.pad(router[0], ((0, 0), (0, 128 - N_EXPERTS)))
    r_hi = r_pad.astype(BF16)
    r_lo = (r_pad - r_hi.astype(F32)).astype(BF16)
    x1, h2_rows, logits = _postmix((yp_f, ys_f), (yp_b, ys_b), conv, x, bonus, g, mod3, rwkv_gn_g, rwkv_gn_b, ones_blk,
                                   w_out[0].astype(BF16), norm2_g, r_hi, r_lo)

    logits_t = logits[:, :N_EXPERTS].reshape(2, N_TOK, N_EXPERTS).transpose(0, 2, 1).reshape(2 * N_EXPERTS, N_TOK)
    lane = jnp.arange(128, dtype=jnp.int32)
    utri = (lane[:, None] < lane[None, :]).astype(BF16)
    pos, aff = _select(logits_t, utri)
    idx_c, gate_c = _compact(pos.reshape(2 * N_EXPERTS, 1, N_TOK), aff.reshape(2 * N_EXPERTS, 1, N_TOK))
    idx_flat = idx_c[:, :, 0].astype(jnp.int32).reshape(-1)
    gate_col = gate_c[:, :, 0].reshape(2, N_EXPERTS, CAP).transpose(1, 0, 2).reshape(N_EXPERTS, 2 * CAP, 1)

    xe = _gather(idx_flat, h2_rows.reshape(2, N_TOK * TOK_ROWS, 128))
    ye = _experts(xe, exp_w1[0], exp_w3[0], exp_w2[0], gate_col)
    out_p, out_s = _combine_final(idx_flat, ye, x1, mod3, row2(final_norm_g))
    return (out_p.reshape(BATCH, SEQ, D_MODEL), out_s.reshape(DEC_BATCH, DEC_SEQ, D_MODEL),
            new_state.reshape(BATCH, 1, 2, N_HEADS, HEAD, HEAD))
```

```python
import functools

import jax
import jax.numpy as jnp
from jax import lax
from jax.experimental import pallas as pl
from jax.experimental.pallas import tpu as pltpu

F32 = jnp.float32
BF16 = jnp.bfloat16

D_MODEL = 1024
BATCH = 16
SEQ = 256
DEC_BATCH = 4
DEC_SEQ = 1024
GRID_W = 64
CONV_W = 512
CONV_K = 31
CONV_PAD = 15
RWKV_W = 512
HEAD = 64
N_HEADS = 8
LORA = 64
GATE_LORA = 128
N_EXPERTS = 16
D_FF = 2816
D_IN = 2944
CAP = 512
N_TOK = 4096
RMS_EPS = 1e-6
LN_EPS = 1e-5
GN_EPS = 64e-5

ROW_TILE = 512
FF_SPLIT = 2
FF_SLAB = D_FF // FF_SPLIT
FF_SUB = 256
EXPERT_VMEM_LIMIT = 60 * 1024 * 1024
CHUNK = 64
HEADS_PER_GROUP = 4
GROUP_W = HEADS_PER_GROUP * HEAD
PASSES_A = 3
PASSES_INV = 1
PASSES_STATE = 1
SCAN_BATCHES = 4
TOK_ROWS = D_MODEL // 128
TOK_STRIDE = CAP + 8
ROUTE_EXPERTS = 4
BISECT_ITERS = 160
VMEM_LIMIT = 56 * 1024 * 1024


def _cparams(n_axes, vmem=None):
    return pltpu.CompilerParams(dimension_semantics=("arbitrary",) * n_axes,
                                vmem_limit_bytes=vmem or VMEM_LIMIT)


def _cond_index(i):
    n_prompt = (BATCH * SEQ) // ROW_TILE
    per_batch = DEC_SEQ // ROW_TILE
    return jnp.where(i < n_prompt, 0, 1 + (i - n_prompt) // per_batch)


def _split_dot(x, w_bf16):
    hi = x.astype(BF16)
    lo = (x - hi.astype(F32)).astype(BF16)
    return (jnp.dot(hi, w_bf16, preferred_element_type=F32)
            + jnp.dot(lo, w_bf16, preferred_element_type=F32))


def _sigmoid(x):
    return 1.0 / (1.0 + jnp.exp(-x))


def _mod_kernel(c_ref, w_ref, b_ref, o_ref):
    c = c_ref[...]
    s = (c * _sigmoid(c)).astype(BF16)
    o_ref[...] = jnp.dot(s, w_ref[...].astype(BF16), preferred_element_type=F32) + b_ref[...]


def _modulation(cond8, w_mod, b_mod):
    n = 6 * D_MODEL
    tn = D_MODEL
    return pl.pallas_call(
        _mod_kernel,
        grid=(n // tn,),
        in_specs=[pl.BlockSpec((8, D_MODEL), lambda j: (0, 0)),
                  pl.BlockSpec((D_MODEL, tn), lambda j: (0, j)),
                  pl.BlockSpec((1, tn), lambda j: (0, j))],
        out_specs=pl.BlockSpec((8, tn), lambda j: (0, j)),
        out_shape=jax.ShapeDtypeStruct((8, n), F32),
        compiler_params=_cparams(1),
        name="modulation",
    )(cond8, w_mod, b_mod)


def _group_tile(n_prompt_tiles):
    return (lambda i: (jnp.minimum(i, n_prompt_tiles - 1), 0)), (lambda i: (jnp.maximum(i - n_prompt_tiles, 0), 0))


def _pick(prompt_ref, sample_ref, n_prompt_tiles):
    return jnp.where(pl.program_id(0) < n_prompt_tiles, prompt_ref[...], sample_ref[...])


def _mixer_in_kernel(xp_ref, xs_ref, sh_ref, sc_ref, n1g_ref, w_ref, dw_ref, db_ref, lng_ref, lnb_ref, ones_ref,
                     w0_ref, wup_ref, a0_ref, aup_ref, gup_ref, kk_ref, ka_ref, rk_ref,
                     conv_o, rkv_o, kap_o, lw0_o, lw1_o, b0_o, b1_o, kd0_o, kd1_o, g_o, bonus_o):
    n_prompt = N_TOK // ROW_TILE
    x = _pick(xp_ref, xs_ref, n_prompt)
    ms = jnp.mean(x * x, axis=-1, keepdims=True)
    h = x * lax.rsqrt(ms + RMS_EPS) * n1g_ref[...]
    h = (h * (1.0 + sc_ref[0]) + sh_ref[0]).astype(BF16)
    uv = jnp.dot(h, w_ref[:, 0:CONV_W], preferred_element_type=F32)
    ug = jnp.dot(h, w_ref[:, CONV_W:2 * CONV_W], preferred_element_type=F32)
    seg = jnp.where(pl.program_id(0) < n_prompt, SEQ, GRID_W)
    conv_o[...] = _conv_ln_silu(uv * _sigmoid(ug), seg, dw_ref, db_ref, lng_ref, lnb_ref)
    rkv = jnp.dot(h, w_ref[:, 2 * CONV_W:2 * CONV_W + 3 * RWKV_W], preferred_element_type=F32)
    rkv_o[...] = rkv
    lora = jnp.dot(h, w_ref[:, 2 * CONV_W + 3 * RWKV_W:D_IN], preferred_element_type=F32)
    _rwkv_operands(rkv, lora, ones_ref, w0_ref, wup_ref, a0_ref, aup_ref, gup_ref, kk_ref, ka_ref, rk_ref,
                   kap_o, (lw0_o, lw1_o), (b0_o, b1_o), (kd0_o, kd1_o), g_o, bonus_o)


def _mixer_in(x_prompt, x_sample, mod3, norm1_g, w_in_bf16, conv_params, rwkv_params):
    n = 2 * N_TOK
    first, second = _group_tile(N_TOK // ROW_TILE)
    row = lambda c: pl.BlockSpec((ROW_TILE, c), lambda i: (i, 0))
    full = lambda a: pl.BlockSpec(a.shape, lambda i: (0,) * a.ndim)
    consts = (norm1_g, w_in_bf16) + tuple(conv_params) + tuple(rwkv_params)
    widths = [CONV_W, 3 * RWKV_W] + [RWKV_W] * 9
    return pl.pallas_call(
        _mixer_in_kernel,
        grid=(n // ROW_TILE,),
        in_specs=[pl.BlockSpec((ROW_TILE, D_MODEL), first),
                  pl.BlockSpec((ROW_TILE, D_MODEL), second),
                  pl.BlockSpec((1, 1, D_MODEL), lambda i: (_cond_index(i), 0, 0)),
                  pl.BlockSpec((1, 1, D_MODEL), lambda i: (_cond_index(i), 0, 1)),
                  ] + [full(a) for a in consts],
        out_specs=[row(c) for c in widths],
        out_shape=[jax.ShapeDtypeStruct((n, c), F32) for c in widths],
        compiler_params=_cparams(1),
        name="mixer_in",
    )(x_prompt, x_sample, mod3, mod3, *consts)


def _conv_ln_silu(u, seg, dw_ref, db_ref, g_ref, b_ref):
    rows = u.shape[0]
    pos = lax.broadcasted_iota(jnp.int32, (rows, 1), 0) & (seg - 1)
    acc = jnp.zeros_like(u)
    for k in range(CONV_K):
        s = k - CONV_PAD
        shifted = u if s == 0 else pltpu.roll(u, (-s) % rows, 0)
        valid = (pos + s >= 0) & (pos + s < seg)
        acc = acc + jnp.where(valid, shifted, 0.0) * dw_ref[k:k + 1, :]
    out = acc + db_ref[...]
    mu = jnp.mean(out, axis=-1, keepdims=True)
    d = out - mu
    var = jnp.mean(d * d, axis=-1, keepdims=True)
    out = d * lax.rsqrt(var + LN_EPS) * g_ref[...] + b_ref[...]
    return out * _sigmoid(out)


def _rwkv_operands(rkv, lora, ones_ref, w0_ref, wup_ref, a0_ref, aup_ref, gup_ref, kk_ref, ka_ref, rk_ref,
                   kap_o, lw_o, b_o, kd_o, g_o, bonus_o):
    r = rkv[:, 0:RWKV_W]
    k = rkv[:, RWKV_W:2 * RWKV_W]
    v = rkv[:, 2 * RWKV_W:3 * RWKV_W]
    xw = jnp.tanh(lora[:, 0:2 * LORA]).astype(BF16)
    xa = lora[:, 2 * LORA:4 * LORA].astype(BF16)
    xg = _sigmoid(lora[:, 4 * LORA:4 * LORA + GATE_LORA]).astype(BF16)
    ones = ones_ref[...]

    kk0 = k * kk_ref[...]
    ss = _split_dot(kk0 * kk0, ones)
    kap = kk0 * lax.rsqrt(jnp.maximum(ss, 1e-24))
    kap_o[...] = kap
    g_o[...] = jnp.dot(xg, gup_ref[...], preferred_element_type=F32)

    kd_sum = jnp.zeros_like(k)
    for d in range(2):
        wl = w0_ref[d:d + 1, :] + jnp.dot(xw, wup_ref[d], preferred_element_type=F32)
        lw_o[d][...] = -jnp.exp(-0.5) * _sigmoid(wl)
        a = _sigmoid(a0_ref[d:d + 1, :] + jnp.dot(xa, aup_ref[d], preferred_element_type=F32))
        b_o[d][...] = kap * a
        kd = k * (1.0 + (a - 1.0) * ka_ref[...])
        kd_o[d][...] = kd
        kd_sum = kd_sum + kd
    bonus_o[...] = _split_dot(r * kd_sum * rk_ref[...], ones) * v


def _pieces(x, passes):
    hi = x.astype(BF16)
    return (hi,) if passes == 1 else (hi, (x - hi.astype(F32)).astype(BF16))


_NN = ((1,), (0,))
_NT = ((1,), (1,))
_TN = ((0,), (0,))


def _mm(a, b, dims):
    dg = lambda p, q: lax.dot_general(p, q, (dims, ((), ())), preferred_element_type=F32)
    if len(a) == 1:
        out = dg(a[0], b[0])
    else:
        free = 1 - dims[0][0]
        both = dg(jnp.concatenate(a, axis=free), b[0])
        m = a[0].shape[free]
        out = both[:m] + both[m:]
    if len(b) > 1:
        out = out + dg(a[0], b[1])
    return out


def _block_diag(pieces, head_masks):
    return tuple(jnp.concatenate([p * m for m in head_masks], axis=0) for p in pieces)


def _chunk_masks(reverse):
    row = lax.broadcasted_iota(jnp.int32, (CHUNK, GROUP_W), 0)
    col = lax.broadcasted_iota(jnp.int32, (CHUNK, GROUP_W), 1) % CHUNK
    strict = (col > row) if reverse else (col < row)
    incl = (col >= row) if reverse else (col <= row)
    eye_side = jnp.where(col == row, 1.0, 0.0)
    tri = jnp.where(incl[:, :CHUNK], 1.0, 0.0).astype(BF16)
    return strict, incl, eye_side, tri, (0 if reverse else CHUNK - 1)


def _chunk_steps(chains):
    ln = CHUNK
    each = lambda f, *cols: [f(*args) for args in zip(*cols)]
    r, kap, v, lw, b, kd, n, masks = (list(col) for col in zip(*chains))
    strict = [m[0] for m in masks]
    incl = [m[1] for m in masks]
    lane_blk = lax.broadcasted_iota(jnp.int32, (ln, GROUP_W), 1) // HEAD
    head_masks = [jnp.where(lane_blk == h, 1.0, 0.0).astype(BF16) for h in range(HEADS_PER_GROUP)]
    bd = lambda x, passes: _block_diag(_pieces(x, passes), head_masks)
    r2 = lax.broadcasted_iota(jnp.int32, (GROUP_W, GROUP_W), 0) // HEAD
    c2 = lax.broadcasted_iota(jnp.int32, (GROUP_W, GROUP_W), 1) // HEAD
    same_head = r2 == c2

    def running_sum(lw_c, m):
        return _mm((m[3],), _pieces(lw_c, 3), _NN)

    cw = each(running_sum, lw, masks)
    tot = each(lambda cw_c, m: cw_c[m[4]:m[4] + 1, :], cw, masks)
    e_neg = each(lambda cw_c: jnp.exp(-cw_c), cw)
    kr = each(lambda kap_c, r_c, cw_c, lw_c: _pieces(
        jnp.concatenate([kap_c * jnp.exp(cw_c - lw_c), r_c * jnp.exp(cw_c)], axis=0), PASSES_A),
        kap, r, cw, lw)
    ab = each(lambda kr_c, b_c, e: _mm(kr_c, bd(b_c * e, PASSES_A), _NT), kr, b, e_neg)
    ak = each(lambda kr_c, kd_c, e: _mm(kr_c, bd(kd_c * e, PASSES_A), _NT), kr, kd, e_neg)

    q = each(lambda ab_c, s: jnp.where(s, -ab_c[:ln], 0.0), ab, strict)
    x = each(lambda q_c, m: m[2] + q_c, q, masks)
    q = each(lambda q_c: _mm(_pieces(q_c, PASSES_INV), bd(q_c, PASSES_INV), _NN), q)
    n_rounds = CHUNK.bit_length() - 2
    for j in range(1, n_rounds + 1):
        last_round = j == n_rounds
        lhs = x if last_round else each(lambda q_c, x_c: jnp.concatenate([q_c, x_c], axis=0), q, x)
        prod = each(lambda l_c, q_c: _mm(_pieces(l_c, PASSES_INV), bd(q_c, PASSES_INV), _NN), lhs, q)
        if last_round:
            x = each(lambda x_c, p_c: x_c + p_c, x, prod)
        else:
            q = each(lambda p_c: p_c[:ln], prod)
            x = each(lambda x_c, p_c: x_c + p_c[ln:], x, prod)

    a3 = each(lambda ab_c, i: jnp.where(i, ab_c[ln:], 0.0), ab, incl)
    a24 = each(lambda ak_c, s, i: jnp.concatenate([jnp.where(s, ak_c[:ln], 0.0), jnp.where(i, ak_c[ln:], 0.0)], axis=0),
               ak, strict, incl)
    av = each(lambda a_c, v_c: _mm(_pieces(a_c, PASSES_A), bd(v_c, PASSES_A), _NN), a24, v)
    krn = each(lambda kr_c, n_c: _mm(kr_c[:PASSES_STATE], _pieces(n_c, PASSES_STATE), _NT), kr, n)
    u = each(lambda x_c, krn_c, av_c: -_mm(_pieces(x_c, PASSES_INV), bd(krn_c[:ln] + av_c[:ln], PASSES_INV), _NN),
             x, krn, av)
    y = each(lambda krn_c, a3_c, u_c, av_c: krn_c[ln:] + _mm(_pieces(a3_c, PASSES_STATE), bd(u_c, PASSES_STATE), _NN)
             + av_c[ln:], krn, a3, u, av)
    e_rest = each(lambda tot_c, cw_c: jnp.exp(tot_c - cw_c), tot, cw)
    upd = each(lambda u_c, v_c, b_c, kd_c, e: _mm(_pieces(jnp.concatenate([u_c, v_c], axis=0), PASSES_A),
                                                   _pieces(jnp.concatenate([b_c * e, kd_c * e], axis=0), PASSES_A), _TN),
               u, v, b, kd, e_rest)
    n_new = each(lambda n_c, tot_c, upd_c: n_c * jnp.exp(tot_c) + jnp.where(same_head, upd_c, 0.0), n, tot, upd)
    return list(zip(y, n_new))


def _chunk_kernel(*refs, has_init, has_final):
    n_tok = 12 * SCAN_BATCHES
    tok_refs = [refs[k * 12:(k + 1) * 12] for k in range(SCAN_BATCHES)]
    refs = refs[n_tok:]
    s0 = refs[0] if has_init else None
    refs = refs[1:] if has_init else refs
    y_refs = refs[:2]
    sfin = refs[2] if has_final else None
    n_scr = refs[-1]
    c = pl.program_id(1)
    head_blocks = [(k, d, g, h) for k in range(SCAN_BATCHES) for d in range(2)
                   for g in range(N_HEADS // HEADS_PER_GROUP) for h in range(HEADS_PER_GROUP)]
    diag = lambda h: slice(h * HEAD, (h + 1) * HEAD)

    @pl.when(c == 0)
    def _():
        n_scr[...] = jnp.zeros(n_scr.shape, F32)
        if has_init:
            for k, d, g, h in head_blocks:
                n_scr[d, k, g, diag(h), diag(h)] = s0[k, d, g * HEADS_PER_GROUP + h]

    slots = [(k, d, g) for k in range(SCAN_BATCHES) for d in range(2) for g in range(N_HEADS // HEADS_PER_GROUP)]
    lanes = lambda g: slice(g * GROUP_W, (g + 1) * GROUP_W)
    masks = [_chunk_masks(reverse=False), _chunk_masks(reverse=True)]

    def operands(k, d, g):
        rf, kf, vf, rb, kb, vb, lwf, bf, kdf, lwb, bb, kdb = tok_refs[k]
        group = (rb, kb, vb, lwb, bb, kdb) if d else (rf, kf, vf, lwf, bf, kdf)
        return tuple(ref[:, lanes(g)] for ref in group)

    chains = [operands(k, d, g) + (n_scr[d, k, g], masks[d]) for k, d, g in slots]
    for (k, d, g), (y, n_new) in zip(slots, _chunk_steps(chains)):
        y_refs[d][k, :, lanes(g)] = y
        n_scr[d, k, g] = n_new

    if has_final:
        @pl.when(c == pl.num_programs(1) - 1)
        def _():
            for k, d, g, h in head_blocks:
                sfin[k, d, g * HEADS_PER_GROUP + h] = n_scr[d, k, g, diag(h), diag(h)]


def _chunk_scan(rkv, kap, dir_ops, s0, *, row0, n_batch, t_len, want_final):
    n_chunks = t_len // CHUNK
    blk0 = row0 // CHUNK
    n_grp = N_HEADS // HEADS_PER_GROUP
    nb = SCAN_BATCHES

    def tok_spec(k, reverse, col=0):
        def index(bi, ci):
            return (blk0 + (bi * nb + k) * n_chunks + (n_chunks - 1 - ci if reverse else ci), col)
        return pl.BlockSpec((CHUNK, RWKV_W), index)

    in_specs, args = [], []
    for k in range(nb):
        for rev in (False, True):
            in_specs += [tok_spec(k, rev, 0), tok_spec(k, rev), tok_spec(k, rev, 2)]
        in_specs += [tok_spec(k, False)] * 3 + [tok_spec(k, True)] * 3
        args += [rkv, kap, rkv, rkv, kap, rkv, *dir_ops[0], *dir_ops[1]]
    state = pl.BlockSpec((nb, 2, N_HEADS, HEAD, HEAD), lambda bi, ci: (bi, 0, 0, 0, 0))
    has_init = s0 is not None
    if has_init:
        in_specs.append(state)
        args.append(s0)
    y_shape = jax.ShapeDtypeStruct((n_batch, t_len, RWKV_W), F32)
    out_specs = [pl.BlockSpec((nb, CHUNK, RWKV_W), lambda bi, ci: (bi, ci, 0)),
                 pl.BlockSpec((nb, CHUNK, RWKV_W), lambda bi, ci: (bi, n_chunks - 1 - ci, 0))]
    out_shape = [y_shape, y_shape]
    if want_final:
        out_specs.append(state)
        out_shape.append(jax.ShapeDtypeStruct((n_batch, 2, N_HEADS, HEAD, HEAD), F32))
    outs = pl.pallas_call(
        functools.partial(_chunk_kernel, has_init=has_init, has_final=want_final),
        grid=(n_batch // nb, n_chunks),
        in_specs=in_specs,
        out_specs=out_specs,
        out_shape=out_shape,
        scratch_shapes=[pltpu.VMEM((2, nb, n_grp, GROUP_W, GROUP_W), F32)],
        compiler_params=_cparams(2),
        name=f"wkv_chunk_t{t_len}",
    )(*args)
    return [outs[0].reshape(n_batch * t_len, RWKV_W), outs[1].reshape(n_batch * t_len, RWKV_W)] + list(outs[2:])


def _postmix_kernel(yfp_ref, yfs_ref, ybp_ref, ybs_ref, xp_ref, xs_ref, conv_ref, bonus_ref, g_ref,
                    g1_ref, sh2_ref, sc2_ref, gng_ref, gnb_ref, ones_ref, wout_ref, n2g_ref, rhi_ref, rlo_ref,
                    x1_o, h2_o, logit_o):
    n_prompt = N_TOK // ROW_TILE
    ones = ones_ref[...]
    y = _pick(yfp_ref, yfs_ref, n_prompt) + _pick(ybp_ref, ybs_ref, n_prompt)
    mu = _split_dot(y, ones) * (1.0 / HEAD)
    d = y - mu
    var = _split_dot(d * d, ones) * (1.0 / HEAD)
    yn = d * lax.rsqrt(var + GN_EPS) * gng_ref[...] + gnb_ref[...]
    rw = ((yn + bonus_ref[...]) * g_ref[...]).astype(BF16)
    mix = (jnp.dot(conv_ref[...].astype(BF16), wout_ref[0:CONV_W, :], preferred_element_type=F32)
           + jnp.dot(rw, wout_ref[CONV_W:CONV_W + RWKV_W, :], preferred_element_type=F32))
    x1 = _pick(xp_ref, xs_ref, n_prompt) + g1_ref[0] * mix
    x1_o[...] = x1
    ms = jnp.mean(x1 * x1, axis=-1, keepdims=True)
    h2 = x1 * lax.rsqrt(ms + RMS_EPS) * n2g_ref[...]
    h2 = h2 * (1.0 + sc2_ref[0]) + sh2_ref[0]
    for c in range(TOK_ROWS):
        h2_o[pl.ds(c, ROW_TILE, stride=TOK_ROWS), :] = h2[:, c * 128:(c + 1) * 128]
    hi = h2.astype(BF16)
    lo = (h2 - hi.astype(F32)).astype(BF16)
    logit_o[...] = (jnp.dot(hi, rhi_ref[...], preferred_element_type=F32)
                    + jnp.dot(lo, rhi_ref[...], preferred_element_type=F32)
                    + jnp.dot(hi, rlo_ref[...], preferred_element_type=F32))


def _postmix(yf, yb, x, conv, bonus, g, mod3, gn_g, gn_b, ones_blk, w_out_bf16, norm2_g, r_hi, r_lo):
    n = 2 * N_TOK
    row = lambda c: pl.BlockSpec((ROW_TILE, c), lambda i: (i, 0))
    pair = lambda c: [pl.BlockSpec((ROW_TILE, c), m) for m in _group_tile(N_TOK // ROW_TILE)]
    full = lambda a: pl.BlockSpec(a.shape, lambda i: (0,) * a.ndim)
    modspec = lambda j: pl.BlockSpec((1, 1, D_MODEL), lambda i: (_cond_index(i), 0, j))
    return pl.pallas_call(
        _postmix_kernel,
        grid=(n // ROW_TILE,),
        in_specs=pair(RWKV_W) + pair(RWKV_W) + pair(D_MODEL) + [
            row(CONV_W), row(RWKV_W), row(RWKV_W), modspec(2), modspec(3), modspec(4), full(gn_g), full(gn_b), full(ones_blk),
            full(w_out_bf16), full(norm2_g), full(r_hi), full(r_lo)],
        out_specs=[row(D_MODEL), pl.BlockSpec((ROW_TILE * TOK_ROWS, 128), lambda i: (i, 0)), row(128)],
        out_shape=[jax.ShapeDtypeStruct((n, D_MODEL), F32), jax.ShapeDtypeStruct((n * TOK_ROWS, 128), F32),
                   jax.ShapeDtypeStruct((n, 128), F32)],
        compiler_params=_cparams(1),
        name="postmix",
    )(*yf, *yb, *x, conv, bonus, g, mod3, mod3, mod3, gn_g, gn_b, ones_blk, w_out_bf16, norm2_g, r_hi, r_lo)


def _prefix_lanes(m, utri):
    rows = m.shape[0]
    off = jnp.zeros((rows, 1), F32)
    out = []
    for c in range(N_TOK // 128):
        blk = m[:, c * 128:(c + 1) * 128]
        out.append(jnp.dot(blk.astype(BF16), utri, preferred_element_type=F32) + off)
        off = off + jnp.sum(blk, axis=1, keepdims=True)
    return jnp.concatenate(out, axis=1)


def _select_kernel(lt_ref, utri_ref, pos_o, aff_o):
    utri = utri_ref[...]
    affs = []
    for grp in range(2):
        x = lt_ref[grp * N_EXPERTS:(grp + 1) * N_EXPERTS, :]
        e = jnp.exp(x - jnp.max(x, axis=0, keepdims=True))
        affs.append(e / jnp.sum(e, axis=0, keepdims=True))
    aff = jnp.concatenate(affs, axis=0)
    aff_o[...] = aff
    n_rows = 2 * N_EXPERTS

    def body(_, lh):
        lo, hi = lh
        m2 = 0.5 * (lo + hi)
        m1 = 0.5 * (lo + m2)
        m3 = 0.5 * (m2 + hi)
        ge1, ge2, ge3 = (jnp.sum(jnp.where(aff > m, 1.0, 0.0), axis=1, keepdims=True) >= CAP for m in (m1, m2, m3))
        lo = jnp.where(ge3, m3, jnp.where(ge2, m2, jnp.where(ge1, m1, lo)))
        hi = jnp.where(ge3, hi, jnp.where(ge2, m3, jnp.where(ge1, m2, m1)))
        return lo, hi

    lo, hi = lax.fori_loop(0, BISECT_ITERS // 2, body,
                           (jnp.full((n_rows, 1), -1.0, F32), jnp.full((n_rows, 1), 1.0, F32)))
    gt = jnp.where(aff > hi, 1.0, 0.0)
    tie = jnp.where(aff > lo, 1.0, 0.0) - gt
    need = CAP - jnp.sum(gt, axis=1, keepdims=True)
    sel = gt + tie * jnp.where(_prefix_lanes(tie, utri) < need, 1.0, 0.0)
    pos = _prefix_lanes(sel, utri)
    pos_o[...] = jnp.where(sel > 0.5, pos, -1.0)


def _select(logits_t, utri):
    shp = jax.ShapeDtypeStruct((2 * N_EXPERTS, N_TOK), F32)
    full = lambda a: pl.BlockSpec(a.shape, lambda i: (0,) * a.ndim)
    return pl.pallas_call(
        _select_kernel,
        grid=(1,),
        in_specs=[full(logits_t), full(utri)],
        out_specs=[pl.BlockSpec(shp.shape, lambda i: (0, 0))] * 2,
        out_shape=[shp, shp],
        compiler_params=_cparams(1),
        name="ec_select",
    )(logits_t, utri)


def _compact_kernel(pos_ref, aff_ref, idx_o, gate_o):
    slot = lax.broadcasted_iota(jnp.int32, (CAP, 1), 0).astype(F32)
    idx = jnp.zeros((CAP, 128), F32)
    gate = jnp.zeros((CAP, 128), F32)
    for c in range(N_TOK // 128):
        p = pos_ref[0, :, c * 128:(c + 1) * 128]
        a = aff_ref[0, :, c * 128:(c + 1) * 128]
        tok = (lax.broadcasted_iota(jnp.int32, (1, 128), 1) + c * 128).astype(F32)
        hit = p == slot
        idx = idx + jnp.where(hit, tok, 0.0)
        gate = gate + jnp.where(hit, a, 0.0)
    idx_o[0] = jnp.broadcast_to(jnp.sum(idx, axis=1, keepdims=True), (CAP, 128))
    gate_o[0] = jnp.broadcast_to(jnp.sum(gate, axis=1, keepdims=True), (CAP, 128))


def _compact(pos3, aff3):
    rows = pos3.shape[0]
    shp = jax.ShapeDtypeStruct((rows, CAP, 128), F32)
    return pl.pallas_call(
        _compact_kernel,
        grid=(rows,),
        in_specs=[pl.BlockSpec((1, 1, N_TOK), lambda i: (i, 0, 0))] * 2,
        out_specs=[pl.BlockSpec((1, CAP, 128), lambda i: (i, 0, 0))] * 2,
        out_shape=[shp, shp],
        compiler_params=_cparams(1),
        name="ec_compact",
    )(pos3, aff3)


def _gather_kernel(idx_ref, h_ref, o_ref, tile):
    grp = pl.program_id(0)
    eb = pl.program_id(1)
    for ei in range(ROUTE_EXPERTS):
        base = (grp * N_EXPERTS + eb * ROUTE_EXPERTS + ei) * CAP

        def tok(j8, c, base=base):
            for u in range(8):
                j = j8 * 8 + u
                row = pl.multiple_of(idx_ref[base + j] * 8, 8)
                tile[pl.ds(j, 8, stride=TOK_STRIDE), :] = h_ref[0, pl.ds(row, 8), :]
            return c

        lax.fori_loop(0, CAP // 8, tok, 0)
        for c in range(TOK_ROWS):
            o_ref[ei, :, c * 128:(c + 1) * 128] = tile[c * TOK_STRIDE:c * TOK_STRIDE + CAP, :].astype(BF16)


def _gather(idx_flat, h_rows):
    return pl.pallas_call(
        _gather_kernel,
        grid_spec=pltpu.PrefetchScalarGridSpec(
            num_scalar_prefetch=1,
            grid=(2, N_EXPERTS // ROUTE_EXPERTS),
            in_specs=[pl.BlockSpec((1, N_TOK * 8, 128), lambda g, e, idx: (g, 0, 0))],
            out_specs=pl.BlockSpec((ROUTE_EXPERTS, CAP, D_MODEL), lambda g, e, idx: (e, g, 0)),
            scratch_shapes=[pltpu.VMEM((8 * TOK_STRIDE, 128), F32)]),
        out_shape=jax.ShapeDtypeStruct((N_EXPERTS, 2 * CAP, D_MODEL), BF16),
        compiler_params=_cparams(2),
        name="ec_gather",
    )(idx_flat, h_rows)


def _expert_kernel(xe_ref, w1_ref, w3_ref, w2_ref, gate_ref, o_ref, hid_scr):
    s = pl.program_id(1)
    subs = [(lo, min(FF_SUB, FF_SLAB - lo)) for lo in range(0, FF_SLAB, FF_SUB)]

    @pl.when(s < FF_SPLIT)
    def _():
        x = xe_ref[0]
        for lo, width in subs:
            h1 = jnp.dot(x, w1_ref[0, :, lo:lo + width].astype(BF16), preferred_element_type=F32)
            h3 = jnp.dot(x, w3_ref[0, :, lo:lo + width].astype(BF16), preferred_element_type=F32)
            hid_scr[s, :, lo:lo + width] = (h1 * _sigmoid(h1) * h3).astype(BF16)

    @pl.when(s >= FF_SPLIT)
    def _():
        acc = None
        for lo, width in subs:
            part = jnp.dot(hid_scr[s - FF_SPLIT, :, lo:lo + width], w2_ref[0, lo:lo + width, :].astype(BF16),
                           preferred_element_type=F32)
            acc = part if acc is None else acc + part

        @pl.when(s == FF_SPLIT)
        def _():
            o_ref[0] = acc

        @pl.when(s > FF_SPLIT)
        def _():
            o_ref[0] = o_ref[0] + acc

    @pl.when(s == 2 * FF_SPLIT - 1)
    def _():
        o_ref[0] = o_ref[0] * gate_ref[0]


def _experts(xe, w1, w3, w2, gate_col):
    m = xe.shape[1]
    up = lambda e, s: (e, 0, jnp.minimum(s, FF_SPLIT - 1))
    down = lambda e, s: (e, jnp.maximum(s - FF_SPLIT, 0), 0)
    return pl.pallas_call(
        _expert_kernel,
        grid=(N_EXPERTS, 2 * FF_SPLIT),
        in_specs=[pl.BlockSpec((1, m, D_MODEL), lambda e, s: (e, 0, 0)),
                  pl.BlockSpec((1, D_MODEL, FF_SLAB), up),
                  pl.BlockSpec((1, D_MODEL, FF_SLAB), up),
                  pl.BlockSpec((1, FF_SLAB, D_MODEL), down),
                  pl.BlockSpec((1, m, 1), lambda e, s: (e, 0, 0))],
        out_specs=pl.BlockSpec((1, m, D_MODEL), lambda e, s: (e, 0, 0)),
        out_shape=jax.ShapeDtypeStruct((N_EXPERTS, m, D_MODEL), F32),
        scratch_shapes=[pltpu.VMEM((FF_SPLIT, m, FF_SLAB), BF16)],
        compiler_params=_cparams(2, vmem=EXPERT_VMEM_LIMIT),
        name="ec_experts",
    )(xe, w1, w3, w2, gate_col)


def _combine_kernel(idx_ref, ye_ref, x1_ref, g2_ref, fg_ref, op_ref, os_ref, acc, tile):
    grp = pl.program_id(0)
    s = pl.program_id(1)
    n_scatter = N_EXPERTS // ROUTE_EXPERTS

    @pl.when(s == 0)
    def _():
        acc[...] = jnp.zeros_like(acc)

    @pl.when(s < n_scatter)
    def _():
        for ei in range(ROUTE_EXPERTS):
            base = (grp * N_EXPERTS + s * ROUTE_EXPERTS + ei) * CAP
            for c in range(TOK_ROWS):
                tile[c * TOK_STRIDE:c * TOK_STRIDE + CAP, :] = ye_ref[ei, :, c * 128:(c + 1) * 128]

            def tok(j4, c, base=base):
                rows, vals = [], []
                for u in range(4):
                    j = j4 * 4 + u
                    row = pl.multiple_of(idx_ref[base + j] * 8, 8)
                    rows.append(row)
                    vals.append(acc[pl.ds(row, 8), :] + tile[pl.ds(j, 8, stride=TOK_STRIDE), :])
                for row, val in zip(rows, vals):
                    acc[pl.ds(row, 8), :] = val
                return c

            lax.fori_loop(0, CAP // 4, tok, 0)

    @pl.when(s >= n_scatter)
    def _():
        row0 = (s - n_scatter) * (ROW_TILE * TOK_ROWS)
        y = jnp.concatenate([acc[pl.ds(row0 + c, ROW_TILE, stride=TOK_ROWS), :] for c in range(TOK_ROWS)], axis=1)
        x = x1_ref[...] + g2_ref[0] * y
        ms = jnp.mean(x * x, axis=-1, keepdims=True)
        out = x * lax.rsqrt(ms + RMS_EPS) * fg_ref[...]

        @pl.when(grp == 0)
        def _():
            op_ref[...] = out

        @pl.when(grp == 1)
        def _():
            os_ref[...] = out


def _combine_final(idx_flat, ye, x1, mod3, final_g):
    n_scatter = N_EXPERTS // ROUTE_EXPERTS
    tiles = N_TOK // ROW_TILE
    tile_of = lambda g, s: g * tiles + jnp.maximum(s - n_scatter, 0)
    shp = jax.ShapeDtypeStruct((N_TOK, D_MODEL), F32)
    return pl.pallas_call(
        _combine_kernel,
        grid_spec=pltpu.PrefetchScalarGridSpec(
            num_scalar_prefetch=1,
            grid=(2, n_scatter + tiles),
            in_specs=[pl.BlockSpec((ROUTE_EXPERTS, CAP, D_MODEL), lambda g, s, idx: (jnp.minimum(s, n_scatter - 1), g, 0)),
                      pl.BlockSpec((ROW_TILE, D_MODEL), lambda g, s, idx: (tile_of(g, s), 0)),
                      pl.BlockSpec((1, 1, D_MODEL), lambda g, s, idx: (_cond_index(tile_of(g, s)), 0, 5)),
                      pl.BlockSpec((1, D_MODEL), lambda g, s, idx: (0, 0))],
            out_specs=[pl.BlockSpec((ROW_TILE, D_MODEL),
                                    lambda g, s, idx: (jnp.where(g == 0, jnp.maximum(s - n_scatter, 0), tiles - 1), 0)),
                       pl.BlockSpec((ROW_TILE, D_MODEL),
                                    lambda g, s, idx: (jnp.where(g == 1, jnp.maximum(s - n_scatter, 0), 0), 0))],
            scratch_shapes=[pltpu.VMEM((N_TOK * TOK_ROWS, 128), F32), pltpu.VMEM((TOK_ROWS * TOK_STRIDE, 128), F32)]),
        out_shape=[shp, shp],
        compiler_params=_cparams(2),
        name="ec_combine_final",
    )(idx_flat, ye, x1, mod3, final_g)


def kernel(x_prompt, x_sample, state_wkv, c, c_ctx, norm1_g, norm2_g, w_mod, b_mod, w_in, conv_dw, conv_b, conv_ln_g,
           conv_ln_b, rwkv_w0, rwkv_w_up, rwkv_a0, rwkv_a_up, rwkv_g_up, rwkv_k_k, rwkv_k_a, rwkv_r_k, rwkv_gn_g,
           rwkv_gn_b, w_out, router, exp_w1, exp_w3, exp_w2, final_norm_g):
    row2 = lambda a: a.reshape(1, -1)
    x = (x_prompt.reshape(N_TOK, D_MODEL), x_sample.reshape(N_TOK, D_MODEL))
    cond8 = jnp.concatenate([c_ctx[None, :], c, jnp.zeros((8 - 1 - DEC_BATCH, D_MODEL), F32)], axis=0)

    mod = _modulation(cond8, w_mod[0], b_mod)
    mod3 = mod.reshape(8, 1, 6 * D_MODEL)

    lane_head = jnp.arange(RWKV_W, dtype=jnp.int32) // HEAD
    ones_blk = (lane_head[:, None] == lane_head[None, :]).astype(BF16)

    def per_dir_rows(up):
        z = jnp.zeros_like(up[0])
        return jnp.stack([jnp.concatenate([up[0], z], axis=0), jnp.concatenate([z, up[1]], axis=0)]).astype(BF16)
    conv, rkv, kap, w0, w1, b0, b1, kd0, kd1, g, bonus = _mixer_in(
        *x, mod3, norm1_g, w_in[0].astype(BF16), (conv_dw[0], conv_b, conv_ln_g, conv_ln_b),
        (ones_blk, rwkv_w0[0], per_dir_rows(rwkv_w_up[0]), rwkv_a0[0], per_dir_rows(rwkv_a_up[0]),
         rwkv_g_up[0].astype(BF16), rwkv_k_k, rwkv_k_a, row2(rwkv_r_k[0])))

    dir_ops = ((w0, b0, kd0), (w1, b1, kd1))
    yp_f, yp_b, new_state = _chunk_scan(rkv, kap, dir_ops, None, row0=0, n_batch=BATCH, t_len=SEQ, want_final=True)
    ys_f, ys_b = _chunk_scan(rkv, kap, dir_ops, state_wkv[:, 0], row0=N_TOK, n_batch=DEC_BATCH, t_len=DEC_SEQ,
                             want_final=False)

    r_pad = jnp.pad(router[0], ((0, 0), (0, 128 - N_EXPERTS)))
    r_hi = r_pad.astype(BF16)
    r_lo = (r_pad - r_hi.astype(F32)).astype(BF16)
    x1, h2_rows, logits = _postmix((yp_f, ys_f), (yp_b, ys_b), x, conv, bonus, g, mod3, rwkv_gn_g, rwkv_gn_b, ones_blk,
                                   w_out[0].astype(BF16), norm2_g, r_hi, r_lo)

    logits_t = logits[:, :N_EXPERTS].reshape(2, N_TOK, N_EXPERTS).transpose(0, 2, 1).reshape(2 * N_EXPERTS, N_TOK)
    lane = jnp.arange(128, dtype=jnp.int32)
    utri = (lane[:, None] < lane[None, :]).astype(BF16)
    pos, aff = _select(logits_t, utri)
    idx_c, gate_c = _compact(pos.reshape(2 * N_EXPERTS, 1, N_TOK), aff.reshape(2 * N_EXPERTS, 1, N_TOK))
    idx_flat = idx_c[:, :, 0].astype(jnp.int32).reshape(-1)
    gate_col = gate_c[:, :, 0].reshape(2, N_EXPERTS, CAP).transpose(1, 0, 2).reshape(N_EXPERTS, 2 * CAP, 1)

    xe = _gather(idx_flat, h2_rows.reshape(2, N_TOK * TOK_ROWS, 128))
    ye = _experts(xe, exp_w1[0], exp_w3[0], exp_w2[0], gate_col)
    out_p, out_s = _combine_final(idx_flat, ye, x1, mod3, row2(final_norm_g))
    return (out_p.reshape(BATCH, SEQ, D_MODEL), out_s.reshape(DEC_BATCH, DEC_SEQ, D_MODEL),
            new_state.reshape(BATCH, 1, 2, N_HEADS, HEAD, HEAD))
```

```python
import functools

import jax
import jax.numpy as jnp
from jax import lax
from jax.experimental import pallas as pl
from jax.experimental.pallas import tpu as pltpu

F32 = jnp.float32
BF16 = jnp.bfloat16

D_MODEL = 1024
BATCH = 16
SEQ = 256
DEC_BATCH = 4
DEC_SEQ = 1024
GRID_W = 64
CONV_W = 512
CONV_K = 31
CONV_PAD = 15
RWKV_W = 512
HEAD = 64
N_HEADS = 8
LORA = 64
GATE_LORA = 128
N_EXPERTS = 16
D_FF = 2816
D_IN = 2944
CAP = 512
N_TOK = 4096
RMS_EPS = 1e-6
LN_EPS = 1e-5
GN_EPS = 64e-5

ROW_TILE = 512
FF_SPLIT = 2
FF_SLAB = D_FF // FF_SPLIT
FF_SUB = 256
EXPERT_VMEM_LIMIT = 60 * 1024 * 1024
CHUNK = 64
HEADS_PER_GROUP = 4
GROUP_W = HEADS_PER_GROUP * HEAD
PASSES_A = 3
PASSES_INV = 1
PASSES_STATE = 1
SCAN_BATCHES = 4
TOK_ROWS = D_MODEL // 128
TOK_STRIDE = CAP + 8
ROUTE_EXPERTS = 4
COMPACT_WIN = 128 + 8
BISECT_ITERS = 160
VMEM_LIMIT = 56 * 1024 * 1024


def _cparams(n_axes, vmem=None):
    return pltpu.CompilerParams(dimension_semantics=("arbitrary",) * n_axes,
                                vmem_limit_bytes=vmem or VMEM_LIMIT)


def _cond_index(i):
    n_prompt = (BATCH * SEQ) // ROW_TILE
    per_batch = DEC_SEQ // ROW_TILE
    return jnp.where(i < n_prompt, 0, 1 + (i - n_prompt) // per_batch)


def _split_dot(x, w_bf16):
    hi = x.astype(BF16)
    lo = (x - hi.astype(F32)).astype(BF16)
    return (jnp.dot(hi, w_bf16, preferred_element_type=F32)
            + jnp.dot(lo, w_bf16, preferred_element_type=F32))


def _head_sums(x, ones_bf16):
    return jnp.concatenate([_split_dot(x[:, c:c + GROUP_W], ones_bf16) for c in range(0, x.shape[1], GROUP_W)],
                           axis=1)


def _sigmoid(x):
    return 1.0 / (1.0 + jnp.exp(-x))


def _mod_kernel(c_ref, w_ref, b_ref, o_ref):
    c = c_ref[...]
    s = (c * _sigmoid(c)).astype(BF16)
    o_ref[...] = jnp.dot(s, w_ref[...].astype(BF16), preferred_element_type=F32) + b_ref[...]


def _modulation(cond8, w_mod, b_mod):
    n = 6 * D_MODEL
    tn = D_MODEL
    return pl.pallas_call(
        _mod_kernel,
        grid=(n // tn,),
        in_specs=[pl.BlockSpec((8, D_MODEL), lambda j: (0, 0)),
                  pl.BlockSpec((D_MODEL, tn), lambda j: (0, j)),
                  pl.BlockSpec((1, tn), lambda j: (0, j))],
        out_specs=pl.BlockSpec((8, tn), lambda j: (0, j)),
        out_shape=jax.ShapeDtypeStruct((8, n), F32),
        compiler_params=_cparams(1),
        name="modulation",
    )(cond8, w_mod, b_mod)


def _group_tile(n_prompt_tiles):
    return (lambda i: (jnp.minimum(i, n_prompt_tiles - 1), 0)), (lambda i: (jnp.maximum(i - n_prompt_tiles, 0), 0))


def _pick(prompt_ref, sample_ref, n_prompt_tiles):
    return jnp.where(pl.program_id(0) < n_prompt_tiles, prompt_ref[...], sample_ref[...])


def _mixer_in_kernel(xp_ref, xs_ref, sh_ref, sc_ref, n1g_ref, w_ref, dw_ref, db_ref, lng_ref, lnb_ref, ones_ref,
                     w0_ref, wup_ref, a0_ref, aup_ref, gup_ref, kk_ref, ka_ref, rk_ref,
                     conv_o, rkv_o, kap_o, lw0_o, lw1_o, b0_o, b1_o, kd0_o, kd1_o, g_o, bonus_o):
    n_prompt = N_TOK // ROW_TILE
    x = _pick(xp_ref, xs_ref, n_prompt)
    ms = jnp.mean(x * x, axis=-1, keepdims=True)
    h = x * lax.rsqrt(ms + RMS_EPS) * n1g_ref[...]
    h = (h * (1.0 + sc_ref[0]) + sh_ref[0]).astype(BF16)
    uv = jnp.dot(h, w_ref[:, 0:CONV_W], preferred_element_type=F32)
    ug = jnp.dot(h, w_ref[:, CONV_W:2 * CONV_W], preferred_element_type=F32)
    seg = jnp.where(pl.program_id(0) < n_prompt, SEQ, GRID_W)
    conv_o[...] = _conv_ln_silu(uv * _sigmoid(ug), seg, dw_ref, db_ref, lng_ref, lnb_ref)
    rkv = jnp.dot(h, w_ref[:, 2 * CONV_W:2 * CONV_W + 3 * RWKV_W], preferred_element_type=F32)
    rkv_o[...] = rkv
    lora = jnp.dot(h, w_ref[:, 2 * CONV_W + 3 * RWKV_W:D_IN], preferred_element_type=F32)
    _rwkv_operands(rkv, lora, ones_ref, w0_ref, wup_ref, a0_ref, aup_ref, gup_ref, kk_ref, ka_ref, rk_ref,
                   kap_o, (lw0_o, lw1_o), (b0_o, b1_o), (kd0_o, kd1_o), g_o, bonus_o)


def _mixer_in(x_prompt, x_sample, mod3, norm1_g, w_in_bf16, conv_params, rwkv_params):
    n = 2 * N_TOK
    first, second = _group_tile(N_TOK // ROW_TILE)
    row = lambda c: pl.BlockSpec((ROW_TILE, c), lambda i: (i, 0))
    full = lambda a: pl.BlockSpec(a.shape, lambda i: (0,) * a.ndim)
    consts = (norm1_g, w_in_bf16) + tuple(conv_params) + tuple(rwkv_params)
    widths = [CONV_W, 3 * RWKV_W] + [RWKV_W] * 9
    return pl.pallas_call(
        _mixer_in_kernel,
        grid=(n // ROW_TILE,),
        in_specs=[pl.BlockSpec((ROW_TILE, D_MODEL), first),
                  pl.BlockSpec((ROW_TILE, D_MODEL), second),
                  pl.BlockSpec((1, 1, D_MODEL), lambda i: (_cond_index(i), 0, 0)),
                  pl.BlockSpec((1, 1, D_MODEL), lambda i: (_cond_index(i), 0, 1)),
                  ] + [full(a) for a in consts],
        out_specs=[row(c) for c in widths],
        out_shape=[jax.ShapeDtypeStruct((n, c), F32) for c in widths],
        compiler_params=_cparams(1),
        name="mixer_in",
    )(x_prompt, x_sample, mod3, mod3, *consts)


def _conv_ln_silu(u, seg, dw_ref, db_ref, g_ref, b_ref):
    rows = u.shape[0]
    pos = lax.broadcasted_iota(jnp.int32, (rows, 1), 0) & (seg - 1)
    acc = jnp.zeros_like(u)
    for k in range(CONV_K):
        s = k - CONV_PAD
        shifted = u if s == 0 else pltpu.roll(u, (-s) % rows, 0)
        valid = (pos + s >= 0) & (pos + s < seg)
        acc = acc + jnp.where(valid, shifted, 0.0) * dw_ref[k:k + 1, :]
    out = acc + db_ref[...]
    mu = jnp.mean(out, axis=-1, keepdims=True)
    d = out - mu
    var = jnp.mean(d * d, axis=-1, keepdims=True)
    out = d * lax.rsqrt(var + LN_EPS) * g_ref[...] + b_ref[...]
    return out * _sigmoid(out)


def _rwkv_operands(rkv, lora, ones_ref, w0_ref, wup_ref, a0_ref, aup_ref, gup_ref, kk_ref, ka_ref, rk_ref,
                   kap_o, lw_o, b_o, kd_o, g_o, bonus_o):
    r = rkv[:, 0:RWKV_W]
    k = rkv[:, RWKV_W:2 * RWKV_W]
    v = rkv[:, 2 * RWKV_W:3 * RWKV_W]
    xw = jnp.tanh(lora[:, 0:2 * LORA]).astype(BF16)
    xa = lora[:, 2 * LORA:4 * LORA].astype(BF16)
    xg = _sigmoid(lora[:, 4 * LORA:4 * LORA + GATE_LORA]).astype(BF16)
    ones = ones_ref[...]

    kk0 = k * kk_ref[...]
    ss = _head_sums(kk0 * kk0, ones)
    kap = kk0 * lax.rsqrt(jnp.maximum(ss, 1e-24))
    kap_o[...] = kap
    g_o[...] = jnp.dot(xg, gup_ref[...], preferred_element_type=F32)

    kd_sum = jnp.zeros_like(k)
    for d in range(2):
        wl = w0_ref[d:d + 1, :] + jnp.dot(xw, wup_ref[d], preferred_element_type=F32)
        lw_o[d][...] = -jnp.exp(-0.5) * _sigmoid(wl)
        a = _sigmoid(a0_ref[d:d + 1, :] + jnp.dot(xa, aup_ref[d], preferred_element_type=F32))
        b_o[d][...] = kap * a
        kd = k * (1.0 + (a - 1.0) * ka_ref[...])
        kd_o[d][...] = kd
        kd_sum = kd_sum + kd
    bonus_o[...] = _head_sums(r * kd_sum * rk_ref[...], ones) * v


def _pieces(x, passes):
    hi = x.astype(BF16)
    return (hi,) if passes == 1 else (hi, (x - hi.astype(F32)).astype(BF16))


_NN = ((1,), (0,))
_NT = ((1,), (1,))
_TN = ((0,), (0,))


def _mm(a, b, dims):
    dg = lambda p, q: lax.dot_general(p, q, (dims, ((), ())), preferred_element_type=F32)
    if len(a) == 1:
        out = dg(a[0], b[0])
    else:
        free = 1 - dims[0][0]
        both = dg(jnp.concatenate(a, axis=free), b[0])
        m = a[0].shape[free]
        out = both[:m] + both[m:]
    if len(b) > 1:
        out = out + dg(a[0], b[1])
    return out


def _block_diag(pieces, head_masks):
    return tuple(jnp.concatenate([p * m for m in head_masks], axis=0) for p in pieces)


def _chunk_masks(reverse):
    row = lax.broadcasted_iota(jnp.int32, (CHUNK, GROUP_W), 0)
    col = lax.broadcasted_iota(jnp.int32, (CHUNK, GROUP_W), 1) % CHUNK
    strict = (col > row) if reverse else (col < row)
    incl = (col >= row) if reverse else (col <= row)
    eye_side = jnp.where(col == row, 1.0, 0.0)
    tri = jnp.where(incl[:, :CHUNK], 1.0, 0.0).astype(BF16)
    return strict, incl, eye_side, tri, (0 if reverse else CHUNK - 1)


def _chunk_steps(chains):
    ln = CHUNK
    each = lambda f, *cols: [f(*args) for args in zip(*cols)]
    r, kap, v, lw, b, kd, n, masks = (list(col) for col in zip(*chains))
    strict = [m[0] for m in masks]
    incl = [m[1] for m in masks]
    lane_blk = lax.broadcasted_iota(jnp.int32, (ln, GROUP_W), 1) // HEAD
    head_masks = [jnp.where(lane_blk == h, 1.0, 0.0).astype(BF16) for h in range(HEADS_PER_GROUP)]
    bd = lambda x, passes: _block_diag(_pieces(x, passes), head_masks)
    r2 = lax.broadcasted_iota(jnp.int32, (GROUP_W, GROUP_W), 0) // HEAD
    c2 = lax.broadcasted_iota(jnp.int32, (GROUP_W, GROUP_W), 1) // HEAD
    same_head = r2 == c2

    def running_sum(lw_c, m):
        return _mm((m[3],), _pieces(lw_c, 3), _NN)

    cw = each(running_sum, lw, masks)
    tot = each(lambda cw_c, m: cw_c[m[4]:m[4] + 1, :], cw, masks)
    e_neg = each(lambda cw_c: jnp.exp(-cw_c), cw)
    kr = each(lambda kap_c, r_c, cw_c, lw_c: _pieces(
        jnp.concatenate([kap_c * jnp.exp(cw_c - lw_c), r_c * jnp.exp(cw_c)], axis=0), PASSES_A),
        kap, r, cw, lw)
    ab = each(lambda kr_c, b_c, e: _mm(kr_c, bd(b_c * e, PASSES_A), _NT), kr, b, e_neg)
    ak = each(lambda kr_c, kd_c, e: _mm(kr_c, bd(kd_c * e, PASSES_A), _NT), kr, kd, e_neg)

    q = each(lambda ab_c, s: jnp.where(s, -ab_c[:ln], 0.0), ab, strict)
    x = each(lambda q_c, m: m[2] + q_c, q, masks)
    q = each(lambda q_c: _mm(_pieces(q_c, PASSES_INV), bd(q_c, PASSES_INV), _NN), q)
    n_rounds = CHUNK.bit_length() - 2
    for j in range(1, n_rounds + 1):
        last_round = j == n_rounds
        lhs = x if last_round else each(lambda q_c, x_c: jnp.concatenate([q_c, x_c], axis=0), q, x)
        prod = each(lambda l_c, q_c: _mm(_pieces(l_c, PASSES_INV), bd(q_c, PASSES_INV), _NN), lhs, q)
        if last_round:
            x = each(lambda x_c, p_c: x_c + p_c, x, prod)
        else:
            q = each(lambda p_c: p_c[:ln], prod)
            x = each(lambda x_c, p_c: x_c + p_c[ln:], x, prod)

    a3 = each(lambda ab_c, i: jnp.where(i, ab_c[ln:], 0.0), ab, incl)
    a24 = each(lambda ak_c, s, i: jnp.concatenate([jnp.where(s, ak_c[:ln], 0.0), jnp.where(i, ak_c[ln:], 0.0)], axis=0),
               ak, strict, incl)
    av = each(lambda a_c, v_c: _mm(_pieces(a_c, PASSES_A), bd(v_c, PASSES_A), _NN), a24, v)
    krn = each(lambda kr_c, n_c: _mm(kr_c[:PASSES_STATE], _pieces(n_c, PASSES_STATE), _NT), kr, n)
    u = each(lambda x_c, krn_c, av_c: -_mm(_pieces(x_c, PASSES_INV), bd(krn_c[:ln] + av_c[:ln], PASSES_INV), _NN),
             x, krn, av)
    y = each(lambda krn_c, a3_c, u_c, av_c: krn_c[ln:] + _mm(_pieces(a3_c, PASSES_STATE), bd(u_c, PASSES_STATE), _NN)
             + av_c[ln:], krn, a3, u, av)
    e_rest = each(lambda tot_c, cw_c: jnp.exp(tot_c - cw_c), tot, cw)
    upd = each(lambda u_c, v_c, b_c, kd_c, e: _mm(_pieces(jnp.concatenate([u_c, v_c], axis=0), PASSES_A),
                                                   _pieces(jnp.concatenate([b_c * e, kd_c * e], axis=0), PASSES_A), _TN),
               u, v, b, kd, e_rest)
    n_new = each(lambda n_c, tot_c, upd_c: n_c * jnp.exp(tot_c) + jnp.where(same_head, upd_c, 0.0), n, tot, upd)
    return list(zip(y, n_new))


def _chunk_kernel(*refs, has_init, has_final):
    n_tok = 12 * SCAN_BATCHES
    tok_refs = [refs[k * 12:(k + 1) * 12] for k in range(SCAN_BATCHES)]
    refs = refs[n_tok:]
    s0 = refs[0] if has_init else None
    refs = refs[1:] if has_init else refs
    y_refs = refs[:2]
    sfin = refs[2] if has_final else None
    n_scr = refs[-1]
    c = pl.program_id(1)
    head_blocks = [(k, d, g, h) for k in range(SCAN_BATCHES) for d in range(2)
                   for g in range(N_HEADS // HEADS_PER_GROUP) for h in range(HEADS_PER_GROUP)]
    diag = lambda h: slice(h * HEAD, (h + 1) * HEAD)

    @pl.when(c == 0)
    def _():
        n_scr[...] = jnp.zeros(n_scr.shape, F32)
        if has_init:
            for k, d, g, h in head_blocks:
                n_scr[d, k, g, diag(h), diag(h)] = s0[k, d, g * HEADS_PER_GROUP + h]

    slots = [(k, d, g) for k in range(SCAN_BATCHES) for d in range(2) for g in range(N_HEADS // HEADS_PER_GROUP)]
    lanes = lambda g: slice(g * GROUP_W, (g + 1) * GROUP_W)
    masks = [_chunk_masks(reverse=False), _chunk_masks(reverse=True)]

    def operands(k, d, g):
        rf, kf, vf, rb, kb, vb, lwf, bf, kdf, lwb, bb, kdb = tok_refs[k]
        group = (rb, kb, vb, lwb, bb, kdb) if d else (rf, kf, vf, lwf, bf, kdf)
        return tuple(ref[:, lanes(g)] for ref in group)

    chains = [operands(k, d, g) + (n_scr[d, k, g], masks[d]) for k, d, g in slots]
    for (k, d, g), (y, n_new) in zip(slots, _chunk_steps(chains)):
        y_refs[d][k, :, lanes(g)] = y
        n_scr[d, k, g] = n_new

    if has_final:
        @pl.when(c == pl.num_programs(1) - 1)
        def _():
            for k, d, g, h in head_blocks:
                sfin[k, d, g * HEADS_PER_GROUP + h] = n_scr[d, k, g, diag(h), diag(h)]


def _chunk_scan(rkv, kap, dir_ops, s0, *, row0, n_batch, t_len, want_final):
    n_chunks = t_len // CHUNK
    blk0 = row0 // CHUNK
    n_grp = N_HEADS // HEADS_PER_GROUP
    nb = SCAN_BATCHES

    def tok_spec(k, reverse, col=0):
        def index(bi, ci):
            return (blk0 + (bi * nb + k) * n_chunks + (n_chunks - 1 - ci if reverse else ci), col)
        return pl.BlockSpec((CHUNK, RWKV_W), index)

    in_specs, args = [], []
    for k in range(nb):
        for rev in (False, True):
            in_specs += [tok_spec(k, rev, 0), tok_spec(k, rev), tok_spec(k, rev, 2)]
        in_specs += [tok_spec(k, False)] * 3 + [tok_spec(k, True)] * 3
        args += [rkv, kap, rkv, rkv, kap, rkv, *dir_ops[0], *dir_ops[1]]
    state = pl.BlockSpec((nb, 2, N_HEADS, HEAD, HEAD), lambda bi, ci: (bi, 0, 0, 0, 0))
    has_init = s0 is not None
    if has_init:
        in_specs.append(state)
        args.append(s0)
    y_shape = jax.ShapeDtypeStruct((n_batch, t_len, RWKV_W), F32)
    out_specs = [pl.BlockSpec((nb, CHUNK, RWKV_W), lambda bi, ci: (bi, ci, 0)),
                 pl.BlockSpec((nb, CHUNK, RWKV_W), lambda bi, ci: (bi, n_chunks - 1 - ci, 0))]
    out_shape = [y_shape, y_shape]
    if want_final:
        out_specs.append(state)
        out_shape.append(jax.ShapeDtypeStruct((n_batch, 2, N_HEADS, HEAD, HEAD), F32))
    outs = pl.pallas_call(
        functools.partial(_chunk_kernel, has_init=has_init, has_final=want_final),
        grid=(n_batch // nb, n_chunks),
        in_specs=in_specs,
        out_specs=out_specs,
        out_shape=out_shape,
        scratch_shapes=[pltpu.VMEM((2, nb, n_grp, GROUP_W, GROUP_W), F32)],
        compiler_params=_cparams(2),
        name=f"wkv_chunk_t{t_len}",
    )(*args)
    return [outs[0].reshape(n_batch * t_len, RWKV_W), outs[1].reshape(n_batch * t_len, RWKV_W)] + list(outs[2:])


def _postmix_kernel(yfp_ref, yfs_ref, ybp_ref, ybs_ref, xp_ref, xs_ref, conv_ref, bonus_ref, g_ref,
                    g1_ref, sh2_ref, sc2_ref, gng_ref, gnb_ref, ones_ref, wout_ref, n2g_ref, rhi_ref, rlo_ref,
                    x1_o, h2_o, logit_o):
    n_prompt = N_TOK // ROW_TILE
    ones = ones_ref[...]
    y = _pick(yfp_ref, yfs_ref, n_prompt) + _pick(ybp_ref, ybs_ref, n_prompt)
    mu = _head_sums(y, ones) * (1.0 / HEAD)
    d = y - mu
    var = _head_sums(d * d, ones) * (1.0 / HEAD)
    yn = d * lax.rsqrt(var + GN_EPS) * gng_ref[...] + gnb_ref[...]
    rw = ((yn + bonus_ref[...]) * g_ref[...]).astype(BF16)
    mix = (jnp.dot(conv_ref[...].astype(BF16), wout_ref[0:CONV_W, :], preferred_element_type=F32)
           + jnp.dot(rw, wout_ref[CONV_W:CONV_W + RWKV_W, :], preferred_element_type=F32))
    x1 = _pick(xp_ref, xs_ref, n_prompt) + g1_ref[0] * mix
    x1_o[...] = x1
    ms = jnp.mean(x1 * x1, axis=-1, keepdims=True)
    h2 = x1 * lax.rsqrt(ms + RMS_EPS) * n2g_ref[...]
    h2 = h2 * (1.0 + sc2_ref[0]) + sh2_ref[0]
    for c in range(TOK_ROWS):
        h2_o[pl.ds(c, ROW_TILE, stride=TOK_ROWS), :] = h2[:, c * 128:(c + 1) * 128]
    hi = h2.astype(BF16)
    lo = (h2 - hi.astype(F32)).astype(BF16)
    logit_o[...] = (jnp.dot(hi, rhi_ref[...], preferred_element_type=F32)
                    + jnp.dot(lo, rhi_ref[...], preferred_element_type=F32)
                    + jnp.dot(hi, rlo_ref[...], preferred_element_type=F32))


def _postmix(yf, yb, x, conv, bonus, g, mod3, gn_g, gn_b, ones_blk, w_out_bf16, norm2_g, r_hi, r_lo):
    n = 2 * N_TOK
    row = lambda c: pl.BlockSpec((ROW_TILE, c), lambda i: (i, 0))
    pair = lambda c: [pl.BlockSpec((ROW_TILE, c), m) for m in _group_tile(N_TOK // ROW_TILE)]
    full = lambda a: pl.BlockSpec(a.shape, lambda i: (0,) * a.ndim)
    modspec = lambda j: pl.BlockSpec((1, 1, D_MODEL), lambda i: (_cond_index(i), 0, j))
    return pl.pallas_call(
        _postmix_kernel,
        grid=(n // ROW_TILE,),
        in_specs=pair(RWKV_W) + pair(RWKV_W) + pair(D_MODEL) + [
            row(CONV_W), row(RWKV_W), row(RWKV_W), modspec(2), modspec(3), modspec(4), full(gn_g), full(gn_b), full(ones_blk),
            full(w_out_bf16), full(norm2_g), full(r_hi), full(r_lo)],
        out_specs=[row(D_MODEL), pl.BlockSpec((ROW_TILE * TOK_ROWS, 128), lambda i: (i, 0)), row(128)],
        out_shape=[jax.ShapeDtypeStruct((n, D_MODEL), F32), jax.ShapeDtypeStruct((n * TOK_ROWS, 128), F32),
                   jax.ShapeDtypeStruct((n, 128), F32)],
        compiler_params=_cparams(1),
        name="postmix",
    )(*yf, *yb, *x, conv, bonus, g, mod3, mod3, mod3, gn_g, gn_b, ones_blk, w_out_bf16, norm2_g, r_hi, r_lo)


def _prefix_lanes(m, utri):
    rows = m.shape[0]
    off = jnp.zeros((rows, 1), F32)
    out, starts = [], []
    for c in range(N_TOK // 128):
        blk = m[:, c * 128:(c + 1) * 128]
        out.append(jnp.dot(blk.astype(BF16), utri, preferred_element_type=F32) + off)
        starts.append(off)
        off = off + jnp.sum(blk, axis=1, keepdims=True)
    return jnp.concatenate(out, axis=1), jnp.concatenate(starts, axis=1)


def _select_kernel(lt_ref, utri_ref, pos_o, aff_o, start_o):
    utri = utri_ref[...]
    affs = []
    for grp in range(2):
        x = lt_ref[grp * N_EXPERTS:(grp + 1) * N_EXPERTS, :]
        e = jnp.exp(x - jnp.max(x, axis=0, keepdims=True))
        affs.append(e / jnp.sum(e, axis=0, keepdims=True))
    aff = jnp.concatenate(affs, axis=0)
    aff_o[...] = aff
    n_rows = 2 * N_EXPERTS

    def body(_, lh):
        lo, hi = lh
        m2 = 0.5 * (lo + hi)
        m1 = 0.5 * (lo + m2)
        m3 = 0.5 * (m2 + hi)
        ge1, ge2, ge3 = (jnp.sum(jnp.where(aff > m, 1.0, 0.0), axis=1, keepdims=True) >= CAP for m in (m1, m2, m3))
        lo = jnp.where(ge3, m3, jnp.where(ge2, m2, jnp.where(ge1, m1, lo)))
        hi = jnp.where(ge3, hi, jnp.where(ge2, m3, jnp.where(ge1, m2, m1)))
        return lo, hi

    lo, hi = lax.fori_loop(0, BISECT_ITERS // 2, body,
                           (jnp.full((n_rows, 1), -1.0, F32), jnp.full((n_rows, 1), 1.0, F32)))
    gt = jnp.where(aff > hi, 1.0, 0.0)
    tie = jnp.where(aff > lo, 1.0, 0.0) - gt
    need = CAP - jnp.sum(gt, axis=1, keepdims=True)
    sel = gt + tie * jnp.where(_prefix_lanes(tie, utri)[0] < need, 1.0, 0.0)
    pos, starts = _prefix_lanes(sel, utri)
    pos_o[...] = jnp.where(sel > 0.5, pos, -1.0)
    start_o[...] = jnp.concatenate([starts, jnp.zeros((n_rows, 128 - N_TOK // 128), F32)], axis=1)


def _select(logits_t, utri):
    shp = jax.ShapeDtypeStruct((2 * N_EXPERTS, N_TOK), F32)
    full = lambda a: pl.BlockSpec(a.shape, lambda i: (0,) * a.ndim)
    return pl.pallas_call(
        _select_kernel,
        grid=(1,),
        in_specs=[full(logits_t), full(utri)],
        out_specs=[pl.BlockSpec(shp.shape, lambda i: (0, 0))] * 2 + [pl.BlockSpec((2 * N_EXPERTS, 128), lambda i: (0, 0))],
        out_shape=[shp, shp, jax.ShapeDtypeStruct((2 * N_EXPERTS, 128), F32)],
        compiler_params=_cparams(1),
        name="ec_select",
    )(logits_t, utri)


def _compact_kernel(start_ref, pos_ref, aff_ref, idx_o, gate_o, idx_acc, gate_acc):
    i = pl.program_id(0)
    n_blk = N_TOK // 128
    idx_acc[...] = jnp.zeros_like(idx_acc)
    gate_acc[...] = jnp.zeros_like(gate_acc)
    rel = lax.broadcasted_iota(jnp.int32, (COMPACT_WIN, 1), 0)
    for c in range(n_blk):
        base = pl.multiple_of(lax.shift_right_logical(start_ref[i * n_blk + c], 3) * 8, 8)
        slot = (rel + base).astype(F32)
        p = pos_ref[0, :, c * 128:(c + 1) * 128]
        a = aff_ref[0, :, c * 128:(c + 1) * 128]
        tok = (lax.broadcasted_iota(jnp.int32, (1, 128), 1) + c * 128).astype(F32)
        hit = p == slot
        idx_acc[pl.ds(base, COMPACT_WIN), :] = idx_acc[pl.ds(base, COMPACT_WIN), :] + jnp.where(hit, tok, 0.0)
        gate_acc[pl.ds(base, COMPACT_WIN), :] = gate_acc[pl.ds(base, COMPACT_WIN), :] + jnp.where(hit, a, 0.0)
    idx_o[0] = jnp.broadcast_to(jnp.sum(idx_acc[0:CAP, :], axis=1, keepdims=True), (CAP, 128))
    gate_o[0] = jnp.broadcast_to(jnp.sum(gate_acc[0:CAP, :], axis=1, keepdims=True), (CAP, 128))


def _compact(block_starts, pos3, aff3):
    rows = pos3.shape[0]
    shp = jax.ShapeDtypeStruct((rows, CAP, 128), F32)
    return pl.pallas_call(
        _compact_kernel,
        grid_spec=pltpu.PrefetchScalarGridSpec(
            num_scalar_prefetch=1,
            grid=(rows,),
            in_specs=[pl.BlockSpec((1, 1, N_TOK), lambda i, st: (i, 0, 0))] * 2,
            out_specs=[pl.BlockSpec((1, CAP, 128), lambda i, st: (i, 0, 0))] * 2,
            scratch_shapes=[pltpu.VMEM((CAP + COMPACT_WIN, 128), F32)] * 2),
        out_shape=[shp, shp],
        compiler_params=_cparams(1),
        name="ec_compact",
    )(block_starts, pos3, aff3)


def _gather_kernel(idx_ref, h_ref, o_ref, tile):
    grp = pl.program_id(0)
    eb = pl.program_id(1)
    for ei in range(ROUTE_EXPERTS):
        base = (grp * N_EXPERTS + eb * ROUTE_EXPERTS + ei) * CAP

        def tok(j8, c, base=base):
            for u in range(8):
                j = j8 * 8 + u
                row = pl.multiple_of(idx_ref[base + j] * 8, 8)
                tile[pl.ds(j, 8, stride=TOK_STRIDE), :] = h_ref[0, pl.ds(row, 8), :]
            return c

        lax.fori_loop(0, CAP // 8, tok, 0)
        for c in range(TOK_ROWS):
            o_ref[ei, :, c * 128:(c + 1) * 128] = tile[c * TOK_STRIDE:c * TOK_STRIDE + CAP, :].astype(BF16)


def _gather(idx_flat, h_rows):
    return pl.pallas_call(
        _gather_kernel,
        grid_spec=pltpu.PrefetchScalarGridSpec(
            num_scalar_prefetch=1,
            grid=(2, N_EXPERTS // ROUTE_EXPERTS),
            in_specs=[pl.BlockSpec((1, N_TOK * 8, 128), lambda g, e, idx: (g, 0, 0))],
            out_specs=pl.BlockSpec((ROUTE_EXPERTS, CAP, D_MODEL), lambda g, e, idx: (e, g, 0)),
            scratch_shapes=[pltpu.VMEM((8 * TOK_STRIDE, 128), F32)]),
        out_shape=jax.ShapeDtypeStruct((N_EXPERTS, 2 * CAP, D_MODEL), BF16),
        compiler_params=_cparams(2),
        name="ec_gather",
    )(idx_flat, h_rows)


def _expert_kernel(xe_ref, w1_ref, w3_ref, w2_ref, gate_p_ref, gate_s_ref, o_ref, hid_scr):
    s = pl.program_id(1)
    subs = [(lo, min(FF_SUB, FF_SLAB - lo)) for lo in range(0, FF_SLAB, FF_SUB)]

    @pl.when(s < FF_SPLIT)
    def _():
        x = xe_ref[0]
        for lo, width in subs:
            h1 = jnp.dot(x, w1_ref[0, :, lo:lo + width].astype(BF16), preferred_element_type=F32)
            h3 = jnp.dot(x, w3_ref[0, :, lo:lo + width].astype(BF16), preferred_element_type=F32)
            hid_scr[s, :, lo:lo + width] = (h1 * _sigmoid(h1) * h3).astype(BF16)

    @pl.when(s >= FF_SPLIT)
    def _():
        acc = None
        for lo, width in subs:
            part = jnp.dot(hid_scr[s - FF_SPLIT, :, lo:lo + width], w2_ref[0, lo:lo + width, :].astype(BF16),
                           preferred_element_type=F32)
            acc = part if acc is None else acc + part

        @pl.when(s == FF_SPLIT)
        def _():
            o_ref[0] = acc

        @pl.when(s > FF_SPLIT)
        def _():
            o_ref[0] = o_ref[0] + acc

    @pl.when(s == 2 * FF_SPLIT - 1)
    def _():
        gate = jnp.concatenate([gate_p_ref[0][:, 0:1], gate_s_ref[0][:, 0:1]], axis=0)
        o_ref[0] = o_ref[0] * gate


def _experts(xe, w1, w3, w2, gate_rows):
    m = xe.shape[1]
    up = lambda e, s: (e, 0, jnp.minimum(s, FF_SPLIT - 1))
    down = lambda e, s: (e, jnp.maximum(s - FF_SPLIT, 0), 0)
    return pl.pallas_call(
        _expert_kernel,
        grid=(N_EXPERTS, 2 * FF_SPLIT),
        in_specs=[pl.BlockSpec((1, m, D_MODEL), lambda e, s: (e, 0, 0)),
                  pl.BlockSpec((1, D_MODEL, FF_SLAB), up),
                  pl.BlockSpec((1, D_MODEL, FF_SLAB), up),
                  pl.BlockSpec((1, FF_SLAB, D_MODEL), down),
                  pl.BlockSpec((1, CAP, 128), lambda e, s: (e, 0, 0)),
                  pl.BlockSpec((1, CAP, 128), lambda e, s: (N_EXPERTS + e, 0, 0))],
        out_specs=pl.BlockSpec((1, m, D_MODEL), lambda e, s: (e, 0, 0)),
        out_shape=jax.ShapeDtypeStruct((N_EXPERTS, m, D_MODEL), F32),
        scratch_shapes=[pltpu.VMEM((FF_SPLIT, m, FF_SLAB), BF16)],
        compiler_params=_cparams(2, vmem=EXPERT_VMEM_LIMIT),
        name="ec_experts",
    )(xe, w1, w3, w2, gate_rows, gate_rows)


def _combine_kernel(idx_ref, ye_ref, x1_ref, g2_ref, fg_ref, op_ref, os_ref, acc, tile):
    grp = pl.program_id(0)
    s = pl.program_id(1)
    n_scatter = N_EXPERTS // ROUTE_EXPERTS

    @pl.when(s == 0)
    def _():
        acc[...] = jnp.zeros_like(acc)

    @pl.when(s < n_scatter)
    def _():
        for ei in range(ROUTE_EXPERTS):
            base = (grp * N_EXPERTS + s * ROUTE_EXPERTS + ei) * CAP
            for c in range(TOK_ROWS):
                tile[c * TOK_STRIDE:c * TOK_STRIDE + CAP, :] = ye_ref[ei, :, c * 128:(c + 1) * 128]

            def tok(j4, c, base=base):
                rows, vals = [], []
                for u in range(4):
                    j = j4 * 4 + u
                    row = pl.multiple_of(idx_ref[base + j] * 8, 8)
                    rows.append(row)
                    vals.append(acc[pl.ds(row, 8), :] + tile[pl.ds(j, 8, stride=TOK_STRIDE), :])
                for row, val in zip(rows, vals):
                    acc[pl.ds(row, 8), :] = val
                return c

            lax.fori_loop(0, CAP // 4, tok, 0)

    @pl.when(s >= n_scatter)
    def _():
        row0 = (s - n_scatter) * (ROW_TILE * TOK_ROWS)
        y = jnp.concatenate([acc[pl.ds(row0 + c, ROW_TILE, stride=TOK_ROWS), :] for c in range(TOK_ROWS)], axis=1)
        x = x1_ref[...] + g2_ref[0] * y
        ms = jnp.mean(x * x, axis=-1, keepdims=True)
        out = x * lax.rsqrt(ms + RMS_EPS) * fg_ref[...]

        @pl.when(grp == 0)
        def _():
            op_ref[...] = out

        @pl.when(grp == 1)
        def _():
            os_ref[...] = out


def _combine_final(idx_flat, ye, x1, mod3, final_g):
    n_scatter = N_EXPERTS // ROUTE_EXPERTS
    tiles = N_TOK // ROW_TILE
    tile_of = lambda g, s: g * tiles + jnp.maximum(s - n_scatter, 0)
    shp = jax.ShapeDtypeStruct((N_TOK, D_MODEL), F32)
    return pl.pallas_call(
        _combine_kernel,
        grid_spec=pltpu.PrefetchScalarGridSpec(
            num_scalar_prefetch=1,
            grid=(2, n_scatter + tiles),
            in_specs=[pl.BlockSpec((ROUTE_EXPERTS, CAP, D_MODEL), lambda g, s, idx: (jnp.minimum(s, n_scatter - 1), g, 0)),
                      pl.BlockSpec((ROW_TILE, D_MODEL), lambda g, s, idx: (tile_of(g, s), 0)),
                      pl.BlockSpec((1, 1, D_MODEL), lambda g, s, idx: (_cond_index(tile_of(g, s)), 0, 5)),
                      pl.BlockSpec((1, D_MODEL), lambda g, s, idx: (0, 0))],
            out_specs=[pl.BlockSpec((ROW_TILE, D_MODEL),
                                    lambda g, s, idx: (jnp.where(g == 0, jnp.maximum(s - n_scatter, 0), tiles - 1), 0)),
                       pl.BlockSpec((ROW_TILE, D_MODEL),
                                    lambda g, s, idx: (jnp.where(g == 1, jnp.maximum(s - n_scatter, 0), 0), 0))],
            scratch_shapes=[pltpu.VMEM((N_TOK * TOK_ROWS, 128), F32), pltpu.VMEM((TOK_ROWS * TOK_STRIDE, 128), F32)]),
        out_shape=[shp, shp],
        compiler_params=_cparams(2),
        name="ec_combine_final",
    )(idx_flat, ye, x1, mod3, final_g)


def kernel(x_prompt, x_sample, state_wkv, c, c_ctx, norm1_g, norm2_g, w_mod, b_mod, w_in, conv_dw, conv_b, conv_ln_g,
           conv_ln_b, rwkv_w0, rwkv_w_up, rwkv_a0, rwkv_a_up, rwkv_g_up, rwkv_k_k, rwkv_k_a, rwkv_r_k, rwkv_gn_g,
           rwkv_gn_b, w_out, router, exp_w1, exp_w3, exp_w2, final_norm_g):
    row2 = lambda a: a.reshape(1, -1)
    x = (x_prompt.reshape(N_TOK, D_MODEL), x_sample.reshape(N_TOK, D_MODEL))
    cond8 = jnp.concatenate([c_ctx[None, :], c, jnp.zeros((8 - 1 - DEC_BATCH, D_MODEL), F32)], axis=0)

    mod = _modulation(cond8, w_mod[0], b_mod)
    mod3 = mod.reshape(8, 1, 6 * D_MODEL)

    lane_head = jnp.arange(GROUP_W, dtype=jnp.int32) // HEAD
    ones_blk = (lane_head[:, None] == lane_head[None, :]).astype(BF16)

    def per_dir_rows(up):
        z = jnp.zeros_like(up[0])
        return jnp.stack([jnp.concatenate([up[0], z], axis=0), jnp.concatenate([z, up[1]], axis=0)]).astype(BF16)
    conv, rkv, kap, w0, w1, b0, b1, kd0, kd1, g, bonus = _mixer_in(
        *x, mod3, norm1_g, w_in[0].astype(BF16), (conv_dw[0], conv_b, conv_ln_g, conv_ln_b),
        (ones_blk, rwkv_w0[0], per_dir_rows(rwkv_w_up[0]), rwkv_a0[0], per_dir_rows(rwkv_a_up[0]),
         rwkv_g_up[0].astype(BF16), rwkv_k_k, rwkv_k_a, row2(rwkv_r_k[0])))

    dir_ops = ((w0, b0, kd0), (w1, b1, kd1))
    yp_f, yp_b, new_state = _chunk_scan(rkv, kap, dir_ops, None, row0=0, n_batch=BATCH, t_len=SEQ, want_final=True)
    ys_f, ys_b = _chunk_scan(rkv, kap, dir_ops, state_wkv[:, 0], row0=N_TOK, n_batch=DEC_BATCH, t_len=DEC_SEQ,
                             want_final=False)

    r_pad = jnp.pad(router[0], ((0, 0), (0, 128 - N_EXPERTS)))
    r_hi = r_pad.astype(BF16)
    r_lo = (r_pad - r_hi.astype(F32)).astype(BF16)
    x1, h2_rows, logits = _postmix((yp_f, ys_f), (yp_b, ys_b), x, conv, bonus, g, mod3, rwkv_gn_g, rwkv_gn_b, ones_blk,
                                   w_out[0].astype(BF16), norm2_g, r_hi, r_lo)

    logits_t = logits[:, :N_EXPERTS].reshape(2, N_TOK, N_EXPERTS).transpose(0, 2, 1).reshape(2 * N_EXPERTS, N_TOK)
    lane = jnp.arange(128, dtype=jnp.int32)
    utri = (lane[:, None] < lane[None, :]).astype(BF16)
    pos, aff, starts = _select(logits_t, utri)
    block_starts = starts[:, :N_TOK // 128].astype(jnp.int32).reshape(-1)
    idx_c, gate_c = _compact(block_starts, pos.reshape(2 * N_EXPERTS, 1, N_TOK), aff.reshape(2 * N_EXPERTS, 1, N_TOK))
    idx_flat = idx_c[:, :, 0].astype(jnp.int32).reshape(-1)

    xe = _gather(idx_flat, h2_rows.reshape(2, N_TOK * TOK_ROWS, 128))
    ye = _experts(xe, exp_w1[0], exp_w3[0], exp_w2[0], gate_c)
    out_p, out_s = _combine_final(idx_flat, ye, x1, mod3, row2(final_norm_g))
    return (out_p.reshape(BATCH, SEQ, D_MODEL), out_s.reshape(DEC_BATCH, DEC_SEQ, D_MODEL),
            new_state.reshape(BATCH, 1, 2, N_HEADS, HEAD, HEAD))
```

```python
import functools

import jax
import jax.numpy as jnp
from jax import lax
from jax.experimental import pallas as pl
from jax.experimental.pallas import tpu as pltpu

F32 = jnp.float32
BF16 = jnp.bfloat16

D_MODEL = 1024
BATCH = 16
SEQ = 256
DEC_BATCH = 4
DEC_SEQ = 1024
GRID_W = 64
CONV_W = 512
CONV_K = 31
CONV_PAD = 15
CONV_GAP = 16
RWKV_W = 512
HEAD = 64
N_HEADS = 8
LORA = 64
GATE_LORA = 128
N_EXPERTS = 16
D_FF = 2816
D_IN = 2944
CAP = 512
N_TOK = 4096
RMS_EPS = 1e-6
LN_EPS = 1e-5
GN_EPS = 64e-5

ROW_TILE = 512
FF_SPLIT = 2
FF_SLAB = D_FF // FF_SPLIT
FF_SUB = 256
EXPERT_VMEM_LIMIT = 60 * 1024 * 1024
CHUNK = 64
HEADS_PER_GROUP = 4
GROUP_W = HEADS_PER_GROUP * HEAD
PASSES_A = 3
PASSES_INV = 1
PASSES_STATE = 1
SCAN_BATCHES = 4
TOK_ROWS = D_MODEL // 128
TOK_STRIDE = CAP + 8
ROUTE_EXPERTS = 4
SCATTER_UNROLL = 8
COMPACT_WIN = 128 + 8
BISECT_ITERS = 160
VMEM_LIMIT = 56 * 1024 * 1024


def _cparams(n_axes, vmem=None):
    return pltpu.CompilerParams(dimension_semantics=("arbitrary",) * n_axes,
                                vmem_limit_bytes=vmem or VMEM_LIMIT)


def _cond_index(i):
    n_prompt = (BATCH * SEQ) // ROW_TILE
    per_batch = DEC_SEQ // ROW_TILE
    return jnp.where(i < n_prompt, 0, 1 + (i - n_prompt) // per_batch)


def _split_dot(x, w_bf16):
    hi = x.astype(BF16)
    lo = (x - hi.astype(F32)).astype(BF16)
    return (jnp.dot(hi, w_bf16, preferred_element_type=F32)
            + jnp.dot(lo, w_bf16, preferred_element_type=F32))


def _head_sums(x, ones_bf16):
    return jnp.concatenate([_split_dot(x[:, c:c + GROUP_W], ones_bf16) for c in range(0, x.shape[1], GROUP_W)],
                           axis=1)


def _sigmoid(x):
    return 1.0 / (1.0 + jnp.exp(-x))


def _mod_kernel(c_ref, w_ref, b_ref, o_ref):
    c = c_ref[...]
    s = (c * _sigmoid(c)).astype(BF16)
    o_ref[...] = jnp.dot(s, w_ref[...].astype(BF16), preferred_element_type=F32) + b_ref[...]


def _modulation(cond8, w_mod, b_mod):
    n = 6 * D_MODEL
    tn = D_MODEL
    return pl.pallas_call(
        _mod_kernel,
        grid=(n // tn,),
        in_specs=[pl.BlockSpec((8, D_MODEL), lambda j: (0, 0)),
                  pl.BlockSpec((D_MODEL, tn), lambda j: (0, j)),
                  pl.BlockSpec((1, tn), lambda j: (0, j))],
        out_specs=pl.BlockSpec((8, tn), lambda j: (0, j)),
        out_shape=jax.ShapeDtypeStruct((8, n), F32),
        compiler_params=_cparams(1),
        name="modulation",
    )(cond8, w_mod, b_mod)


def _group_tile(n_prompt_tiles):
    return (lambda i: (jnp.minimum(i, n_prompt_tiles - 1), 0)), (lambda i: (jnp.maximum(i - n_prompt_tiles, 0), 0))


def _pick(prompt_ref, sample_ref, n_prompt_tiles):
    return jnp.where(pl.program_id(0) < n_prompt_tiles, prompt_ref[...], sample_ref[...])


def _mixer_in_kernel(xp_ref, xs_ref, sh_ref, sc_ref, n1g_ref, w_ref, dw_ref, db_ref, lng_ref, lnb_ref, ones_ref,
                     w0_ref, wup_ref, a0_ref, aup_ref, gup_ref, kk_ref, ka_ref, rk_ref,
                     conv_o, rkv_o, kap_o, lw0_o, lw1_o, b0_o, b1_o, kd0_o, kd1_o, g_o, bonus_o):
    n_prompt = N_TOK // ROW_TILE
    x = _pick(xp_ref, xs_ref, n_prompt)
    ms = jnp.mean(x * x, axis=-1, keepdims=True)
    h = x * lax.rsqrt(ms + RMS_EPS) * n1g_ref[...]
    h = (h * (1.0 + sc_ref[0]) + sh_ref[0]).astype(BF16)
    uv = jnp.dot(h, w_ref[:, 0:CONV_W], preferred_element_type=F32)
    ug = jnp.dot(h, w_ref[:, CONV_W:2 * CONV_W], preferred_element_type=F32)
    u = uv * _sigmoid(ug)

    @pl.when(pl.program_id(0) < n_prompt)
    def _():
        conv_o[...] = _conv_ln_silu(u, SEQ, dw_ref, db_ref, lng_ref, lnb_ref)

    @pl.when(pl.program_id(0) >= n_prompt)
    def _():
        conv_o[...] = _conv_ln_silu(u, GRID_W, dw_ref, db_ref, lng_ref, lnb_ref)

    rkv = jnp.dot(h, w_ref[:, 2 * CONV_W:2 * CONV_W + 3 * RWKV_W], preferred_element_type=F32)
    rkv_o[...] = rkv
    lora = jnp.dot(h, w_ref[:, 2 * CONV_W + 3 * RWKV_W:D_IN], preferred_element_type=F32)
    _rwkv_operands(rkv, lora, ones_ref, w0_ref, wup_ref, a0_ref, aup_ref, gup_ref, kk_ref, ka_ref, rk_ref,
                   kap_o, (lw0_o, lw1_o), (b0_o, b1_o), (kd0_o, kd1_o), g_o, bonus_o)


def _mixer_in(x_prompt, x_sample, mod3, norm1_g, w_in_bf16, conv_params, rwkv_params):
    n = 2 * N_TOK
    first, second = _group_tile(N_TOK // ROW_TILE)
    row = lambda c: pl.BlockSpec((ROW_TILE, c), lambda i: (i, 0))
    full = lambda a: pl.BlockSpec(a.shape, lambda i: (0,) * a.ndim)
    consts = (norm1_g, w_in_bf16) + tuple(conv_params) + tuple(rwkv_params)
    widths = [CONV_W, 3 * RWKV_W] + [RWKV_W] * 9
    return pl.pallas_call(
        _mixer_in_kernel,
        grid=(n // ROW_TILE,),
        in_specs=[pl.BlockSpec((ROW_TILE, D_MODEL), first),
                  pl.BlockSpec((ROW_TILE, D_MODEL), second),
                  pl.BlockSpec((1, 1, D_MODEL), lambda i: (_cond_index(i), 0, 0)),
                  pl.BlockSpec((1, 1, D_MODEL), lambda i: (_cond_index(i), 0, 1)),
                  ] + [full(a) for a in consts],
        out_specs=[row(c) for c in widths],
        out_shape=[jax.ShapeDtypeStruct((n, c), F32) for c in widths],
        compiler_params=_cparams(1),
        name="mixer_in",
    )(x_prompt, x_sample, mod3, mod3, *consts)


def _conv_ln_silu(u, seg, dw_ref, db_ref, g_ref, b_ref):
    n_seg = u.shape[0] // seg
    pitch = seg + CONV_GAP
    gap = jnp.zeros((CONV_GAP, u.shape[1]), F32)
    up = jnp.concatenate([piece for i in range(n_seg) for piece in (u[i * seg:(i + 1) * seg], gap)], axis=0)
    acc = up * dw_ref[CONV_PAD:CONV_PAD + 1, :]
    for k in range(CONV_K):
        s = k - CONV_PAD
        if s != 0:
            acc = acc + pltpu.roll(up, (-s) % up.shape[0], 0) * dw_ref[k:k + 1, :]
    out = jnp.concatenate([acc[i * pitch:i * pitch + seg] for i in range(n_seg)], axis=0) + db_ref[...]
    mu = jnp.mean(out, axis=-1, keepdims=True)
    d = out - mu
    var = jnp.mean(d * d, axis=-1, keepdims=True)
    out = d * lax.rsqrt(var + LN_EPS) * g_ref[...] + b_ref[...]
    return out * _sigmoid(out)


def _rwkv_operands(rkv, lora, ones_ref, w0_ref, wup_ref, a0_ref, aup_ref, gup_ref, kk_ref, ka_ref, rk_ref,
                   kap_o, lw_o, b_o, kd_o, g_o, bonus_o):
    r = rkv[:, 0:RWKV_W]
    k = rkv[:, RWKV_W:2 * RWKV_W]
    v = rkv[:, 2 * RWKV_W:3 * RWKV_W]
    xw = jnp.tanh(lora[:, 0:2 * LORA]).astype(BF16)
    xa = lora[:, 2 * LORA:4 * LORA].astype(BF16)
    xg = _sigmoid(lora[:, 4 * LORA:4 * LORA + GATE_LORA]).astype(BF16)
    ones = ones_ref[...]

    kk0 = k * kk_ref[...]
    ss = _head_sums(kk0 * kk0, ones)
    kap = kk0 * lax.rsqrt(jnp.maximum(ss, 1e-24))
    kap_o[...] = kap
    g_o[...] = jnp.dot(xg, gup_ref[...], preferred_element_type=F32)

    kd_sum = jnp.zeros_like(k)
    for d in range(2):
        wl = w0_ref[d:d + 1, :] + jnp.dot(xw, wup_ref[d], preferred_element_type=F32)
        lw_o[d][...] = -jnp.exp(-0.5) * _sigmoid(wl)
        a = _sigmoid(a0_ref[d:d + 1, :] + jnp.dot(xa, aup_ref[d], preferred_element_type=F32))
        b_o[d][...] = kap * a
        kd = k * (1.0 + (a - 1.0) * ka_ref[...])
        kd_o[d][...] = kd
        kd_sum = kd_sum + kd
    bonus_o[...] = _head_sums(r * kd_sum * rk_ref[...], ones) * v


def _pieces(x, passes):
    hi = x.astype(BF16)
    return (hi,) if passes == 1 else (hi, (x - hi.astype(F32)).astype(BF16))


_NN = ((1,), (0,))
_NT = ((1,), (1,))
_TN = ((0,), (0,))


def _mm(a, b, dims):
    dg = lambda p, q: lax.dot_general(p, q, (dims, ((), ())), preferred_element_type=F32)
    if len(a) == 1:
        out = dg(a[0], b[0])
    else:
        free = 1 - dims[0][0]
        both = dg(jnp.concatenate(a, axis=free), b[0])
        m = a[0].shape[free]
        out = both[:m] + both[m:]
    if len(b) > 1:
        out = out + dg(a[0], b[1])
    return out


def _block_diag(pieces, head_masks):
    return tuple(jnp.concatenate([p * m for m in head_masks], axis=0) for p in pieces)


def _chunk_masks(reverse):
    row = lax.broadcasted_iota(jnp.int32, (CHUNK, GROUP_W), 0)
    col = lax.broadcasted_iota(jnp.int32, (CHUNK, GROUP_W), 1) % CHUNK
    strict = (col > row) if reverse else (col < row)
    incl = (col >= row) if reverse else (col <= row)
    eye_side = jnp.where(col == row, 1.0, 0.0)
    tri = jnp.where(incl[:, :CHUNK], 1.0, 0.0).astype(BF16)
    return strict, incl, eye_side, tri, (0 if reverse else CHUNK - 1)


def _chunk_steps(chains):
    ln = CHUNK
    each = lambda f, *cols: [f(*args) for args in zip(*cols)]
    r, kap, v, lw, b, kd, n, masks = (list(col) for col in zip(*chains))
    strict = [m[0] for m in masks]
    incl = [m[1] for m in masks]
    lane_blk = lax.broadcasted_iota(jnp.int32, (ln, GROUP_W), 1) // HEAD
    head_masks = [jnp.where(lane_blk == h, 1.0, 0.0).astype(BF16) for h in range(HEADS_PER_GROUP)]
    bd = lambda x, passes: _block_diag(_pieces(x, passes), head_masks)
    r2 = lax.broadcasted_iota(jnp.int32, (GROUP_W, GROUP_W), 0) // HEAD
    c2 = lax.broadcasted_iota(jnp.int32, (GROUP_W, GROUP_W), 1) // HEAD
    same_head = r2 == c2

    def running_sum(lw_c, m):
        return _mm((m[3],), _pieces(lw_c, 3), _NN)

    cw = each(running_sum, lw, masks)
    tot = each(lambda cw_c, m: cw_c[m[4]:m[4] + 1, :], cw, masks)
    e_neg = each(lambda cw_c: jnp.exp(-cw_c), cw)
    kr = each(lambda kap_c, r_c, cw_c, lw_c: _pieces(
        jnp.concatenate([kap_c * jnp.exp(cw_c - lw_c), r_c * jnp.exp(cw_c)], axis=0), PASSES_A),
        kap, r, cw, lw)
    ab = each(lambda kr_c, b_c, e: _mm(kr_c, bd(b_c * e, PASSES_A), _NT), kr, b, e_neg)
    ak = each(lambda kr_c, kd_c, e: _mm(kr_c, bd(kd_c * e, PASSES_A), _NT), kr, kd, e_neg)

    q = each(lambda ab_c, s: jnp.where(s, -ab_c[:ln], 0.0), ab, strict)
    x = each(lambda q_c, m: m[2] + q_c, q, masks)
    q = each(lambda q_c: _mm(_pieces(q_c, PASSES_INV), bd(q_c, PASSES_INV), _NN), q)
    n_rounds = CHUNK.bit_length() - 2
    for j in range(1, n_rounds + 1):
        last_round = j == n_rounds
        lhs = x if last_round else each(lambda q_c, x_c: jnp.concatenate([q_c, x_c], axis=0), q, x)
        prod = each(lambda l_c, q_c: _mm(_pieces(l_c, PASSES_INV), bd(q_c, PASSES_INV), _NN), lhs, q)
        if last_round:
            x = each(lambda x_c, p_c: x_c + p_c, x, prod)
        else:
            q = each(lambda p_c: p_c[:ln], prod)
            x = each(lambda x_c, p_c: x_c + p_c[ln:], x, prod)

    a3 = each(lambda ab_c, i: jnp.where(i, ab_c[ln:], 0.0), ab, incl)
    a24 = each(lambda ak_c, s, i: jnp.concatenate([jnp.where(s, ak_c[:ln], 0.0), jnp.where(i, ak_c[ln:], 0.0)], axis=0),
               ak, strict, incl)
    av = each(lambda a_c, v_c: _mm(_pieces(a_c, PASSES_A), bd(v_c, PASSES_A), _NN), a24, v)
    krn = each(lambda kr_c, n_c: _mm(kr_c[:PASSES_STATE], _pieces(n_c, PASSES_STATE), _NT), kr, n)
    u = each(lambda x_c, krn_c, av_c: -_mm(_pieces(x_c, PASSES_INV), bd(krn_c[:ln] + av_c[:ln], PASSES_INV), _NN),
             x, krn, av)
    y = each(lambda krn_c, a3_c, u_c, av_c: krn_c[ln:] + _mm(_pieces(a3_c, PASSES_STATE), bd(u_c, PASSES_STATE), _NN)
             + av_c[ln:], krn, a3, u, av)
    e_rest = each(lambda tot_c, cw_c: jnp.exp(tot_c - cw_c), tot, cw)
    upd = each(lambda u_c, v_c, b_c, kd_c, e: _mm(_pieces(jnp.concatenate([u_c, v_c], axis=0), PASSES_A),
                                                   _pieces(jnp.concatenate([b_c * e, kd_c * e], axis=0), PASSES_A), _TN),
               u, v, b, kd, e_rest)
    n_new = each(lambda n_c, tot_c, upd_c: n_c * jnp.exp(tot_c) + jnp.where(same_head, upd_c, 0.0), n, tot, upd)
    return list(zip(y, n_new))


def _chunk_kernel(*refs, has_init, has_final):
    n_tok = 12 * SCAN_BATCHES
    tok_refs = [refs[k * 12:(k + 1) * 12] for k in range(SCAN_BATCHES)]
    refs = refs[n_tok:]
    s0 = refs[0] if has_init else None
    refs = refs[1:] if has_init else refs
    y_refs = refs[:2]
    sfin = refs[2] if has_final else None
    n_scr = refs[-1]
    c = pl.program_id(1)
    head_blocks = [(k, d, g, h) for k in range(SCAN_BATCHES) for d in range(2)
                   for g in range(N_HEADS // HEADS_PER_GROUP) for h in range(HEADS_PER_GROUP)]
    diag = lambda h: slice(h * HEAD, (h + 1) * HEAD)

    @pl.when(c == 0)
    def _():
        n_scr[...] = jnp.zeros(n_scr.shape, F32)
        if has_init:
            for k, d, g, h in head_blocks:
                n_scr[d, k, g, diag(h), diag(h)] = s0[k, d, g * HEADS_PER_GROUP + h]

    slots = [(k, d, g) for k in range(SCAN_BATCHES) for d in range(2) for g in range(N_HEADS // HEADS_PER_GROUP)]
    lanes = lambda g: slice(g * GROUP_W, (g + 1) * GROUP_W)
    masks = [_chunk_masks(reverse=False), _chunk_masks(reverse=True)]

    def operands(k, d, g):
        rf, kf, vf, rb, kb, vb, lwf, bf, kdf, lwb, bb, kdb = tok_refs[k]
        group = (rb, kb, vb, lwb, bb, kdb) if d else (rf, kf, vf, lwf, bf, kdf)
        return tuple(ref[:, lanes(g)] for ref in group)

    chains = [operands(k, d, g) + (n_scr[d, k, g], masks[d]) for k, d, g in slots]
    for (k, d, g), (y, n_new) in zip(slots, _chunk_steps(chains)):
        y_refs[d][k, :, lanes(g)] = y
        n_scr[d, k, g] = n_new

    if has_final:
        @pl.when(c == pl.num_programs(1) - 1)
        def _():
            for k, d, g, h in head_blocks:
                sfin[k, d, g * HEADS_PER_GROUP + h] = n_scr[d, k, g, diag(h), diag(h)]


def _chunk_scan(rkv, kap, dir_ops, s0, *, row0, n_batch, t_len, want_final):
    n_chunks = t_len // CHUNK
    blk0 = row0 // CHUNK
    n_grp = N_HEADS // HEADS_PER_GROUP
    nb = SCAN_BATCHES

    def tok_spec(k, reverse, col=0):
        def index(bi, ci):
            return (blk0 + (bi * nb + k) * n_chunks + (n_chunks - 1 - ci if reverse else ci), col)
        return pl.BlockSpec((CHUNK, RWKV_W), index)

    in_specs, args = [], []
    for k in range(nb):
        for rev in (False, True):
            in_specs += [tok_spec(k, rev, 0), tok_spec(k, rev), tok_spec(k, rev, 2)]
        in_specs += [tok_spec(k, False)] * 3 + [tok_spec(k, True)] * 3
        args += [rkv, kap, rkv, rkv, kap, rkv, *dir_ops[0], *dir_ops[1]]
    state = pl.BlockSpec((nb, 2, N_HEADS, HEAD, HEAD), lambda bi, ci: (bi, 0, 0, 0, 0))
    has_init = s0 is not None
    if has_init:
        in_specs.append(state)
        args.append(s0)
    y_shape = jax.ShapeDtypeStruct((n_batch, t_len, RWKV_W), F32)
    out_specs = [pl.BlockSpec((nb, CHUNK, RWKV_W), lambda bi, ci: (bi, ci, 0)),
                 pl.BlockSpec((nb, CHUNK, RWKV_W), lambda bi, ci: (bi, n_chunks - 1 - ci, 0))]
    out_shape = [y_shape, y_shape]
    if want_final:
        out_specs.append(state)
        out_shape.append(jax.ShapeDtypeStruct((n_batch, 2, N_HEADS, HEAD, HEAD), F32))
    outs = pl.pallas_call(
        functools.partial(_chunk_kernel, has_init=has_init, has_final=want_final),
        grid=(n_batch // nb, n_chunks),
        in_specs=in_specs,
        out_specs=out_specs,
        out_shape=out_shape,
        scratch_shapes=[pltpu.VMEM((2, nb, n_grp, GROUP_W, GROUP_W), F32)],
        compiler_params=_cparams(2),
        name=f"wkv_chunk_t{t_len}",
    )(*args)
    return [outs[0].reshape(n_batch * t_len, RWKV_W), outs[1].reshape(n_batch * t_len, RWKV_W)] + list(outs[2:])


def _postmix_kernel(yfp_ref, yfs_ref, ybp_ref, ybs_ref, xp_ref, xs_ref, conv_ref, bonus_ref, g_ref,
                    g1_ref, sh2_ref, sc2_ref, gng_ref, gnb_ref, ones_ref, wout_ref, n2g_ref, rhi_ref, rlo_ref,
                    x1_o, h2_o, logit_o):
    n_prompt = N_TOK // ROW_TILE
    ones = ones_ref[...]
    y = _pick(yfp_ref, yfs_ref, n_prompt) + _pick(ybp_ref, ybs_ref, n_prompt)
    mu = _head_sums(y, ones) * (1.0 / HEAD)
    d = y - mu
    var = _head_sums(d * d, ones) * (1.0 / HEAD)
    yn = d * lax.rsqrt(var + GN_EPS) * gng_ref[...] + gnb_ref[...]
    rw = ((yn + bonus_ref[...]) * g_ref[...]).astype(BF16)
    mix = (jnp.dot(conv_ref[...].astype(BF16), wout_ref[0:CONV_W, :], preferred_element_type=F32)
           + jnp.dot(rw, wout_ref[CONV_W:CONV_W + RWKV_W, :], preferred_element_type=F32))
    x1 = _pick(xp_ref, xs_ref, n_prompt) + g1_ref[0] * mix
    x1_o[...] = x1
    ms = jnp.mean(x1 * x1, axis=-1, keepdims=True)
    h2 = x1 * lax.rsqrt(ms + RMS_EPS) * n2g_ref[...]
    h2 = h2 * (1.0 + sc2_ref[0]) + sh2_ref[0]
    for c in range(TOK_ROWS):
        h2_o[pl.ds(c, ROW_TILE, stride=TOK_ROWS), :] = h2[:, c * 128:(c + 1) * 128]
    hi = h2.astype(BF16)
    lo = (h2 - hi.astype(F32)).astype(BF16)
    logit_o[...] = (jnp.dot(hi, rhi_ref[...], preferred_element_type=F32)
                    + jnp.dot(lo, rhi_ref[...], preferred_element_type=F32)
                    + jnp.dot(hi, rlo_ref[...], preferred_element_type=F32))


def _postmix(yf, yb, x, conv, bonus, g, mod3, gn_g, gn_b, ones_blk, w_out_bf16, norm2_g, r_hi, r_lo):
    n = 2 * N_TOK
    row = lambda c: pl.BlockSpec((ROW_TILE, c), lambda i: (i, 0))
    pair = lambda c: [pl.BlockSpec((ROW_TILE, c), m) for m in _group_tile(N_TOK // ROW_TILE)]
    full = lambda a: pl.BlockSpec(a.shape, lambda i: (0,) * a.ndim)
    modspec = lambda j: pl.BlockSpec((1, 1, D_MODEL), lambda i: (_cond_index(i), 0, j))
    return pl.pallas_call(
        _postmix_kernel,
        grid=(n // ROW_TILE,),
        in_specs=pair(RWKV_W) + pair(RWKV_W) + pair(D_MODEL) + [
            row(CONV_W), row(RWKV_W), row(RWKV_W), modspec(2), modspec(3), modspec(4), full(gn_g), full(gn_b), full(ones_blk),
            full(w_out_bf16), full(norm2_g), full(r_hi), full(r_lo)],
        out_specs=[row(D_MODEL), pl.BlockSpec((ROW_TILE * TOK_ROWS, 128), lambda i: (i, 0)), row(128)],
        out_shape=[jax.ShapeDtypeStruct((n, D_MODEL), F32), jax.ShapeDtypeStruct((n * TOK_ROWS, 128), F32),
                   jax.ShapeDtypeStruct((n, 128), F32)],
        compiler_params=_cparams(1),
        name="postmix",
    )(*yf, *yb, *x, conv, bonus, g, mod3, mod3, mod3, gn_g, gn_b, ones_blk, w_out_bf16, norm2_g, r_hi, r_lo)


def _prefix_lanes(m, utri):
    rows = m.shape[0]
    off = jnp.zeros((rows, 1), F32)
    out, starts = [], []
    for c in range(N_TOK // 128):
        blk = m[:, c * 128:(c + 1) * 128]
        out.append(jnp.dot(blk.astype(BF16), utri, preferred_element_type=F32) + off)
        starts.append(off)
        off = off + jnp.sum(blk, axis=1, keepdims=True)
    return jnp.concatenate(out, axis=1), jnp.concatenate(starts, axis=1)


def _select_kernel(lt_ref, utri_ref, pos_o, aff_o, start_o):
    utri = utri_ref[...]
    affs = []
    for grp in range(2):
        x = lt_ref[grp * N_EXPERTS:(grp + 1) * N_EXPERTS, :]
        e = jnp.exp(x - jnp.max(x, axis=0, keepdims=True))
        affs.append(e / jnp.sum(e, axis=0, keepdims=True))
    aff = jnp.concatenate(affs, axis=0)
    aff_o[...] = aff
    n_rows = 2 * N_EXPERTS

    def body(_, lh):
        lo, hi = lh
        m2 = 0.5 * (lo + hi)
        m1 = 0.5 * (lo + m2)
        m3 = 0.5 * (m2 + hi)
        ge1, ge2, ge3 = (jnp.sum(jnp.where(aff > m, 1.0, 0.0), axis=1, keepdims=True) >= CAP for m in (m1, m2, m3))
        lo = jnp.where(ge3, m3, jnp.where(ge2, m2, jnp.where(ge1, m1, lo)))
        hi = jnp.where(ge3, hi, jnp.where(ge2, m3, jnp.where(ge1, m2, m1)))
        return lo, hi

    lo, hi = lax.fori_loop(0, BISECT_ITERS // 2, body,
                           (jnp.full((n_rows, 1), -1.0, F32), jnp.full((n_rows, 1), 1.0, F32)))
    gt = jnp.where(aff > hi, 1.0, 0.0)
    tie = jnp.where(aff > lo, 1.0, 0.0) - gt
    need = CAP - jnp.sum(gt, axis=1, keepdims=True)
    sel = gt + tie * jnp.where(_prefix_lanes(tie, utri)[0] < need, 1.0, 0.0)
    pos, starts = _prefix_lanes(sel, utri)
    pos_o[...] = jnp.where(sel > 0.5, pos, -1.0)
    start_o[...] = jnp.concatenate([starts, jnp.zeros((n_rows, 128 - N_TOK // 128), F32)], axis=1)


def _select(logits_t, utri):
    shp = jax.ShapeDtypeStruct((2 * N_EXPERTS, N_TOK), F32)
    full = lambda a: pl.BlockSpec(a.shape, lambda i: (0,) * a.ndim)
    return pl.pallas_call(
        _select_kernel,
        grid=(1,),
        in_specs=[full(logits_t), full(utri)],
        out_specs=[pl.BlockSpec(shp.shape, lambda i: (0, 0))] * 2 + [pl.BlockSpec((2 * N_EXPERTS, 128), lambda i: (0, 0))],
        out_shape=[shp, shp, jax.ShapeDtypeStruct((2 * N_EXPERTS, 128), F32)],
        compiler_params=_cparams(1),
        name="ec_select",
    )(logits_t, utri)


def _compact_kernel(start_ref, pos_ref, aff_ref, idx_o, gate_o, idx_acc, gate_acc):
    i = pl.program_id(0)
    n_blk = N_TOK // 128
    idx_acc[...] = jnp.zeros_like(idx_acc)
    gate_acc[...] = jnp.zeros_like(gate_acc)
    rel = lax.broadcasted_iota(jnp.int32, (COMPACT_WIN, 1), 0)
    for c in range(n_blk):
        base = pl.multiple_of(lax.shift_right_logical(start_ref[i * n_blk + c], 3) * 8, 8)
        slot = (rel + base).astype(F32)
        p = pos_ref[0, :, c * 128:(c + 1) * 128]
        a = aff_ref[0, :, c * 128:(c + 1) * 128]
        tok = (lax.broadcasted_iota(jnp.int32, (1, 128), 1) + c * 128).astype(F32)
        hit = p == slot
        idx_acc[pl.ds(base, COMPACT_WIN), :] = idx_acc[pl.ds(base, COMPACT_WIN), :] + jnp.where(hit, tok, 0.0)
        gate_acc[pl.ds(base, COMPACT_WIN), :] = gate_acc[pl.ds(base, COMPACT_WIN), :] + jnp.where(hit, a, 0.0)
    idx_o[0] = jnp.broadcast_to(jnp.sum(idx_acc[0:CAP, :], axis=1, keepdims=True), (CAP, 128))
    gate_o[0] = jnp.broadcast_to(jnp.sum(gate_acc[0:CAP, :], axis=1, keepdims=True), (CAP, 128))


def _compact(block_starts, pos3, aff3):
    rows = pos3.shape[0]
    shp = jax.ShapeDtypeStruct((rows, CAP, 128), F32)
    return pl.pallas_call(
        _compact_kernel,
        grid_spec=pltpu.PrefetchScalarGridSpec(
            num_scalar_prefetch=1,
            grid=(rows,),
            in_specs=[pl.BlockSpec((1, 1, N_TOK), lambda i, st: (i, 0, 0))] * 2,
            out_specs=[pl.BlockSpec((1, CAP, 128), lambda i, st: (i, 0, 0))] * 2,
            scratch_shapes=[pltpu.VMEM((CAP + COMPACT_WIN, 128), F32)] * 2),
        out_shape=[shp, shp],
        compiler_params=_cparams(1),
        name="ec_compact",
    )(block_starts, pos3, aff3)


def _gather_kernel(idx_ref, h_ref, o_ref, tile):
    grp = pl.program_id(0)
    eb = pl.program_id(1)
    for ei in range(ROUTE_EXPERTS):
        base = (grp * N_EXPERTS + eb * ROUTE_EXPERTS + ei) * CAP

        def tok(j8, c, base=base):
            for u in range(8):
                j = j8 * 8 + u
                row = pl.multiple_of(idx_ref[base + j] * 8, 8)
                tile[pl.ds(j, 8, stride=TOK_STRIDE), :] = h_ref[0, pl.ds(row, 8), :]
            return c

        lax.fori_loop(0, CAP // 8, tok, 0)
        for c in range(TOK_ROWS):
            o_ref[ei, :, c * 128:(c + 1) * 128] = tile[c * TOK_STRIDE:c * TOK_STRIDE + CAP, :].astype(BF16)


def _gather(idx_flat, h_rows):
    return pl.pallas_call(
        _gather_kernel,
        grid_spec=pltpu.PrefetchScalarGridSpec(
            num_scalar_prefetch=1,
            grid=(2, N_EXPERTS // ROUTE_EXPERTS),
            in_specs=[pl.BlockSpec((1, N_TOK * 8, 128), lambda g, e, idx: (g, 0, 0))],
            out_specs=pl.BlockSpec((ROUTE_EXPERTS, CAP, D_MODEL), lambda g, e, idx: (e, g, 0)),
            scratch_shapes=[pltpu.VMEM((8 * TOK_STRIDE, 128), F32)]),
        out_shape=jax.ShapeDtypeStruct((N_EXPERTS, 2 * CAP, D_MODEL), BF16),
        compiler_params=_cparams(2),
        name="ec_gather",
    )(idx_flat, h_rows)


def _expert_kernel(xe_ref, w1_ref, w3_ref, w2_ref, gate_p_ref, gate_s_ref, o_ref, hid_scr):
    s = pl.program_id(1)
    subs = [(lo, min(FF_SUB, FF_SLAB - lo)) for lo in range(0, FF_SLAB, FF_SUB)]

    @pl.when(s < FF_SPLIT)
    def _():
        x = xe_ref[0]
        for lo, width in subs:
            h1 = jnp.dot(x, w1_ref[0, :, lo:lo + width].astype(BF16), preferred_element_type=F32)
            h3 = jnp.dot(x, w3_ref[0, :, lo:lo + width].astype(BF16), preferred_element_type=F32)
            hid_scr[s, :, lo:lo + width] = (h1 * _sigmoid(h1) * h3).astype(BF16)

    @pl.when(s >= FF_SPLIT)
    def _():
        acc = None
        for lo, width in subs:
            part = jnp.dot(hid_scr[s - FF_SPLIT, :, lo:lo + width], w2_ref[0, lo:lo + width, :].astype(BF16),
                           preferred_element_type=F32)
            acc = part if acc is None else acc + part

        @pl.when(s == FF_SPLIT)
        def _():
            o_ref[0] = acc

        @pl.when(s > FF_SPLIT)
        def _():
            o_ref[0] = o_ref[0] + acc

    @pl.when(s == 2 * FF_SPLIT - 1)
    def _():
        gate = jnp.concatenate([gate_p_ref[0][:, 0:1], gate_s_ref[0][:, 0:1]], axis=0)
        o_ref[0] = o_ref[0] * gate


def _experts(xe, w1, w3, w2, gate_rows):
    m = xe.shape[1]
    up = lambda e, s: (e, 0, jnp.minimum(s, FF_SPLIT - 1))
    down = lambda e, s: (e, jnp.maximum(s - FF_SPLIT, 0), 0)
    return pl.pallas_call(
        _expert_kernel,
        grid=(N_EXPERTS, 2 * FF_SPLIT),
        in_specs=[pl.BlockSpec((1, m, D_MODEL), lambda e, s: (e, 0, 0)),
                  pl.BlockSpec((1, D_MODEL, FF_SLAB), up),
                  pl.BlockSpec((1, D_MODEL, FF_SLAB), up),
                  pl.BlockSpec((1, FF_SLAB, D_MODEL), down),
                  pl.BlockSpec((1, CAP, 128), lambda e, s: (e, 0, 0)),
                  pl.BlockSpec((1, CAP, 128), lambda e, s: (N_EXPERTS + e, 0, 0))],
        out_specs=pl.BlockSpec((1, m, D_MODEL), lambda e, s: (e, 0, 0)),
        out_shape=jax.ShapeDtypeStruct((N_EXPERTS, m, D_MODEL), F32),
        scratch_shapes=[pltpu.VMEM((FF_SPLIT, m, FF_SLAB), BF16)],
        compiler_params=_cparams(2, vmem=EXPERT_VMEM_LIMIT),
        name="ec_experts",
    )(xe, w1, w3, w2, gate_rows, gate_rows)


def _combine_kernel(idx_ref, ye_ref, x1_ref, g2_ref, fg_ref, op_ref, os_ref, acc, tile):
    grp = pl.program_id(0)
    s = pl.program_id(1)
    n_scatter = N_EXPERTS // ROUTE_EXPERTS

    @pl.when(s == 0)
    def _():
        acc[...] = jnp.zeros_like(acc)

    @pl.when(s < n_scatter)
    def _():
        for ei in range(ROUTE_EXPERTS):
            base = (grp * N_EXPERTS + s * ROUTE_EXPERTS + ei) * CAP
            for c in range(TOK_ROWS):
                tile[c * TOK_STRIDE:c * TOK_STRIDE + CAP, :] = ye_ref[ei, :, c * 128:(c + 1) * 128]

            def tok(jb, c, base=base):
                rows, vals = [], []
                for u in range(SCATTER_UNROLL):
                    j = jb * SCATTER_UNROLL + u
                    row = pl.multiple_of(idx_ref[base + j] * 8, 8)
                    rows.append(row)
                    vals.append(acc[pl.ds(row, 8), :] + tile[pl.ds(j, 8, stride=TOK_STRIDE), :])
                for row, val in zip(rows, vals):
                    acc[pl.ds(row, 8), :] = val
                return c

            lax.fori_loop(0, CAP // SCATTER_UNROLL, tok, 0)

    @pl.when(s >= n_scatter)
    def _():
        row0 = (s - n_scatter) * (ROW_TILE * TOK_ROWS)
        y = jnp.concatenate([acc[pl.ds(row0 + c, ROW_TILE, stride=TOK_ROWS), :] for c in range(TOK_ROWS)], axis=1)
        x = x1_ref[...] + g2_ref[0] * y
        ms = jnp.mean(x * x, axis=-1, keepdims=True)
        out = x * lax.rsqrt(ms + RMS_EPS) * fg_ref[...]

        @pl.when(grp == 0)
        def _():
            op_ref[...] = out

        @pl.when(grp == 1)
        def _():
            os_ref[...] = out


def _combine_final(idx_flat, ye, x1, mod3, final_g):
    n_scatter = N_EXPERTS // ROUTE_EXPERTS
    tiles = N_TOK // ROW_TILE
    tile_of = lambda g, s: g * tiles + jnp.maximum(s - n_scatter, 0)
    shp = jax.ShapeDtypeStruct((N_TOK, D_MODEL), F32)
    return pl.pallas_call(
        _combine_kernel,
        grid_spec=pltpu.PrefetchScalarGridSpec(
            num_scalar_prefetch=1,
            grid=(2, n_scatter + tiles),
            in_specs=[pl.BlockSpec((ROUTE_EXPERTS, CAP, D_MODEL), lambda g, s, idx: (jnp.minimum(s, n_scatter - 1), g, 0)),
                      pl.BlockSpec((ROW_TILE, D_MODEL), lambda g, s, idx: (tile_of(g, s), 0)),
                      pl.BlockSpec((1, 1, D_MODEL), lambda g, s, idx: (_cond_index(tile_of(g, s)), 0, 5)),
                      pl.BlockSpec((1, D_MODEL), lambda g, s, idx: (0, 0))],
            out_specs=[pl.BlockSpec((ROW_TILE, D_MODEL),
                                    lambda g, s, idx: (jnp.where(g == 0, jnp.maximum(s - n_scatter, 0), tiles - 1), 0)),
                       pl.BlockSpec((ROW_TILE, D_MODEL),
                                    lambda g, s, idx: (jnp.where(g == 1, jnp.maximum(s - n_scatter, 0), 0), 0))],
            scratch_shapes=[pltpu.VMEM((N_TOK * TOK_ROWS, 128), F32), pltpu.VMEM((TOK_ROWS * TOK_STRIDE, 128), F32)]),
        out_shape=[shp, shp],
        compiler_params=_cparams(2),
        name="ec_combine_final",
    )(idx_flat, ye, x1, mod3, final_g)


def kernel(x_prompt, x_sample, state_wkv, c, c_ctx, norm1_g, norm2_g, w_mod, b_mod, w_in, conv_dw, conv_b, conv_ln_g,
           conv_ln_b, rwkv_w0, rwkv_w_up, rwkv_a0, rwkv_a_up, rwkv_g_up, rwkv_k_k, rwkv_k_a, rwkv_r_k, rwkv_gn_g,
           rwkv_gn_b, w_out, router, exp_w1, exp_w3, exp_w2, final_norm_g):
    row2 = lambda a: a.reshape(1, -1)
    x = (x_prompt.reshape(N_TOK, D_MODEL), x_sample.reshape(N_TOK, D_MODEL))
    cond8 = jnp.concatenate([c_ctx[None, :], c, jnp.zeros((8 - 1 - DEC_BATCH, D_MODEL), F32)], axis=0)

    mod = _modulation(cond8, w_mod[0], b_mod)
    mod3 = mod.reshape(8, 1, 6 * D_MODEL)

    lane_head = jnp.arange(GROUP_W, dtype=jnp.int32) // HEAD
    ones_blk = (lane_head[:, None] == lane_head[None, :]).astype(BF16)

    def per_dir_rows(up):
        z = jnp.zeros_like(up[0])
        return jnp.stack([jnp.concatenate([up[0], z], axis=0), jnp.concatenate([z, up[1]], axis=0)]).astype(BF16)
    conv, rkv, kap, w0, w1, b0, b1, kd0, kd1, g, bonus = _mixer_in(
        *x, mod3, norm1_g, w_in[0].astype(BF16), (conv_dw[0], conv_b, conv_ln_g, conv_ln_b),
        (ones_blk, rwkv_w0[0], per_dir_rows(rwkv_w_up[0]), rwkv_a0[0], per_dir_rows(rwkv_a_up[0]),
         rwkv_g_up[0].astype(BF16), rwkv_k_k, rwkv_k_a, row2(rwkv_r_k[0])))

    dir_ops = ((w0, b0, kd0), (w1, b1, kd1))
    yp_f, yp_b, new_state = _chunk_scan(rkv, kap, dir_ops, None, row0=0, n_batch=BATCH, t_len=SEQ, want_final=True)
    ys_f, ys_b = _chunk_scan(rkv, kap, dir_ops, state_wkv[:, 0], row0=N_TOK, n_batch=DEC_BATCH, t_len=DEC_SEQ,
                             want_final=False)

    r_pad = jnp.pad(router[0], ((0, 0), (0, 128 - N_EXPERTS)))
    r_hi = r_pad.astype(BF16)
    r_lo = (r_pad - r_hi.astype(F32)).astype(BF16)
    x1, h2_rows, logits = _postmix((yp_f, ys_f), (yp_b, ys_b), x, conv, bonus, g, mod3, rwkv_gn_g, rwkv_gn_b, ones_blk,
                                   w_out[0].astype(BF16), norm2_g, r_hi, r_lo)

    logits_t = logits[:, :N_EXPERTS].reshape(2, N_TOK, N_EXPERTS).transpose(0, 2, 1).reshape(2 * N_EXPERTS, N_TOK)
    lane = jnp.arange(128, dtype=jnp.int32)
    utri = (lane[:, None] < lane[None, :]).astype(BF16)
    pos, aff, starts = _select(logits_t, utri)
    block_starts = starts[:, :N_TOK // 128].astype(jnp.int32).reshape(-1)
    idx_c, gate_c = _compact(block_starts, pos.reshape(2 * N_EXPERTS, 1, N_TOK), aff.reshape(2 * N_EXPERTS, 1, N_TOK))
    idx_flat = idx_c[:, :, 0].astype(jnp.int32).reshape(-1)

    xe = _gather(idx_flat, h2_rows.reshape(2, N_TOK * TOK_ROWS, 128))
    ye = _experts(xe, exp_w1[0], exp_w3[0], exp_w2[0], gate_c)
    out_p, out_s = _combine_final(idx_flat, ye, x1, mod3, row2(final_norm_g))
    return (out_p.reshape(BATCH, SEQ, D_MODEL), out_s.reshape(DEC_BATCH, DEC_SEQ, D_MODEL),
            new_state.reshape(BATCH, 1, 2, N_HEADS, HEAD, HEAD))
```

```python
import functools

import jax
import jax.numpy as jnp
from jax import lax
from jax.experimental import pallas as pl
from jax.experimental.pallas import tpu as pltpu

F32 = jnp.float32
BF16 = jnp.bfloat16

D_MODEL = 1024
BATCH = 16
SEQ = 256
DEC_BATCH = 4
DEC_SEQ = 1024
GRID_W = 64
CONV_W = 512
CONV_K = 31
CONV_PAD = 15
CONV_GAP = 16
RWKV_W = 512
HEAD = 64
N_HEADS = 8
LORA = 64
GATE_LORA = 128
N_EXPERTS = 16
D_FF = 2816
D_IN = 2944
CAP = 512
N_TOK = 4096
RMS_EPS = 1e-6
LN_EPS = 1e-5
GN_EPS = 64e-5

ROW_TILE = 512
FF_SPLIT = 2
FF_SLAB = D_FF // FF_SPLIT
FF_SUB = 256
EXPERT_VMEM_LIMIT = 60 * 1024 * 1024
CHUNK = 64
HEADS_PER_GROUP = 4
GROUP_W = HEADS_PER_GROUP * HEAD
PASSES_A = 3
PASSES_INV = 1
PASSES_STATE = 1
SCAN_BATCHES = 4
TOK_ROWS = D_MODEL // 128
TOK_STRIDE = CAP + 8
ROUTE_EXPERTS = 4
SCATTER_UNROLL = 8
COMPACT_WIN = 128 + 8
BISECT_ITERS = 160
VMEM_LIMIT = 56 * 1024 * 1024


def _cparams(n_axes, vmem=None):
    return pltpu.CompilerParams(dimension_semantics=("arbitrary",) * n_axes,
                                vmem_limit_bytes=vmem or VMEM_LIMIT)


def _cond_index(i):
    n_prompt = (BATCH * SEQ) // ROW_TILE
    per_batch = DEC_SEQ // ROW_TILE
    return jnp.where(i < n_prompt, 0, 1 + (i - n_prompt) // per_batch)


def _split_dot(x, w_bf16):
    hi = x.astype(BF16)
    lo = (x - hi.astype(F32)).astype(BF16)
    return (jnp.dot(hi, w_bf16, preferred_element_type=F32)
            + jnp.dot(lo, w_bf16, preferred_element_type=F32))


def _head_sums(x, ones_bf16):
    return jnp.concatenate([_split_dot(x[:, c:c + GROUP_W], ones_bf16) for c in range(0, x.shape[1], GROUP_W)],
                           axis=1)


def _sigmoid(x):
    return 1.0 / (1.0 + jnp.exp(-x))


def _mod_kernel(c_ref, w_ref, b_ref, o_ref):
    c = c_ref[...]
    s = (c * _sigmoid(c)).astype(BF16)
    o_ref[...] = jnp.dot(s, w_ref[...].astype(BF16), preferred_element_type=F32) + b_ref[...]


def _modulation(cond8, w_mod, b_mod):
    n = 6 * D_MODEL
    tn = D_MODEL
    return pl.pallas_call(
        _mod_kernel,
        grid=(n // tn,),
        in_specs=[pl.BlockSpec((8, D_MODEL), lambda j: (0, 0)),
                  pl.BlockSpec((D_MODEL, tn), lambda j: (0, j)),
                  pl.BlockSpec((1, tn), lambda j: (0, j))],
        out_specs=pl.BlockSpec((8, tn), lambda j: (0, j)),
        out_shape=jax.ShapeDtypeStruct((8, n), F32),
        compiler_params=_cparams(1),
        name="modulation",
    )(cond8, w_mod, b_mod)


def _group_tile(n_prompt_tiles):
    return (lambda i: (jnp.minimum(i, n_prompt_tiles - 1), 0)), (lambda i: (jnp.maximum(i - n_prompt_tiles, 0), 0))


def _pick(prompt_ref, sample_ref, n_prompt_tiles):
    return jnp.where(pl.program_id(0) < n_prompt_tiles, prompt_ref[...], sample_ref[...])


def _mixer_in_kernel(x_ref, sh_ref, sc_ref, n1g_ref, w_ref, dw_ref, db_ref, lng_ref, lnb_ref, ones_ref,
                     w0_ref, wup_ref, a0_ref, aup_ref, gup_ref, kk_ref, ka_ref, rk_ref,
                     conv_o, rkv_o, kap_o, lw0_o, lw1_o, b0_o, b1_o, kd0_o, kd1_o, g_o, bonus_o, *, seg):
    x = x_ref[...]
    ms = jnp.mean(x * x, axis=-1, keepdims=True)
    h = x * lax.rsqrt(ms + RMS_EPS) * n1g_ref[...]
    h = (h * (1.0 + sc_ref[0]) + sh_ref[0]).astype(BF16)
    uv = jnp.dot(h, w_ref[:, 0:CONV_W], preferred_element_type=F32)
    ug = jnp.dot(h, w_ref[:, CONV_W:2 * CONV_W], preferred_element_type=F32)
    conv_o[...] = _conv_ln_silu(uv * _sigmoid(ug), seg, dw_ref, db_ref, lng_ref, lnb_ref)
    rkv = jnp.dot(h, w_ref[:, 2 * CONV_W:2 * CONV_W + 3 * RWKV_W], preferred_element_type=F32)
    rkv_o[...] = rkv
    lora = jnp.dot(h, w_ref[:, 2 * CONV_W + 3 * RWKV_W:D_IN], preferred_element_type=F32)
    _rwkv_operands(rkv, lora, ones_ref, w0_ref, wup_ref, a0_ref, aup_ref, gup_ref, kk_ref, ka_ref, rk_ref,
                   kap_o, (lw0_o, lw1_o), (b0_o, b1_o), (kd0_o, kd1_o), g_o, bonus_o)


def _mixer_in(x, mod3, norm1_g, w_in_bf16, conv_params, rwkv_params, *, seg, tile0):
    row = lambda c: pl.BlockSpec((ROW_TILE, c), lambda i: (i, 0))
    full = lambda a: pl.BlockSpec(a.shape, lambda i: (0,) * a.ndim)
    consts = (norm1_g, w_in_bf16) + tuple(conv_params) + tuple(rwkv_params)
    widths = [CONV_W, 3 * RWKV_W] + [RWKV_W] * 9
    return pl.pallas_call(
        functools.partial(_mixer_in_kernel, seg=seg),
        grid=(N_TOK // ROW_TILE,),
        in_specs=[row(D_MODEL),
                  pl.BlockSpec((1, 1, D_MODEL), lambda i: (_cond_index(tile0 + i), 0, 0)),
                  pl.BlockSpec((1, 1, D_MODEL), lambda i: (_cond_index(tile0 + i), 0, 1)),
                  ] + [full(a) for a in consts],
        out_specs=[row(c) for c in widths],
        out_shape=[jax.ShapeDtypeStruct((N_TOK, c), F32) for c in widths],
        compiler_params=_cparams(1),
        name=f"mixer_in_seg{seg}",
    )(x, mod3, mod3, *consts)


def _conv_ln_silu(u, seg, dw_ref, db_ref, g_ref, b_ref):
    n_seg = u.shape[0] // seg
    pitch = seg + CONV_GAP
    gap = jnp.zeros((CONV_GAP, u.shape[1]), F32)
    up = jnp.concatenate([piece for i in range(n_seg) for piece in (u[i * seg:(i + 1) * seg], gap)], axis=0)
    acc = up * dw_ref[CONV_PAD:CONV_PAD + 1, :]
    for k in range(CONV_K):
        s = k - CONV_PAD
        if s != 0:
            acc = acc + pltpu.roll(up, (-s) % up.shape[0], 0) * dw_ref[k:k + 1, :]
    out = jnp.concatenate([acc[i * pitch:i * pitch + seg] for i in range(n_seg)], axis=0) + db_ref[...]
    mu = jnp.mean(out, axis=-1, keepdims=True)
    d = out - mu
    var = jnp.mean(d * d, axis=-1, keepdims=True)
    out = d * lax.rsqrt(var + LN_EPS) * g_ref[...] + b_ref[...]
    return out * _sigmoid(out)


def _rwkv_operands(rkv, lora, ones_ref, w0_ref, wup_ref, a0_ref, aup_ref, gup_ref, kk_ref, ka_ref, rk_ref,
                   kap_o, lw_o, b_o, kd_o, g_o, bonus_o):
    r = rkv[:, 0:RWKV_W]
    k = rkv[:, RWKV_W:2 * RWKV_W]
    v = rkv[:, 2 * RWKV_W:3 * RWKV_W]
    xw = jnp.tanh(lora[:, 0:2 * LORA]).astype(BF16)
    xa = lora[:, 2 * LORA:4 * LORA].astype(BF16)
    xg = _sigmoid(lora[:, 4 * LORA:4 * LORA + GATE_LORA]).astype(BF16)
    ones = ones_ref[...]

    kk0 = k * kk_ref[...]
    ss = _head_sums(kk0 * kk0, ones)
    kap = kk0 * lax.rsqrt(jnp.maximum(ss, 1e-24))
    kap_o[...] = kap
    g_o[...] = jnp.dot(xg, gup_ref[...], preferred_element_type=F32)

    kd_sum = jnp.zeros_like(k)
    for d in range(2):
        wl = w0_ref[d:d + 1, :] + jnp.dot(xw, wup_ref[d], preferred_element_type=F32)
        lw_o[d][...] = -jnp.exp(-0.5) * _sigmoid(wl)
        a = _sigmoid(a0_ref[d:d + 1, :] + jnp.dot(xa, aup_ref[d], preferred_element_type=F32))
        b_o[d][...] = kap * a
        kd = k * (1.0 + (a - 1.0) * ka_ref[...])
        kd_o[d][...] = kd
        kd_sum = kd_sum + kd
    bonus_o[...] = _head_sums(r * kd_sum * rk_ref[...], ones) * v


def _pieces(x, passes):
    hi = x.astype(BF16)
    return (hi,) if passes == 1 else (hi, (x - hi.astype(F32)).astype(BF16))


_NN = ((1,), (0,))
_NT = ((1,), (1,))
_TN = ((0,), (0,))


def _mm(a, b, dims):
    dg = lambda p, q: lax.dot_general(p, q, (dims, ((), ())), preferred_element_type=F32)
    if len(a) == 1:
        out = dg(a[0], b[0])
    else:
        free = 1 - dims[0][0]
        both = dg(jnp.concatenate(a, axis=free), b[0])
        m = a[0].shape[free]
        out = both[:m] + both[m:]
    if len(b) > 1:
        out = out + dg(a[0], b[1])
    return out


def _block_diag(pieces, head_masks):
    return tuple(jnp.concatenate([p * m for m in head_masks], axis=0) for p in pieces)


def _chunk_masks(reverse):
    row = lax.broadcasted_iota(jnp.int32, (CHUNK, GROUP_W), 0)
    col = lax.broadcasted_iota(jnp.int32, (CHUNK, GROUP_W), 1) % CHUNK
    strict = (col > row) if reverse else (col < row)
    incl = (col >= row) if reverse else (col <= row)
    eye_side = jnp.where(col == row, 1.0, 0.0)
    tri = jnp.where(incl[:, :CHUNK], 1.0, 0.0).astype(BF16)
    return strict, incl, eye_side, tri, (0 if reverse else CHUNK - 1)


def _chunk_steps(chains):
    ln = CHUNK
    each = lambda f, *cols: [f(*args) for args in zip(*cols)]
    r, kap, v, lw, b, kd, n, masks = (list(col) for col in zip(*chains))
    strict = [m[0] for m in masks]
    incl = [m[1] for m in masks]
    lane_blk = lax.broadcasted_iota(jnp.int32, (ln, GROUP_W), 1) // HEAD
    head_masks = [jnp.where(lane_blk == h, 1.0, 0.0).astype(BF16) for h in range(HEADS_PER_GROUP)]
    bd = lambda x, passes: _block_diag(_pieces(x, passes), head_masks)
    r2 = lax.broadcasted_iota(jnp.int32, (GROUP_W, GROUP_W), 0) // HEAD
    c2 = lax.broadcasted_iota(jnp.int32, (GROUP_W, GROUP_W), 1) // HEAD
    same_head = r2 == c2

    def running_sum(lw_c, m):
        return _mm((m[3],), _pieces(lw_c, 3), _NN)

    cw = each(running_sum, lw, masks)
    tot = each(lambda cw_c, m: cw_c[m[4]:m[4] + 1, :], cw, masks)
    e_neg = each(lambda cw_c: jnp.exp(-cw_c), cw)
    kr = each(lambda kap_c, r_c, cw_c, lw_c: _pieces(
        jnp.concatenate([kap_c * jnp.exp(cw_c - lw_c), r_c * jnp.exp(cw_c)], axis=0), PASSES_A),
        kap, r, cw, lw)
    ab = each(lambda kr_c, b_c, e: _mm(kr_c, bd(b_c * e, PASSES_A), _NT), kr, b, e_neg)
    ak = each(lambda kr_c, kd_c, e: _mm(kr_c, bd(kd_c * e, PASSES_A), _NT), kr, kd, e_neg)

    q = each(lambda ab_c, s: jnp.where(s, -ab_c[:ln], 0.0), ab, strict)
    x = each(lambda q_c, m: m[2] + q_c, q, masks)
    q = each(lambda q_c: _mm(_pieces(q_c, PASSES_INV), bd(q_c, PASSES_INV), _NN), q)
    n_rounds = CHUNK.bit_length() - 2
    for j in range(1, n_rounds + 1):
        last_round = j == n_rounds
        lhs = x if last_round else each(lambda q_c, x_c: jnp.concatenate([q_c, x_c], axis=0), q, x)
        prod = each(lambda l_c, q_c: _mm(_pieces(l_c, PASSES_INV), bd(q_c, PASSES_INV), _NN), lhs, q)
        if last_round:
            x = each(lambda x_c, p_c: x_c + p_c, x, prod)
        else:
            q = each(lambda p_c: p_c[:ln], prod)
            x = each(lambda x_c, p_c: x_c + p_c[ln:], x, prod)

    a3 = each(lambda ab_c, i: jnp.where(i, ab_c[ln:], 0.0), ab, incl)
    a24 = each(lambda ak_c, s, i: jnp.concatenate([jnp.where(s, ak_c[:ln], 0.0), jnp.where(i, ak_c[ln:], 0.0)], axis=0),
               ak, strict, incl)
    av = each(lambda a_c, v_c: _mm(_pieces(a_c, PASSES_A), bd(v_c, PASSES_A), _NN), a24, v)
    krn = each(lambda kr_c, n_c: _mm(kr_c[:PASSES_STATE], _pieces(n_c, PASSES_STATE), _NT), kr, n)
    u = each(lambda x_c, krn_c, av_c: -_mm(_pieces(x_c, PASSES_INV), bd(krn_c[:ln] + av_c[:ln], PASSES_INV), _NN),
             x, krn, av)
    y = each(lambda krn_c, a3_c, u_c, av_c: krn_c[ln:] + _mm(_pieces(a3_c, PASSES_STATE), bd(u_c, PASSES_STATE), _NN)
             + av_c[ln:], krn, a3, u, av)
    e_rest = each(lambda tot_c, cw_c: jnp.exp(tot_c - cw_c), tot, cw)
    upd = each(lambda u_c, v_c, b_c, kd_c, e: _mm(_pieces(jnp.concatenate([u_c, v_c], axis=0), PASSES_A),
                                                   _pieces(jnp.concatenate([b_c * e, kd_c * e], axis=0), PASSES_A), _TN),
               u, v, b, kd, e_rest)
    n_new = each(lambda n_c, tot_c, upd_c: n_c * jnp.exp(tot_c) + jnp.where(same_head, upd_c, 0.0), n, tot, upd)
    return list(zip(y, n_new))


def _chunk_kernel(*refs, has_init, has_final):
    n_tok = 12 * SCAN_BATCHES
    tok_refs = [refs[k * 12:(k + 1) * 12] for k in range(SCAN_BATCHES)]
    refs = refs[n_tok:]
    s0 = refs[0] if has_init else None
    refs = refs[1:] if has_init else refs
    y_refs = refs[:2]
    sfin = refs[2] if has_final else None
    n_scr = refs[-1]
    c = pl.program_id(1)
    head_blocks = [(k, d, g, h) for k in range(SCAN_BATCHES) for d in range(2)
                   for g in range(N_HEADS // HEADS_PER_GROUP) for h in range(HEADS_PER_GROUP)]
    diag = lambda h: slice(h * HEAD, (h + 1) * HEAD)

    @pl.when(c == 0)
    def _():
        n_scr[...] = jnp.zeros(n_scr.shape, F32)
        if has_init:
            for k, d, g, h in head_blocks:
                n_scr[d, k, g, diag(h), diag(h)] = s0[k, d, g * HEADS_PER_GROUP + h]

    slots = [(k, d, g) for k in range(SCAN_BATCHES) for d in range(2) for g in range(N_HEADS // HEADS_PER_GROUP)]
    lanes = lambda g: slice(g * GROUP_W, (g + 1) * GROUP_W)
    masks = [_chunk_masks(reverse=False), _chunk_masks(reverse=True)]

    def operands(k, d, g):
        rf, kf, vf, rb, kb, vb, lwf, bf, kdf, lwb, bb, kdb = tok_refs[k]
        group = (rb, kb, vb, lwb, bb, kdb) if d else (rf, kf, vf, lwf, bf, kdf)
        return tuple(ref[:, lanes(g)] for ref in group)

    chains = [operands(k, d, g) + (n_scr[d, k, g], masks[d]) for k, d, g in slots]
    for (k, d, g), (y, n_new) in zip(slots, _chunk_steps(chains)):
        y_refs[d][k, :, lanes(g)] = y
        n_scr[d, k, g] = n_new

    if has_final:
        @pl.when(c == pl.num_programs(1) - 1)
        def _():
            for k, d, g, h in head_blocks:
                sfin[k, d, g * HEADS_PER_GROUP + h] = n_scr[d, k, g, diag(h), diag(h)]


def _chunk_scan(rkv, kap, dir_ops, s0, *, n_batch, t_len, want_final):
    n_chunks = t_len // CHUNK
    n_grp = N_HEADS // HEADS_PER_GROUP
    nb = SCAN_BATCHES

    def tok_spec(k, reverse, col=0):
        def index(bi, ci):
            return ((bi * nb + k) * n_chunks + (n_chunks - 1 - ci if reverse else ci), col)
        return pl.BlockSpec((CHUNK, RWKV_W), index)

    in_specs, args = [], []
    for k in range(nb):
        for rev in (False, True):
            in_specs += [tok_spec(k, rev, 0), tok_spec(k, rev), tok_spec(k, rev, 2)]
        in_specs += [tok_spec(k, False)] * 3 + [tok_spec(k, True)] * 3
        args += [rkv, kap, rkv, rkv, kap, rkv, *dir_ops[0], *dir_ops[1]]
    state = pl.BlockSpec((nb, 2, N_HEADS, HEAD, HEAD), lambda bi, ci: (bi, 0, 0, 0, 0))
    has_init = s0 is not None
    if has_init:
        in_specs.append(state)
        args.append(s0)
    y_shape = jax.ShapeDtypeStruct((n_batch, t_len, RWKV_W), F32)
    out_specs = [pl.BlockSpec((nb, CHUNK, RWKV_W), lambda bi, ci: (bi, ci, 0)),
                 pl.BlockSpec((nb, CHUNK, RWKV_W), lambda bi, ci: (bi, n_chunks - 1 - ci, 0))]
    out_shape = [y_shape, y_shape]
    if want_final:
        out_specs.append(state)
        out_shape.append(jax.ShapeDtypeStruct((n_batch, 2, N_HEADS, HEAD, HEAD), F32))
    outs = pl.pallas_call(
        functools.partial(_chunk_kernel, has_init=has_init, has_final=want_final),
        grid=(n_batch // nb, n_chunks),
        in_specs=in_specs,
        out_specs=out_specs,
        out_shape=out_shape,
        scratch_shapes=[pltpu.VMEM((2, nb, n_grp, GROUP_W, GROUP_W), F32)],
        compiler_params=_cparams(2),
        name=f"wkv_chunk_t{t_len}",
    )(*args)
    return [outs[0].reshape(n_batch * t_len, RWKV_W), outs[1].reshape(n_batch * t_len, RWKV_W)] + list(outs[2:])


def _postmix_kernel(yfp_ref, yfs_ref, ybp_ref, ybs_ref, xp_ref, xs_ref, convp_ref, convs_ref, bonusp_ref, bonuss_ref,
                    gp_ref, gs_ref, g1_ref, sh2_ref, sc2_ref, gng_ref, gnb_ref, ones_ref, wout_ref, n2g_ref, rhi_ref, rlo_ref,
                    x1_o, h2_o, logit_o):
    n_prompt = N_TOK // ROW_TILE
    ones = ones_ref[...]
    y = _pick(yfp_ref, yfs_ref, n_prompt) + _pick(ybp_ref, ybs_ref, n_prompt)
    mu = _head_sums(y, ones) * (1.0 / HEAD)
    d = y - mu
    var = _head_sums(d * d, ones) * (1.0 / HEAD)
    yn = d * lax.rsqrt(var + GN_EPS) * gng_ref[...] + gnb_ref[...]
    rw = ((yn + _pick(bonusp_ref, bonuss_ref, n_prompt)) * _pick(gp_ref, gs_ref, n_prompt)).astype(BF16)
    conv = _pick(convp_ref, convs_ref, n_prompt).astype(BF16)
    mix = (jnp.dot(conv, wout_ref[0:CONV_W, :], preferred_element_type=F32)
           + jnp.dot(rw, wout_ref[CONV_W:CONV_W + RWKV_W, :], preferred_element_type=F32))
    x1 = _pick(xp_ref, xs_ref, n_prompt) + g1_ref[0] * mix
    x1_o[...] = x1
    ms = jnp.mean(x1 * x1, axis=-1, keepdims=True)
    h2 = x1 * lax.rsqrt(ms + RMS_EPS) * n2g_ref[...]
    h2 = h2 * (1.0 + sc2_ref[0]) + sh2_ref[0]
    for c in range(TOK_ROWS):
        h2_o[pl.ds(c, ROW_TILE, stride=TOK_ROWS), :] = h2[:, c * 128:(c + 1) * 128]
    hi = h2.astype(BF16)
    lo = (h2 - hi.astype(F32)).astype(BF16)
    logit_o[...] = (jnp.dot(hi, rhi_ref[...], preferred_element_type=F32)
                    + jnp.dot(lo, rhi_ref[...], preferred_element_type=F32)
                    + jnp.dot(hi, rlo_ref[...], preferred_element_type=F32))


def _postmix(yf, yb, x, conv, bonus, g, mod3, gn_g, gn_b, ones_blk, w_out_bf16, norm2_g, r_hi, r_lo):
    n = 2 * N_TOK
    row = lambda c: pl.BlockSpec((ROW_TILE, c), lambda i: (i, 0))
    pair = lambda c: [pl.BlockSpec((ROW_TILE, c), m) for m in _group_tile(N_TOK // ROW_TILE)]
    full = lambda a: pl.BlockSpec(a.shape, lambda i: (0,) * a.ndim)
    modspec = lambda j: pl.BlockSpec((1, 1, D_MODEL), lambda i: (_cond_index(i), 0, j))
    return pl.pallas_call(
        _postmix_kernel,
        grid=(n // ROW_TILE,),
        in_specs=pair(RWKV_W) + pair(RWKV_W) + pair(D_MODEL) + pair(CONV_W) + pair(RWKV_W) + pair(RWKV_W) + [
            modspec(2), modspec(3), modspec(4), full(gn_g), full(gn_b), full(ones_blk),
            full(w_out_bf16), full(norm2_g), full(r_hi), full(r_lo)],
        out_specs=[row(D_MODEL), pl.BlockSpec((ROW_TILE * TOK_ROWS, 128), lambda i: (i, 0)), row(128)],
        out_shape=[jax.ShapeDtypeStruct((n, D_MODEL), F32), jax.ShapeDtypeStruct((n * TOK_ROWS, 128), F32),
                   jax.ShapeDtypeStruct((n, 128), F32)],
        compiler_params=_cparams(1),
        name="postmix",
    )(*yf, *yb, *x, *conv, *bonus, *g, mod3, mod3, mod3, gn_g, gn_b, ones_blk, w_out_bf16, norm2_g, r_hi, r_lo)


def _prefix_lanes(m, utri):
    rows = m.shape[0]
    off = jnp.zeros((rows, 1), F32)
    out, starts = [], []
    for c in range(N_TOK // 128):
        blk = m[:, c * 128:(c + 1) * 128]
        out.append(jnp.dot(blk.astype(BF16), utri, preferred_element_type=F32) + off)
        starts.append(off)
        off = off + jnp.sum(blk, axis=1, keepdims=True)
    return jnp.concatenate(out, axis=1), jnp.concatenate(starts, axis=1)


def _select_kernel(lt_ref, utri_ref, pos_o, aff_o, start_o):
    utri = utri_ref[...]
    affs = []
    for grp in range(2):
        x = lt_ref[grp * N_EXPERTS:(grp + 1) * N_EXPERTS, :]
        e = jnp.exp(x - jnp.max(x, axis=0, keepdims=True))
        affs.append(e / jnp.sum(e, axis=0, keepdims=True))
    aff = jnp.concatenate(affs, axis=0)
    aff_o[...] = aff
    n_rows = 2 * N_EXPERTS

    def body(_, lh):
        lo, hi = lh
        m2 = 0.5 * (lo + hi)
        m1 = 0.5 * (lo + m2)
        m3 = 0.5 * (m2 + hi)
        ge1, ge2, ge3 = (jnp.sum(jnp.where(aff > m, 1.0, 0.0), axis=1, keepdims=True) >= CAP for m in (m1, m2, m3))
        lo = jnp.where(ge3, m3, jnp.where(ge2, m2, jnp.where(ge1, m1, lo)))
        hi = jnp.where(ge3, hi, jnp.where(ge2, m3, jnp.where(ge1, m2, m1)))
        return lo, hi

    lo, hi = lax.fori_loop(0, BISECT_ITERS // 2, body,
                           (jnp.full((n_rows, 1), -1.0, F32), jnp.full((n_rows, 1), 1.0, F32)))
    gt = jnp.where(aff > hi, 1.0, 0.0)
    tie = jnp.where(aff > lo, 1.0, 0.0) - gt
    need = CAP - jnp.sum(gt, axis=1, keepdims=True)
    sel = gt + tie * jnp.where(_prefix_lanes(tie, utri)[0] < need, 1.0, 0.0)
    pos, starts = _prefix_lanes(sel, utri)
    pos_o[...] = jnp.where(sel > 0.5, pos, -1.0)
    start_o[...] = jnp.concatenate([starts, jnp.zeros((n_rows, 128 - N_TOK // 128), F32)], axis=1)


def _select(logits_t, utri):
    shp = jax.ShapeDtypeStruct((2 * N_EXPERTS, N_TOK), F32)
    full = lambda a: pl.BlockSpec(a.shape, lambda i: (0,) * a.ndim)
    return pl.pallas_call(
        _select_kernel,
        grid=(1,),
        in_specs=[full(logits_t), full(utri)],
        out_specs=[pl.BlockSpec(shp.shape, lambda i: (0, 0))] * 2 + [pl.BlockSpec((2 * N_EXPERTS, 128), lambda i: (0, 0))],
        out_shape=[shp, shp, jax.ShapeDtypeStruct((2 * N_EXPERTS, 128), F32)],
        compiler_params=_cparams(1),
        name="ec_select",
    )(logits_t, utri)


def _compact_kernel(start_ref, pos_ref, aff_ref, idx_o, gate_o, idx_acc, gate_acc):
    i = pl.program_id(0)
    n_blk = N_TOK // 128
    idx_acc[...] = jnp.zeros_like(idx_acc)
    gate_acc[...] = jnp.zeros_like(gate_acc)
    rel = lax.broadcasted_iota(jnp.int32, (COMPACT_WIN, 1), 0)
    for c in range(n_blk):
        base = pl.multiple_of(lax.shift_right_logical(start_ref[i * n_blk + c], 3) * 8, 8)
        slot = (rel + base).astype(F32)
        p = pos_ref[0, :, c * 128:(c + 1) * 128]
        a = aff_ref[0, :, c * 128:(c + 1) * 128]
        tok = (lax.broadcasted_iota(jnp.int32, (1, 128), 1) + c * 128).astype(F32)
        hit = p == slot
        idx_acc[pl.ds(base, COMPACT_WIN), :] = idx_acc[pl.ds(base, COMPACT_WIN), :] + jnp.where(hit, tok, 0.0)
        gate_acc[pl.ds(base, COMPACT_WIN), :] = gate_acc[pl.ds(base, COMPACT_WIN), :] + jnp.where(hit, a, 0.0)
    idx_o[0] = jnp.broadcast_to(jnp.sum(idx_acc[0:CAP, :], axis=1, keepdims=True), (CAP, 128))
    gate_o[0] = jnp.broadcast_to(jnp.sum(gate_acc[0:CAP, :], axis=1, keepdims=True), (CAP, 128))


def _compact(block_starts, pos3, aff3):
    rows = pos3.shape[0]
    shp = jax.ShapeDtypeStruct((rows, CAP, 128), F32)
    return pl.pallas_call(
        _compact_kernel,
        grid_spec=pltpu.PrefetchScalarGridSpec(
            num_scalar_prefetch=1,
            grid=(rows,),
            in_specs=[pl.BlockSpec((1, 1, N_TOK), lambda i, st: (i, 0, 0))] * 2,
            out_specs=[pl.BlockSpec((1, CAP, 128), lambda i, st: (i, 0, 0))] * 2,
            scratch_shapes=[pltpu.VMEM((CAP + COMPACT_WIN, 128), F32)] * 2),
        out_shape=[shp, shp],
        compiler_params=_cparams(1),
        name="ec_compact",
    )(block_starts, pos3, aff3)


def _gather_kernel(idx_ref, h_ref, o_ref, tile):
    grp = pl.program_id(0)
    eb = pl.program_id(1)
    for ei in range(ROUTE_EXPERTS):
        base = (grp * N_EXPERTS + eb * ROUTE_EXPERTS + ei) * CAP

        def tok(j8, c, base=base):
            for u in range(8):
                j = j8 * 8 + u
                row = pl.multiple_of(idx_ref[base + j] * 8, 8)
                tile[pl.ds(j, 8, stride=TOK_STRIDE), :] = h_ref[0, pl.ds(row, 8), :]
            return c

        lax.fori_loop(0, CAP // 8, tok, 0)
        for c in range(TOK_ROWS):
            o_ref[ei, :, c * 128:(c + 1) * 128] = tile[c * TOK_STRIDE:c * TOK_STRIDE + CAP, :].astype(BF16)


def _gather(idx_flat, h_rows):
    return pl.pallas_call(
        _gather_kernel,
        grid_spec=pltpu.PrefetchScalarGridSpec(
            num_scalar_prefetch=1,
            grid=(2, N_EXPERTS // ROUTE_EXPERTS),
            in_specs=[pl.BlockSpec((1, N_TOK * 8, 128), lambda g, e, idx: (g, 0, 0))],
            out_specs=pl.BlockSpec((ROUTE_EXPERTS, CAP, D_MODEL), lambda g, e, idx: (e, g, 0)),
            scratch_shapes=[pltpu.VMEM((8 * TOK_STRIDE, 128), F32)]),
        out_shape=jax.ShapeDtypeStruct((N_EXPERTS, 2 * CAP, D_MODEL), BF16),
        compiler_params=_cparams(2),
        name="ec_gather",
    )(idx_flat, h_rows)


def _expert_kernel(xe_ref, w1_ref, w3_ref, w2_ref, gate_p_ref, gate_s_ref, o_ref, hid_scr):
    s = pl.program_id(1)
    subs = [(lo, min(FF_SUB, FF_SLAB - lo)) for lo in range(0, FF_SLAB, FF_SUB)]

    @pl.when(s < FF_SPLIT)
    def _():
        x = xe_ref[0]
        for lo, width in subs:
            h1 = jnp.dot(x, w1_ref[0, :, lo:lo + width].astype(BF16), preferred_element_type=F32)
            h3 = jnp.dot(x, w3_ref[0, :, lo:lo + width].astype(BF16), preferred_element_type=F32)
            hid_scr[s, :, lo:lo + width] = (h1 * _sigmoid(h1) * h3).astype(BF16)

    @pl.when(s >= FF_SPLIT)
    def _():
        acc = None
        for lo, width in subs:
            part = jnp.dot(hid_scr[s - FF_SPLIT, :, lo:lo + width], w2_ref[0, lo:lo + width, :].astype(BF16),
                           preferred_element_type=F32)
            acc = part if acc is None else acc + part

        chunks = [(grp, c, slice(c * TOK_STRIDE, c * TOK_STRIDE + CAP)) for grp in range(2) for c in range(TOK_ROWS)]
        piece = lambda grp, c: acc[grp * CAP:(grp + 1) * CAP, c * 128:(c + 1) * 128]

        @pl.when(s == FF_SPLIT)
        def _():
            for grp, c, rows in chunks:
                o_ref[0, grp, rows, :] = piece(grp, c)
                o_ref[0, grp, c * TOK_STRIDE + CAP:(c + 1) * TOK_STRIDE, :] = jnp.zeros((TOK_STRIDE - CAP, 128), F32)

        @pl.when((s > FF_SPLIT) & (s < 2 * FF_SPLIT - 1))
        def _():
            for grp, c, rows in chunks:
                o_ref[0, grp, rows, :] = o_ref[0, grp, rows, :] + piece(grp, c)

        @pl.when(s == 2 * FF_SPLIT - 1)
        def _():
            gates = (gate_p_ref[0][:, 0:1], gate_s_ref[0][:, 0:1])
            for grp, c, rows in chunks:
                o_ref[0, grp, rows, :] = (o_ref[0, grp, rows, :] + piece(grp, c)) * gates[grp]


def _experts(xe, w1, w3, w2, gate_rows):
    m = xe.shape[1]
    up = lambda e, s: (e, 0, jnp.minimum(s, FF_SPLIT - 1))
    down = lambda e, s: (e, jnp.maximum(s - FF_SPLIT, 0), 0)
    return pl.pallas_call(
        _expert_kernel,
        grid=(N_EXPERTS, 2 * FF_SPLIT),
        in_specs=[pl.BlockSpec((1, m, D_MODEL), lambda e, s: (e, 0, 0)),
                  pl.BlockSpec((1, D_MODEL, FF_SLAB), up),
                  pl.BlockSpec((1, D_MODEL, FF_SLAB), up),
                  pl.BlockSpec((1, FF_SLAB, D_MODEL), down),
                  pl.BlockSpec((1, CAP, 128), lambda e, s: (e, 0, 0)),
                  pl.BlockSpec((1, CAP, 128), lambda e, s: (N_EXPERTS + e, 0, 0))],
        out_specs=pl.BlockSpec((1, 2, TOK_ROWS * TOK_STRIDE, 128), lambda e, s: (e, 0, 0, 0)),
        out_shape=jax.ShapeDtypeStruct((N_EXPERTS, 2, TOK_ROWS * TOK_STRIDE, 128), F32),
        scratch_shapes=[pltpu.VMEM((FF_SPLIT, m, FF_SLAB), BF16)],
        compiler_params=_cparams(2, vmem=EXPERT_VMEM_LIMIT),
        name="ec_experts",
    )(xe, w1, w3, w2, gate_rows, gate_rows)


def _combine_kernel(idx_ref, ye_ref, x1_ref, g2_ref, fg_ref, op_ref, os_ref, acc):
    grp = pl.program_id(0)
    s = pl.program_id(1)
    n_scatter = N_EXPERTS // ROUTE_EXPERTS

    @pl.when(s == 0)
    def _():
        acc[...] = jnp.zeros_like(acc)

    @pl.when(s < n_scatter)
    def _():
        for ei in range(ROUTE_EXPERTS):
            base = (grp * N_EXPERTS + s * ROUTE_EXPERTS + ei) * CAP
            def tok(jb, c, base=base):
                rows, vals = [], []
                for u in range(SCATTER_UNROLL):
                    j = jb * SCATTER_UNROLL + u
                    row = pl.multiple_of(idx_ref[base + j] * 8, 8)
                    rows.append(row)
                    vals.append(acc[pl.ds(row, 8), :] + ye_ref[ei, 0, pl.ds(j, 8, stride=TOK_STRIDE), :])
                for row, val in zip(rows, vals):
                    acc[pl.ds(row, 8), :] = val
                return c

            lax.fori_loop(0, CAP // SCATTER_UNROLL, tok, 0)

    @pl.when(s >= n_scatter)
    def _():
        row0 = (s - n_scatter) * (ROW_TILE * TOK_ROWS)
        y = jnp.concatenate([acc[pl.ds(row0 + c, ROW_TILE, stride=TOK_ROWS), :] for c in range(TOK_ROWS)], axis=1)
        x = x1_ref[...] + g2_ref[0] * y
        ms = jnp.mean(x * x, axis=-1, keepdims=True)
        out = x * lax.rsqrt(ms + RMS_EPS) * fg_ref[...]

        @pl.when(grp == 0)
        def _():
            op_ref[...] = out

        @pl.when(grp == 1)
        def _():
            os_ref[...] = out


def _combine_final(idx_flat, ye, x1, mod3, final_g):
    n_scatter = N_EXPERTS // ROUTE_EXPERTS
    tiles = N_TOK // ROW_TILE
    tile_of = lambda g, s: g * tiles + jnp.maximum(s - n_scatter, 0)
    shp = jax.ShapeDtypeStruct((N_TOK, D_MODEL), F32)
    return pl.pallas_call(
        _combine_kernel,
        grid_spec=pltpu.PrefetchScalarGridSpec(
            num_scalar_prefetch=1,
            grid=(2, n_scatter + tiles),
            in_specs=[pl.BlockSpec((ROUTE_EXPERTS, 1, TOK_ROWS * TOK_STRIDE, 128),
                                   lambda g, s, idx: (jnp.minimum(s, n_scatter - 1), g, 0, 0)),
                      pl.BlockSpec((ROW_TILE, D_MODEL), lambda g, s, idx: (tile_of(g, s), 0)),
                      pl.BlockSpec((1, 1, D_MODEL), lambda g, s, idx: (_cond_index(tile_of(g, s)), 0, 5)),
                      pl.BlockSpec((1, D_MODEL), lambda g, s, idx: (0, 0))],
            out_specs=[pl.BlockSpec((ROW_TILE, D_MODEL),
                                    lambda g, s, idx: (jnp.where(g == 0, jnp.maximum(s - n_scatter, 0), tiles - 1), 0)),
                       pl.BlockSpec((ROW_TILE, D_MODEL),
                                    lambda g, s, idx: (jnp.where(g == 1, jnp.maximum(s - n_scatter, 0), 0), 0))],
            scratch_shapes=[pltpu.VMEM((N_TOK * TOK_ROWS, 128), F32)]),
        out_shape=[shp, shp],
        compiler_params=_cparams(2),
        name="ec_combine_final",
    )(idx_flat, ye, x1, mod3, final_g)


def kernel(x_prompt, x_sample, state_wkv, c, c_ctx, norm1_g, norm2_g, w_mod, b_mod, w_in, conv_dw, conv_b, conv_ln_g,
           conv_ln_b, rwkv_w0, rwkv_w_up, rwkv_a0, rwkv_a_up, rwkv_g_up, rwkv_k_k, rwkv_k_a, rwkv_r_k, rwkv_gn_g,
           rwkv_gn_b, w_out, router, exp_w1, exp_w3, exp_w2, final_norm_g):
    row2 = lambda a: a.reshape(1, -1)
    x = (x_prompt.reshape(N_TOK, D_MODEL), x_sample.reshape(N_TOK, D_MODEL))
    cond8 = jnp.concatenate([c_ctx[None, :], c, jnp.zeros((8 - 1 - DEC_BATCH, D_MODEL), F32)], axis=0)

    mod = _modulation(cond8, w_mod[0], b_mod)
    mod3 = mod.reshape(8, 1, 6 * D_MODEL)

    lane_head = jnp.arange(GROUP_W, dtype=jnp.int32) // HEAD
    ones_blk = (lane_head[:, None] == lane_head[None, :]).astype(BF16)

    def per_dir_rows(up):
        z = jnp.zeros_like(up[0])
        return jnp.stack([jnp.concatenate([up[0], z], axis=0), jnp.concatenate([z, up[1]], axis=0)]).astype(BF16)
    mixer_consts = (mod3, norm1_g, w_in[0].astype(BF16), (conv_dw[0], conv_b, conv_ln_g, conv_ln_b),
                    (ones_blk, rwkv_w0[0], per_dir_rows(rwkv_w_up[0]), rwkv_a0[0], per_dir_rows(rwkv_a_up[0]),
                     rwkv_g_up[0].astype(BF16), rwkv_k_k, rwkv_k_a, row2(rwkv_r_k[0])))
    groups = [_mixer_in(x[0], *mixer_consts, seg=SEQ, tile0=0),
              _mixer_in(x[1], *mixer_consts, seg=GRID_W, tile0=N_TOK // ROW_TILE)]
    conv, rkv, kap, w0, w1, b0, b1, kd0, kd1, g, bonus = zip(*groups)

    dir_ops = lambda i: ((w0[i], b0[i], kd0[i]), (w1[i], b1[i], kd1[i]))
    yp_f, yp_b, new_state = _chunk_scan(rkv[0], kap[0], dir_ops(0), None, n_batch=BATCH, t_len=SEQ, want_final=True)
    ys_f, ys_b = _chunk_scan(rkv[1], kap[1], dir_ops(1), state_wkv[:, 0], n_batch=DEC_BATCH, t_len=DEC_SEQ,
                             want_final=False)

    r_pad = jnp.pad(router[0], ((0, 0), (0, 128 - N_EXPERTS)))
    r_hi = r_pad.astype(BF16)
    r_lo = (r_pad - r_hi.astype(F32)).astype(BF16)
    x1, h2_rows, logits = _postmix((yp_f, ys_f), (yp_b, ys_b), x, conv, bonus, g, mod3, rwkv_gn_g, rwkv_gn_b, ones_blk,
                                   w_out[0].astype(BF16), norm2_g, r_hi, r_lo)

    logits_t = logits[:, :N_EXPERTS].reshape(2, N_TOK, N_EXPERTS).transpose(0, 2, 1).reshape(2 * N_EXPERTS, N_TOK)
    lane = jnp.arange(128, dtype=jnp.int32)
    utri = (lane[:, None] < lane[None, :]).astype(BF16)
    pos, aff, starts = _select(logits_t, utri)
    block_starts = starts[:, :N_TOK // 128].astype(jnp.int32).reshape(-1)
    idx_c, gate_c = _compact(block_starts, pos.reshape(2 * N_EXPERTS, 1, N_TOK), aff.reshape(2 * N_EXPERTS, 1, N_TOK))
    idx_flat = idx_c[:, :, 0].astype(jnp.int32).reshape(-1)

    xe = _gather(idx_flat, h2_rows.reshape(2, N_TOK * TOK_ROWS, 128))
    ye = _experts(xe, exp_w1[0], exp_w3[0], exp_w2[0], gate_c)
    out_p, out_s = _combine_final(idx_flat, ye, x1, mod3, row2(final_norm_g))
    return (out_p.reshape(BATCH, SEQ, D_MODEL), out_s.reshape(DEC_BATCH, DEC_SEQ, D_MODEL),
            new_state.reshape(BATCH, 1, 2, N_HEADS, HEAD, HEAD))
```

```python
import functools

import jax
import jax.numpy as jnp
from jax import lax
from jax.experimental import pallas as pl
from jax.experimental.pallas import tpu as pltpu

F32 = jnp.float32
BF16 = jnp.bfloat16

D_MODEL = 1024
BATCH = 16
SEQ = 256
DEC_BATCH = 4
DEC_SEQ = 1024
GRID_W = 64
CONV_W = 512
CONV_K = 31
CONV_PAD = 15
CONV_GAP = 16
RWKV_W = 512
HEAD = 64
N_HEADS = 8
LORA = 64
GATE_LORA = 128
N_EXPERTS = 16
D_FF = 2816
D_IN = 2944
CAP = 512
N_TOK = 4096
RMS_EPS = 1e-6
LN_EPS = 1e-5
GN_EPS = 64e-5

ROW_TILE = 512
FF_SPLIT = 2
FF_SLAB = D_FF // FF_SPLIT
FF_SUB = 256
EXPERT_VMEM_LIMIT = 60 * 1024 * 1024
CHUNK = 64
HEADS_PER_GROUP = 4
GROUP_W = HEADS_PER_GROUP * HEAD
PASSES_INTRA = 1
PASSES_INV = 1
PASSES_STATE = 1
PASSES_UPDATE = 3
SCAN_BATCHES = 4
TOK_ROWS = D_MODEL // 128
TOK_STRIDE = CAP + 8
ROUTE_EXPERTS = 4
SCATTER_UNROLL = 8
COMPACT_WIN = 128 + 8
BISECT_ITERS = 160
VMEM_LIMIT = 56 * 1024 * 1024


def _cparams(n_axes, vmem=None):
    return pltpu.CompilerParams(dimension_semantics=("arbitrary",) * n_axes,
                                vmem_limit_bytes=vmem or VMEM_LIMIT)


def _cond_index(i):
    n_prompt = (BATCH * SEQ) // ROW_TILE
    per_batch = DEC_SEQ // ROW_TILE
    return jnp.where(i < n_prompt, 0, 1 + (i - n_prompt) // per_batch)


def _split_dot(x, w_bf16):
    hi = x.astype(BF16)
    lo = (x - hi.astype(F32)).astype(BF16)
    return (jnp.dot(hi, w_bf16, preferred_element_type=F32)
            + jnp.dot(lo, w_bf16, preferred_element_type=F32))


def _head_sums(x, ones_bf16):
    return jnp.concatenate([_split_dot(x[:, c:c + GROUP_W], ones_bf16) for c in range(0, x.shape[1], GROUP_W)],
                           axis=1)


def _sigmoid(x):
    return 1.0 / (1.0 + jnp.exp(-x))


def _mod_kernel(c_ref, w_ref, b_ref, o_ref):
    c = c_ref[...]
    s = (c * _sigmoid(c)).astype(BF16)
    o_ref[...] = jnp.dot(s, w_ref[...].astype(BF16), preferred_element_type=F32) + b_ref[...]


def _modulation(cond8, w_mod, b_mod):
    n = 6 * D_MODEL
    tn = D_MODEL
    return pl.pallas_call(
        _mod_kernel,
        grid=(n // tn,),
        in_specs=[pl.BlockSpec((8, D_MODEL), lambda j: (0, 0)),
                  pl.BlockSpec((D_MODEL, tn), lambda j: (0, j)),
                  pl.BlockSpec((1, tn), lambda j: (0, j))],
        out_specs=pl.BlockSpec((8, tn), lambda j: (0, j)),
        out_shape=jax.ShapeDtypeStruct((8, n), F32),
        compiler_params=_cparams(1),
        name="modulation",
    )(cond8, w_mod, b_mod)


def _group_tile(n_prompt_tiles):
    return (lambda i: (jnp.minimum(i, n_prompt_tiles - 1), 0)), (lambda i: (jnp.maximum(i - n_prompt_tiles, 0), 0))


def _pick(prompt_ref, sample_ref, n_prompt_tiles):
    return jnp.where(pl.program_id(0) < n_prompt_tiles, prompt_ref[...], sample_ref[...])


def _mixer_in_kernel(x_ref, sh_ref, sc_ref, n1g_ref, w_ref, dw_ref, db_ref, lng_ref, lnb_ref, ones_ref,
                     w0_ref, wup_ref, a0_ref, aup_ref, gup_ref, kk_ref, ka_ref, rk_ref,
                     conv_o, rkv_o, kap_o, lw0_o, lw1_o, b0_o, b1_o, kd0_o, kd1_o, g_o, bonus_o, *, seg):
    x = x_ref[...]
    ms = jnp.mean(x * x, axis=-1, keepdims=True)
    h = x * lax.rsqrt(ms + RMS_EPS) * n1g_ref[...]
    h = (h * (1.0 + sc_ref[0]) + sh_ref[0]).astype(BF16)
    uv = jnp.dot(h, w_ref[:, 0:CONV_W], preferred_element_type=F32)
    ug = jnp.dot(h, w_ref[:, CONV_W:2 * CONV_W], preferred_element_type=F32)
    conv_o[...] = _conv_ln_silu(uv * _sigmoid(ug), seg, dw_ref, db_ref, lng_ref, lnb_ref)
    rkv = jnp.dot(h, w_ref[:, 2 * CONV_W:2 * CONV_W + 3 * RWKV_W], preferred_element_type=F32)
    rkv_o[...] = rkv
    lora = jnp.dot(h, w_ref[:, 2 * CONV_W + 3 * RWKV_W:D_IN], preferred_element_type=F32)
    _rwkv_operands(rkv, lora, ones_ref, w0_ref, wup_ref, a0_ref, aup_ref, gup_ref, kk_ref, ka_ref, rk_ref,
                   kap_o, (lw0_o, lw1_o), (b0_o, b1_o), (kd0_o, kd1_o), g_o, bonus_o)


def _mixer_in(x, mod3, norm1_g, w_in_bf16, conv_params, rwkv_params, *, seg, tile0):
    row = lambda c: pl.BlockSpec((ROW_TILE, c), lambda i: (i, 0))
    full = lambda a: pl.BlockSpec(a.shape, lambda i: (0,) * a.ndim)
    consts = (norm1_g, w_in_bf16) + tuple(conv_params) + tuple(rwkv_params)
    widths = [CONV_W, 3 * RWKV_W] + [RWKV_W] * 9
    return pl.pallas_call(
        functools.partial(_mixer_in_kernel, seg=seg),
        grid=(N_TOK // ROW_TILE,),
        in_specs=[row(D_MODEL),
                  pl.BlockSpec((1, 1, D_MODEL), lambda i: (_cond_index(tile0 + i), 0, 0)),
                  pl.BlockSpec((1, 1, D_MODEL), lambda i: (_cond_index(tile0 + i), 0, 1)),
                  ] + [full(a) for a in consts],
        out_specs=[row(c) for c in widths],
        out_shape=[jax.ShapeDtypeStruct((N_TOK, c), F32) for c in widths],
        compiler_params=_cparams(1),
        name=f"mixer_in_seg{seg}",
    )(x, mod3, mod3, *consts)


def _conv_ln_silu(u, seg, dw_ref, db_ref, g_ref, b_ref):
    n_seg = u.shape[0] // seg
    pitch = seg + CONV_GAP
    gap = jnp.zeros((CONV_GAP, u.shape[1]), F32)
    up = jnp.concatenate([piece for i in range(n_seg) for piece in (u[i * seg:(i + 1) * seg], gap)], axis=0)
    acc = up * dw_ref[CONV_PAD:CONV_PAD + 1, :]
    for k in range(CONV_K):
        s = k - CONV_PAD
        if s != 0:
            acc = acc + pltpu.roll(up, (-s) % up.shape[0], 0) * dw_ref[k:k + 1, :]
    out = jnp.concatenate([acc[i * pitch:i * pitch + seg] for i in range(n_seg)], axis=0) + db_ref[...]
    mu = jnp.mean(out, axis=-1, keepdims=True)
    d = out - mu
    var = jnp.mean(d * d, axis=-1, keepdims=True)
    out = d * lax.rsqrt(var + LN_EPS) * g_ref[...] + b_ref[...]
    return out * _sigmoid(out)


def _rwkv_operands(rkv, lora, ones_ref, w0_ref, wup_ref, a0_ref, aup_ref, gup_ref, kk_ref, ka_ref, rk_ref,
                   kap_o, lw_o, b_o, kd_o, g_o, bonus_o):
    r = rkv[:, 0:RWKV_W]
    k = rkv[:, RWKV_W:2 * RWKV_W]
    v = rkv[:, 2 * RWKV_W:3 * RWKV_W]
    xw = jnp.tanh(lora[:, 0:2 * LORA]).astype(BF16)
    xa = lora[:, 2 * LORA:4 * LORA].astype(BF16)
    xg = _sigmoid(lora[:, 4 * LORA:4 * LORA + GATE_LORA]).astype(BF16)
    ones = ones_ref[...]

    kk0 = k * kk_ref[...]
    ss = _head_sums(kk0 * kk0, ones)
    kap = kk0 * lax.rsqrt(jnp.maximum(ss, 1e-24))
    kap_o[...] = kap
    g_o[...] = jnp.dot(xg, gup_ref[...], preferred_element_type=F32)

    kd_sum = jnp.zeros_like(k)
    for d in range(2):
        wl = w0_ref[d:d + 1, :] + jnp.dot(xw, wup_ref[d], preferred_element_type=F32)
        lw_o[d][...] = -jnp.exp(-0.5) * _sigmoid(wl)
        a = _sigmoid(a0_ref[d:d + 1, :] + jnp.dot(xa, aup_ref[d], preferred_element_type=F32))
        b_o[d][...] = kap * a
        kd = k * (1.0 + (a - 1.0) * ka_ref[...])
        kd_o[d][...] = kd
        kd_sum = kd_sum + kd
    bonus_o[...] = _head_sums(r * kd_sum * rk_ref[...], ones) * v


def _pieces(x, passes):
    hi = x.astype(BF16)
    return (hi,) if passes == 1 else (hi, (x - hi.astype(F32)).astype(BF16))


_NN = ((1,), (0,))
_NT = ((1,), (1,))
_TN = ((0,), (0,))


def _mm(a, b, dims):
    dg = lambda p, q: lax.dot_general(p, q, (dims, ((), ())), preferred_element_type=F32)
    if len(a) == 1:
        out = dg(a[0], b[0])
    else:
        free = 1 - dims[0][0]
        both = dg(jnp.concatenate(a, axis=free), b[0])
        m = a[0].shape[free]
        out = both[:m] + both[m:]
    if len(b) > 1:
        out = out + dg(a[0], b[1])
    return out


def _block_diag(pieces, head_masks):
    return tuple(jnp.concatenate([p * m for m in head_masks], axis=0) for p in pieces)


def _chunk_masks(reverse):
    row = lax.broadcasted_iota(jnp.int32, (CHUNK, GROUP_W), 0)
    col = lax.broadcasted_iota(jnp.int32, (CHUNK, GROUP_W), 1) % CHUNK
    strict = (col > row) if reverse else (col < row)
    incl = (col >= row) if reverse else (col <= row)
    eye_side = jnp.where(col == row, 1.0, 0.0)
    tri = jnp.where(incl[:, :CHUNK], 1.0, 0.0).astype(BF16)
    return strict, incl, eye_side, tri, (0 if reverse else CHUNK - 1)


def _chunk_steps(chains):
    ln = CHUNK
    each = lambda f, *cols: [f(*args) for args in zip(*cols)]
    r, kap, v, lw, b, kd, n, masks = (list(col) for col in zip(*chains))
    strict = [m[0] for m in masks]
    incl = [m[1] for m in masks]
    lane_blk = lax.broadcasted_iota(jnp.int32, (ln, GROUP_W), 1) // HEAD
    head_masks = [jnp.where(lane_blk == h, 1.0, 0.0).astype(BF16) for h in range(HEADS_PER_GROUP)]
    bd = lambda x, passes: _block_diag(_pieces(x, passes), head_masks)
    r2 = lax.broadcasted_iota(jnp.int32, (GROUP_W, GROUP_W), 0) // HEAD
    c2 = lax.broadcasted_iota(jnp.int32, (GROUP_W, GROUP_W), 1) // HEAD
    same_head = r2 == c2

    def running_sum(lw_c, m):
        return _mm((m[3],), _pieces(lw_c, 3), _NN)

    cw = each(running_sum, lw, masks)
    tot = each(lambda cw_c, m: cw_c[m[4]:m[4] + 1, :], cw, masks)
    e_neg = each(lambda cw_c: jnp.exp(-cw_c), cw)
    kr = each(lambda kap_c, r_c, cw_c, lw_c: _pieces(
        jnp.concatenate([kap_c * jnp.exp(cw_c - lw_c), r_c * jnp.exp(cw_c)], axis=0), PASSES_INTRA),
        kap, r, cw, lw)
    ab = each(lambda kr_c, b_c, e: _mm(kr_c, bd(b_c * e, PASSES_INTRA), _NT), kr, b, e_neg)
    ak = each(lambda kr_c, kd_c, e: _mm(kr_c, bd(kd_c * e, PASSES_INTRA), _NT), kr, kd, e_neg)

    q = each(lambda ab_c, s: jnp.where(s, -ab_c[:ln], 0.0), ab, strict)
    x = each(lambda q_c, m: m[2] + q_c, q, masks)
    q = each(lambda q_c: _mm(_pieces(q_c, PASSES_INV), bd(q_c, PASSES_INV), _NN), q)
    n_rounds = CHUNK.bit_length() - 2
    for j in range(1, n_rounds + 1):
        last_round = j == n_rounds
        lhs = x if last_round else each(lambda q_c, x_c: jnp.concatenate([q_c, x_c], axis=0), q, x)
        prod = each(lambda l_c, q_c: _mm(_pieces(l_c, PASSES_INV), bd(q_c, PASSES_INV), _NN), lhs, q)
        if last_round:
            x = each(lambda x_c, p_c: x_c + p_c, x, prod)
        else:
            q = each(lambda p_c: p_c[:ln], prod)
            x = each(lambda x_c, p_c: x_c + p_c[ln:], x, prod)

    a3 = each(lambda ab_c, i: jnp.where(i, ab_c[ln:], 0.0), ab, incl)
    a24 = each(lambda ak_c, s, i: jnp.concatenate([jnp.where(s, ak_c[:ln], 0.0), jnp.where(i, ak_c[ln:], 0.0)], axis=0),
               ak, strict, incl)
    av = each(lambda a_c, v_c: _mm(_pieces(a_c, PASSES_INTRA), bd(v_c, PASSES_INTRA), _NN), a24, v)
    krn = each(lambda kr_c, n_c: _mm(kr_c[:PASSES_STATE], _pieces(n_c, PASSES_STATE), _NT), kr, n)
    u = each(lambda x_c, krn_c, av_c: -_mm(_pieces(x_c, PASSES_INV), bd(krn_c[:ln] + av_c[:ln], PASSES_INV), _NN),
             x, krn, av)
    y = each(lambda krn_c, a3_c, u_c, av_c: krn_c[ln:] + _mm(_pieces(a3_c, PASSES_STATE), bd(u_c, PASSES_STATE), _NN)
             + av_c[ln:], krn, a3, u, av)
    e_rest = each(lambda tot_c, cw_c: jnp.exp(tot_c - cw_c), tot, cw)
    upd = each(lambda u_c, v_c, b_c, kd_c, e: _mm(_pieces(jnp.concatenate([u_c, v_c], axis=0), PASSES_UPDATE),
                                                   _pieces(jnp.concatenate([b_c * e, kd_c * e], axis=0), PASSES_UPDATE),
                                                   _TN),
               u, v, b, kd, e_rest)
    n_new = each(lambda n_c, tot_c, upd_c: n_c * jnp.exp(tot_c) + jnp.where(same_head, upd_c, 0.0), n, tot, upd)
    return list(zip(y, n_new))


def _chunk_kernel(*refs, has_init, has_final):
    n_tok = 12 * SCAN_BATCHES
    tok_refs = [refs[k * 12:(k + 1) * 12] for k in range(SCAN_BATCHES)]
    refs = refs[n_tok:]
    s0 = refs[0] if has_init else None
    refs = refs[1:] if has_init else refs
    y_refs = refs[:2]
    sfin = refs[2] if has_final else None
    n_scr = refs[-1]
    c = pl.program_id(1)
    head_blocks = [(k, d, g, h) for k in range(SCAN_BATCHES) for d in range(2)
                   for g in range(N_HEADS // HEADS_PER_GROUP) for h in range(HEADS_PER_GROUP)]
    diag = lambda h: slice(h * HEAD, (h + 1) * HEAD)

    @pl.when(c == 0)
    def _():
        n_scr[...] = jnp.zeros(n_scr.shape, F32)
        if has_init:
            for k, d, g, h in head_blocks:
                n_scr[d, k, g, diag(h), diag(h)] = s0[k, d, g * HEADS_PER_GROUP + h]

    slots = [(k, d, g) for k in range(SCAN_BATCHES) for d in range(2) for g in range(N_HEADS // HEADS_PER_GROUP)]
    lanes = lambda g: slice(g * GROUP_W, (g + 1) * GROUP_W)
    masks = [_chunk_masks(reverse=False), _chunk_masks(reverse=True)]

    def operands(k, d, g):
        rf, kf, vf, rb, kb, vb, lwf, bf, kdf, lwb, bb, kdb = tok_refs[k]
        group = (rb, kb, vb, lwb, bb, kdb) if d else (rf, kf, vf, lwf, bf, kdf)
        return tuple(ref[:, lanes(g)] for ref in group)

    chains = [operands(k, d, g) + (n_scr[d, k, g], masks[d]) for k, d, g in slots]
    for (k, d, g), (y, n_new) in zip(slots, _chunk_steps(chains)):
        y_refs[d][k, :, lanes(g)] = y
        n_scr[d, k, g] = n_new

    if has_final:
        @pl.when(c == pl.num_programs(1) - 1)
        def _():
            for k, d, g, h in head_blocks:
                sfin[k, d, g * HEADS_PER_GROUP + h] = n_scr[d, k, g, diag(h), diag(h)]


def _chunk_scan(rkv, kap, dir_ops, s0, *, n_batch, t_len, want_final):
    n_chunks = t_len // CHUNK
    n_grp = N_HEADS // HEADS_PER_GROUP
    nb = SCAN_BATCHES

    def tok_spec(k, reverse, col=0):
        def index(bi, ci):
            return ((bi * nb + k) * n_chunks + (n_chunks - 1 - ci if reverse else ci), col)
        return pl.BlockSpec((CHUNK, RWKV_W), index)

    in_specs, args = [], []
    for k in range(nb):
        for rev in (False, True):
            in_specs += [tok_spec(k, rev, 0), tok_spec(k, rev), tok_spec(k, rev, 2)]
        in_specs += [tok_spec(k, False)] * 3 + [tok_spec(k, True)] * 3
        args += [rkv, kap, rkv, rkv, kap, rkv, *dir_ops[0], *dir_ops[1]]
    state = pl.BlockSpec((nb, 2, N_HEADS, HEAD, HEAD), lambda bi, ci: (bi, 0, 0, 0, 0))
    has_init = s0 is not None
    if has_init:
        in_specs.append(state)
        args.append(s0)
    y_shape = jax.ShapeDtypeStruct((n_batch, t_len, RWKV_W), F32)
    out_specs = [pl.BlockSpec((nb, CHUNK, RWKV_W), lambda bi, ci: (bi, ci, 0)),
                 pl.BlockSpec((nb, CHUNK, RWKV_W), lambda bi, ci: (bi, n_chunks - 1 - ci, 0))]
    out_shape = [y_shape, y_shape]
    if want_final:
        out_specs.append(state)
        out_shape.append(jax.ShapeDtypeStruct((n_batch, 2, N_HEADS, HEAD, HEAD), F32))
    outs = pl.pallas_call(
        functools.partial(_chunk_kernel, has_init=has_init, has_final=want_final),
        grid=(n_batch // nb, n_chunks),
        in_specs=in_specs,
        out_specs=out_specs,
        out_shape=out_shape,
        scratch_shapes=[pltpu.VMEM((2, nb, n_grp, GROUP_W, GROUP_W), F32)],
        compiler_params=_cparams(2),
        name=f"wkv_chunk_t{t_len}",
    )(*args)
    return [outs[0].reshape(n_batch * t_len, RWKV_W), outs[1].reshape(n_batch * t_len, RWKV_W)] + list(outs[2:])


def _postmix_kernel(yfp_ref, yfs_ref, ybp_ref, ybs_ref, xp_ref, xs_ref, convp_ref, convs_ref, bonusp_ref, bonuss_ref,
                    gp_ref, gs_ref, g1_ref, sh2_ref, sc2_ref, gng_ref, gnb_ref, ones_ref, wout_ref, n2g_ref, rhi_ref, rlo_ref,
                    x1_o, h2_o, logit_o):
    n_prompt = N_TOK // ROW_TILE
    ones = ones_ref[...]
    y = _pick(yfp_ref, yfs_ref, n_prompt) + _pick(ybp_ref, ybs_ref, n_prompt)
    mu = _head_sums(y, ones) * (1.0 / HEAD)
    d = y - mu
    var = _head_sums(d * d, ones) * (1.0 / HEAD)
    yn = d * lax.rsqrt(var + GN_EPS) * gng_ref[...] + gnb_ref[...]
    rw = ((yn + _pick(bonusp_ref, bonuss_ref, n_prompt)) * _pick(gp_ref, gs_ref, n_prompt)).astype(BF16)
    conv = _pick(convp_ref, convs_ref, n_prompt).astype(BF16)
    mix = (jnp.dot(conv, wout_ref[0:CONV_W, :], preferred_element_type=F32)
           + jnp.dot(rw, wout_ref[CONV_W:CONV_W + RWKV_W, :], preferred_element_type=F32))
    x1 = _pick(xp_ref, xs_ref, n_prompt) + g1_ref[0] * mix
    x1_o[...] = x1
    ms = jnp.mean(x1 * x1, axis=-1, keepdims=True)
    h2 = x1 * lax.rsqrt(ms + RMS_EPS) * n2g_ref[...]
    h2 = h2 * (1.0 + sc2_ref[0]) + sh2_ref[0]
    for c in range(TOK_ROWS):
        h2_o[pl.ds(c, ROW_TILE, stride=TOK_ROWS), :] = h2[:, c * 128:(c + 1) * 128]
    hi = h2.astype(BF16)
    lo = (h2 - hi.astype(F32)).astype(BF16)
    logit_o[...] = (jnp.dot(hi, rhi_ref[...], preferred_element_type=F32)
                    + jnp.dot(lo, rhi_ref[...], preferred_element_type=F32)
                    + jnp.dot(hi, rlo_ref[...], preferred_element_type=F32))


def _postmix(yf, yb, x, conv, bonus, g, mod3, gn_g, gn_b, ones_blk, w_out_bf16, norm2_g, r_hi, r_lo):
    n = 2 * N_TOK
    row = lambda c: pl.BlockSpec((ROW_TILE, c), lambda i: (i, 0))
    pair = lambda c: [pl.BlockSpec((ROW_TILE, c), m) for m in _group_tile(N_TOK // ROW_TILE)]
    full = lambda a: pl.BlockSpec(a.shape, lambda i: (0,) * a.ndim)
    modspec = lambda j: pl.BlockSpec((1, 1, D_MODEL), lambda i: (_cond_index(i), 0, j))
    return pl.pallas_call(
        _postmix_kernel,
        grid=(n // ROW_TILE,),
        in_specs=pair(RWKV_W) + pair(RWKV_W) + pair(D_MODEL) + pair(CONV_W) + pair(RWKV_W) + pair(RWKV_W) + [
            modspec(2), modspec(3), modspec(4), full(gn_g), full(gn_b), full(ones_blk),
            full(w_out_bf16), full(norm2_g), full(r_hi), full(r_lo)],
        out_specs=[row(D_MODEL), pl.BlockSpec((ROW_TILE * TOK_ROWS, 128), lambda i: (i, 0)), row(128)],
        out_shape=[jax.ShapeDtypeStruct((n, D_MODEL), F32), jax.ShapeDtypeStruct((n * TOK_ROWS, 128), F32),
                   jax.ShapeDtypeStruct((n, 128), F32)],
        compiler_params=_cparams(1),
        name="postmix",
    )(*yf, *yb, *x, *conv, *bonus, *g, mod3, mod3, mod3, gn_g, gn_b, ones_blk, w_out_bf16, norm2_g, r_hi, r_lo)


def _prefix_lanes(m, utri):
    rows = m.shape[0]
    off = jnp.zeros((rows, 1), F32)
    out, starts = [], []
    for c in range(N_TOK // 128):
        blk = m[:, c * 128:(c + 1) * 128]
        out.append(jnp.dot(blk.astype(BF16), utri, preferred_element_type=F32) + off)
        starts.append(off)
        off = off + jnp.sum(blk, axis=1, keepdims=True)
    return jnp.concatenate(out, axis=1), jnp.concatenate(starts, axis=1)


def _select_kernel(lt_ref, utri_ref, pos_o, aff_o, start_o):
    utri = utri_ref[...]
    affs = []
    for grp in range(2):
        x = lt_ref[grp * N_EXPERTS:(grp + 1) * N_EXPERTS, :]
        e = jnp.exp(x - jnp.max(x, axis=0, keepdims=True))
        affs.append(e / jnp.sum(e, axis=0, keepdims=True))
    aff = jnp.concatenate(affs, axis=0)
    aff_o[...] = aff
    n_rows = 2 * N_EXPERTS

    def body(_, lh):
        lo, hi = lh
        m2 = 0.5 * (lo + hi)
        m1 = 0.5 * (lo + m2)
        m3 = 0.5 * (m2 + hi)
        ge1, ge2, ge3 = (jnp.sum(jnp.where(aff > m, 1.0, 0.0), axis=1, keepdims=True) >= CAP for m in (m1, m2, m3))
        lo = jnp.where(ge3, m3, jnp.where(ge2, m2, jnp.where(ge1, m1, lo)))
        hi = jnp.where(ge3, hi, jnp.where(ge2, m3, jnp.where(ge1, m2, m1)))
        return lo, hi

    lo, hi = lax.fori_loop(0, BISECT_ITERS // 2, body,
                           (jnp.full((n_rows, 1), -1.0, F32), jnp.full((n_rows, 1), 1.0, F32)))
    gt = jnp.where(aff > hi, 1.0, 0.0)
    tie = jnp.where(aff > lo, 1.0, 0.0) - gt
    need = CAP - jnp.sum(gt, axis=1, keepdims=True)
    sel = gt + tie * jnp.where(_prefix_lanes(tie, utri)[0] < need, 1.0, 0.0)
    pos, starts = _prefix_lanes(sel, utri)
    pos_o[...] = jnp.where(sel > 0.5, pos, -1.0)
    start_o[...] = jnp.concatenate([starts, jnp.zeros((n_rows, 128 - N_TOK // 128), F32)], axis=1)


def _select(logits_t, utri):
    shp = jax.ShapeDtypeStruct((2 * N_EXPERTS, N_TOK), F32)
    full = lambda a: pl.BlockSpec(a.shape, lambda i: (0,) * a.ndim)
    return pl.pallas_call(
        _select_kernel,
        grid=(1,),
        in_specs=[full(logits_t), full(utri)],
        out_specs=[pl.BlockSpec(shp.shape, lambda i: (0, 0))] * 2 + [pl.BlockSpec((2 * N_EXPERTS, 128), lambda i: (0, 0))],
        out_shape=[shp, shp, jax.ShapeDtypeStruct((2 * N_EXPERTS, 128), F32)],
        compiler_params=_cparams(1),
        name="ec_select",
    )(logits_t, utri)


def _compact_kernel(start_ref, pos_ref, aff_ref, idx_o, gate_o, idx_acc, gate_acc):
    i = pl.program_id(0)
    n_blk = N_TOK // 128
    idx_acc[...] = jnp.zeros_like(idx_acc)
    gate_acc[...] = jnp.zeros_like(gate_acc)
    rel = lax.broadcasted_iota(jnp.int32, (COMPACT_WIN, 1), 0)
    for c in range(n_blk):
        base = pl.multiple_of(lax.shift_right_logical(start_ref[i * n_blk + c], 3) * 8, 8)
        slot = (rel + base).astype(F32)
        p = pos_ref[0, :, c * 128:(c + 1) * 128]
        a = aff_ref[0, :, c * 128:(c + 1) * 128]
        tok = (lax.broadcasted_iota(jnp.int32, (1, 128), 1) + c * 128).astype(F32)
        hit = p == slot
        idx_acc[pl.ds(base, COMPACT_WIN), :] = idx_acc[pl.ds(base, COMPACT_WIN), :] + jnp.where(hit, tok, 0.0)
        gate_acc[pl.ds(base, COMPACT_WIN), :] = gate_acc[pl.ds(base, COMPACT_WIN), :] + jnp.where(hit, a, 0.0)
    idx_o[0] = jnp.broadcast_to(jnp.sum(idx_acc[0:CAP, :], axis=1, keepdims=True), (CAP, 128))
    gate_o[0] = jnp.broadcast_to(jnp.sum(gate_acc[0:CAP, :], axis=1, keepdims=True), (CAP, 128))


def _compact(block_starts, pos3, aff3):
    rows = pos3.shape[0]
    shp = jax.ShapeDtypeStruct((rows, CAP, 128), F32)
    return pl.pallas_call(
        _compact_kernel,
        grid_spec=pltpu.PrefetchScalarGridSpec(
            num_scalar_prefetch=1,
            grid=(rows,),
            in_specs=[pl.BlockSpec((1, 1, N_TOK), lambda i, st: (i, 0, 0))] * 2,
            out_specs=[pl.BlockSpec((1, CAP, 128), lambda i, st: (i, 0, 0))] * 2,
            scratch_shapes=[pltpu.VMEM((CAP + COMPACT_WIN, 128), F32)] * 2),
        out_shape=[shp, shp],
        compiler_params=_cparams(1),
        name="ec_compact",
    )(block_starts, pos3, aff3)


def _gather_kernel(idx_ref, h_ref, o_ref, tile):
    grp = pl.program_id(0)
    eb = pl.program_id(1)
    for ei in range(ROUTE_EXPERTS):
        base = (grp * N_EXPERTS + eb * ROUTE_EXPERTS + ei) * CAP

        def tok(j8, c, base=base):
            for u in range(8):
                j = j8 * 8 + u
                row = pl.multiple_of(idx_ref[base + j] * 8, 8)
                tile[pl.ds(j, 8, stride=TOK_STRIDE), :] = h_ref[0, pl.ds(row, 8), :]
            return c

        lax.fori_loop(0, CAP // 8, tok, 0)
        for c in range(TOK_ROWS):
            o_ref[ei, :, c * 128:(c + 1) * 128] = tile[c * TOK_STRIDE:c * TOK_STRIDE + CAP, :].astype(BF16)


def _gather(idx_flat, h_rows):
    return pl.pallas_call(
        _gather_kernel,
        grid_spec=pltpu.PrefetchScalarGridSpec(
            num_scalar_prefetch=1,
            grid=(2, N_EXPERTS // ROUTE_EXPERTS),
            in_specs=[pl.BlockSpec((1, N_TOK * 8, 128), lambda g, e, idx: (g, 0, 0))],
            out_specs=pl.BlockSpec((ROUTE_EXPERTS, CAP, D_MODEL), lambda g, e, idx: (e, g, 0)),
            scratch_shapes=[pltpu.VMEM((8 * TOK_STRIDE, 128), F32)]),
        out_shape=jax.ShapeDtypeStruct((N_EXPERTS, 2 * CAP, D_MODEL), BF16),
        compiler_params=_cparams(2),
        name="ec_gather",
    )(idx_flat, h_rows)


def _expert_kernel(xe_ref, w1_ref, w3_ref, w2_ref, gate_p_ref, gate_s_ref, o_ref, hid_scr):
    s = pl.program_id(1)
    subs = [(lo, min(FF_SUB, FF_SLAB - lo)) for lo in range(0, FF_SLAB, FF_SUB)]

    @pl.when(s < FF_SPLIT)
    def _():
        x = xe_ref[0]
        for lo, width in subs:
            h1 = jnp.dot(x, w1_ref[0, :, lo:lo + width].astype(BF16), preferred_element_type=F32)
            h3 = jnp.dot(x, w3_ref[0, :, lo:lo + width].astype(BF16), preferred_element_type=F32)
            hid_scr[s, :, lo:lo + width] = (h1 * _sigmoid(h1) * h3).astype(BF16)

    @pl.when(s >= FF_SPLIT)
    def _():
        acc = None
        for lo, width in subs:
            part = jnp.dot(hid_scr[s - FF_SPLIT, :, lo:lo + width], w2_ref[0, lo:lo + width, :].astype(BF16),
                           preferred_element_type=F32)
            acc = part if acc is None else acc + part

        chunks = [(grp, c, slice(c * TOK_STRIDE, c * TOK_STRIDE + CAP)) for grp in range(2) for c in range(TOK_ROWS)]
        piece = lambda grp, c: acc[grp * CAP:(grp + 1) * CAP, c * 128:(c + 1) * 128]

        @pl.when(s == FF_SPLIT)
        def _():
            for grp, c, rows in chunks:
                o_ref[0, grp, rows, :] = piece(grp, c)
                o_ref[0, grp, c * TOK_STRIDE + CAP:(c + 1) * TOK_STRIDE, :] = jnp.zeros((TOK_STRIDE - CAP, 128), F32)

        @pl.when((s > FF_SPLIT) & (s < 2 * FF_SPLIT - 1))
        def _():
            for grp, c, rows in chunks:
                o_ref[0, grp, rows, :] = o_ref[0, grp, rows, :] + piece(grp, c)

        @pl.when(s == 2 * FF_SPLIT - 1)
        def _():
            gates = (gate_p_ref[0][:, 0:1], gate_s_ref[0][:, 0:1])
            for grp, c, rows in chunks:
                o_ref[0, grp, rows, :] = (o_ref[0, grp, rows, :] + piece(grp, c)) * gates[grp]


def _experts(xe, w1, w3, w2, gate_rows):
    m = xe.shape[1]
    up = lambda e, s: (e, 0, jnp.minimum(s, FF_SPLIT - 1))
    down = lambda e, s: (e, jnp.maximum(s - FF_SPLIT, 0), 0)
    return pl.pallas_call(
        _expert_kernel,
        grid=(N_EXPERTS, 2 * FF_SPLIT),
        in_specs=[pl.BlockSpec((1, m, D_MODEL), lambda e, s: (e, 0, 0)),
                  pl.BlockSpec((1, D_MODEL, FF_SLAB), up),
                  pl.BlockSpec((1, D_MODEL, FF_SLAB), up),
                  pl.BlockSpec((1, FF_SLAB, D_MODEL), down),
                  pl.BlockSpec((1, CAP, 128), lambda e, s: (e, 0, 0)),
                  pl.BlockSpec((1, CAP, 128), lambda e, s: (N_EXPERTS + e, 0, 0))],
        out_specs=pl.BlockSpec((1, 2, TOK_ROWS * TOK_STRIDE, 128), lambda e, s: (e, 0, 0, 0)),
        out_shape=jax.ShapeDtypeStruct((N_EXPERTS, 2, TOK_ROWS * TOK_STRIDE, 128), F32),
        scratch_shapes=[pltpu.VMEM((FF_SPLIT, m, FF_SLAB), BF16)],
        compiler_params=_cparams(2, vmem=EXPERT_VMEM_LIMIT),
        name="ec_experts",
    )(xe, w1, w3, w2, gate_rows, gate_rows)


def _combine_kernel(idx_ref, ye_ref, x1_ref, g2_ref, fg_ref, op_ref, os_ref, acc):
    grp = pl.program_id(0)
    s = pl.program_id(1)
    n_scatter = N_EXPERTS // ROUTE_EXPERTS

    @pl.when(s == 0)
    def _():
        acc[...] = jnp.zeros_like(acc)

    @pl.when(s < n_scatter)
    def _():
        for ei in range(ROUTE_EXPERTS):
            base = (grp * N_EXPERTS + s * ROUTE_EXPERTS + ei) * CAP
            def tok(jb, c, base=base):
                rows, vals = [], []
                for u in range(SCATTER_UNROLL):
                    j = jb * SCATTER_UNROLL + u
                    row = pl.multiple_of(idx_ref[base + j] * 8, 8)
                    rows.append(row)
                    vals.append(acc[pl.ds(row, 8), :] + ye_ref[ei, 0, pl.ds(j, 8, stride=TOK_STRIDE), :])
                for row, val in zip(rows, vals):
                    acc[pl.ds(row, 8), :] = val
                return c

            lax.fori_loop(0, CAP // SCATTER_UNROLL, tok, 0)

    @pl.when(s >= n_scatter)
    def _():
        row0 = (s - n_scatter) * (ROW_TILE * TOK_ROWS)
        y = jnp.concatenate([acc[pl.ds(row0 + c, ROW_TILE, stride=TOK_ROWS), :] for c in range(TOK_ROWS)], axis=1)
        x = x1_ref[...] + g2_ref[0] * y
        ms = jnp.mean(x * x, axis=-1, keepdims=True)
        out = x * lax.rsqrt(ms + RMS_EPS) * fg_ref[...]

        @pl.when(grp == 0)
        def _():
            op_ref[...] = out

        @pl.when(grp == 1)
        def _():
            os_ref[...] = out


def _combine_final(idx_flat, ye, x1, mod3, final_g):
    n_scatter = N_EXPERTS // ROUTE_EXPERTS
    tiles = N_TOK // ROW_TILE
    tile_of = lambda g, s: g * tiles + jnp.maximum(s - n_scatter, 0)
    shp = jax.ShapeDtypeStruct((N_TOK, D_MODEL), F32)
    return pl.pallas_call(
        _combine_kernel,
        grid_spec=pltpu.PrefetchScalarGridSpec(
            num_scalar_prefetch=1,
            grid=(2, n_scatter + tiles),
            in_specs=[pl.BlockSpec((ROUTE_EXPERTS, 1, TOK_ROWS * TOK_STRIDE, 128),
                                   lambda g, s, idx: (jnp.minimum(s, n_scatter - 1), g, 0, 0)),
                      pl.BlockSpec((ROW_TILE, D_MODEL), lambda g, s, idx: (tile_of(g, s), 0)),
                      pl.BlockSpec((1, 1, D_MODEL), lambda g, s, idx: (_cond_index(tile_of(g, s)), 0, 5)),
                      pl.BlockSpec((1, D_MODEL), lambda g, s, idx: (0, 0))],
            out_specs=[pl.BlockSpec((ROW_TILE, D_MODEL),
                                    lambda g, s, idx: (jnp.where(g == 0, jnp.maximum(s - n_scatter, 0), tiles - 1), 0)),
                       pl.BlockSpec((ROW_TILE, D_MODEL),
                                    lambda g, s, idx: (jnp.where(g == 1, jnp.maximum(s - n_scatter, 0), 0), 0))],
            scratch_shapes=[pltpu.VMEM((N_TOK * TOK_ROWS, 128), F32)]),
        out_shape=[shp, shp],
        compiler_params=_cparams(2),
        name="ec_combine_final",
    )(idx_flat, ye, x1, mod3, final_g)


def kernel(x_prompt, x_sample, state_wkv, c, c_ctx, norm1_g, norm2_g, w_mod, b_mod, w_in, conv_dw, conv_b, conv_ln_g,
           conv_ln_b, rwkv_w0, rwkv_w_up, rwkv_a0, rwkv_a_up, rwkv_g_up, rwkv_k_k, rwkv_k_a, rwkv_r_k, rwkv_gn_g,
           rwkv_gn_b, w_out, router, exp_w1, exp_w3, exp_w2, final_norm_g):
    row2 = lambda a: a.reshape(1, -1)
    x = (x_prompt.reshape(N_TOK, D_MODEL), x_sample.reshape(N_TOK, D_MODEL))
    cond8 = jnp.concatenate([c_ctx[None, :], c, jnp.zeros((8 - 1 - DEC_BATCH, D_MODEL), F32)], axis=0)

    mod = _modulation(cond8, w_mod[0], b_mod)
    mod3 = mod.reshape(8, 1, 6 * D_MODEL)

    lane_head = jnp.arange(GROUP_W, dtype=jnp.int32) // HEAD
    ones_blk = (lane_head[:, None] == lane_head[None, :]).astype(BF16)

    def per_dir_rows(up):
        z = jnp.zeros_like(up[0])
        return jnp.stack([jnp.concatenate([up[0], z], axis=0), jnp.concatenate([z, up[1]], axis=0)]).astype(BF16)
    mixer_consts = (mod3, norm1_g, w_in[0].astype(BF16), (conv_dw[0], conv_b, conv_ln_g, conv_ln_b),
                    (ones_blk, rwkv_w0[0], per_dir_rows(rwkv_w_up[0]), rwkv_a0[0], per_dir_rows(rwkv_a_up[0]),
                     rwkv_g_up[0].astype(BF16), rwkv_k_k, rwkv_k_a, row2(rwkv_r_k[0])))
    groups = [_mixer_in(x[0], *mixer_consts, seg=SEQ, tile0=0),
              _mixer_in(x[1], *mixer_consts, seg=GRID_W, tile0=N_TOK // ROW_TILE)]
    conv, rkv, kap, w0, w1, b0, b1, kd0, kd1, g, bonus = zip(*groups)

    dir_ops = lambda i: ((w0[i], b0[i], kd0[i]), (w1[i], b1[i], kd1[i]))
    yp_f, yp_b, new_state = _chunk_scan(rkv[0], kap[0], dir_ops(0), None, n_batch=BATCH, t_len=SEQ, want_final=True)
    ys_f, ys_b = _chunk_scan(rkv[1], kap[1], dir_ops(1), state_wkv[:, 0], n_batch=DEC_BATCH, t_len=DEC_SEQ,
                             want_final=False)

    r_pad = jnp.pad(router[0], ((0, 0), (0, 128 - N_EXPERTS)))
    r_hi = r_pad.astype(BF16)
    r_lo = (r_pad - r_hi.astype(F32)).astype(BF16)
    x1, h2_rows, logits = _postmix((yp_f, ys_f), (yp_b, ys_b), x, conv, bonus, g, mod3, rwkv_gn_g, rwkv_gn_b, ones_blk,
                                   w_out[0].astype(BF16), norm2_g, r_hi, r_lo)

    logits_t = logits[:, :N_EXPERTS].reshape(2, N_TOK, N_EXPERTS).transpose(0, 2, 1).reshape(2 * N_EXPERTS, N_TOK)
    lane = jnp.arange(128, dtype=jnp.int32)
    utri = (lane[:, None] < lane[None, :]).astype(BF16)
    pos, aff, starts = _select(logits_t, utri)
    block_starts = starts[:, :N_TOK // 128].astype(jnp.int32).reshape(-1)
    idx_c, gate_c = _compact(block_starts, pos.reshape(2 * N_EXPERTS, 1, N_TOK), aff.reshape(2 * N_EXPERTS, 1, N_TOK))
    idx_flat = idx_c[:, :, 0].astype(jnp.int32).reshape(-1)

    xe = _gather(idx_flat, h2_rows.reshape(2, N_TOK * TOK_ROWS, 128))
    ye = _experts(xe, exp_w1[0], exp_w3[0], exp_w2[0], gate_c)
    out_p, out_s = _combine_final(idx_flat, ye, x1, mod3, row2(final_norm_g))
    return (out_p.reshape(BATCH, SEQ, D_MODEL), out_s.reshape(DEC_BATCH, DEC_SEQ, D_MODEL),
            new_state.reshape(BATCH, 1, 2, N_HEADS, HEAD, HEAD))
```

```python
import functools

import jax
import jax.numpy as jnp
from jax import lax
from jax.experimental import pallas as pl
from jax.experimental.pallas import tpu as pltpu

F32 = jnp.float32
BF16 = jnp.bfloat16
LANES = 128
SUBLANES = 8

D_MODEL = 1024
BATCH = 16
SEQ = 256
DEC_BATCH = 4
DEC_SEQ = 1024
GRID_W = 64
CONV_W = 512
CONV_K = 31
CONV_PAD = 15
CONV_GAP = 2 * SUBLANES
RWKV_W = 512
HEAD = 64
N_HEADS = 8
LORA = 64
GATE_LORA = 128
N_EXPERTS = 16
D_FF = 2816
D_IN = 2944
CAP = 512
N_TOK = 4096
RMS_EPS = 1e-6
LN_EPS = 1e-5
GN_EPS = 64e-5

ROW_TILE = 512
FF_SPLIT = 2
FF_SLAB = D_FF // FF_SPLIT
FF_SUB = 256
EXPERT_VMEM_LIMIT = 60 * 1024 * 1024
CHUNK = 64
HEADS_PER_GROUP = 4
GROUP_W = HEADS_PER_GROUP * HEAD
PASSES_INTRA = 1
PASSES_INV = 1
PASSES_STATE = 1
PASSES_UPDATE = 3
SCAN_BATCHES = 4
TOK_ROWS = D_MODEL // LANES
TOK_STRIDE = CAP + SUBLANES
ROUTE_EXPERTS = 4
SCATTER_UNROLL = 8
COMPACT_WIN = LANES + SUBLANES
BISECT_ITERS = 160
VMEM_LIMIT = 56 * 1024 * 1024


def _cparams(n_axes, vmem=None):
    return pltpu.CompilerParams(dimension_semantics=("arbitrary",) * n_axes,
                                vmem_limit_bytes=vmem or VMEM_LIMIT)


def _cond_index(i):
    n_prompt = (BATCH * SEQ) // ROW_TILE
    per_batch = DEC_SEQ // ROW_TILE
    return jnp.where(i < n_prompt, 0, 1 + (i - n_prompt) // per_batch)


def _split_dot(x, w_bf16):
    hi = x.astype(BF16)
    lo = (x - hi.astype(F32)).astype(BF16)
    return (jnp.dot(hi, w_bf16, preferred_element_type=F32)
            + jnp.dot(lo, w_bf16, preferred_element_type=F32))


def _head_sums(x, ones_bf16):
    return jnp.concatenate([_split_dot(x[:, c:c + GROUP_W], ones_bf16) for c in range(0, x.shape[1], GROUP_W)],
                           axis=1)


def _sigmoid(x):
    return 1.0 / (1.0 + jnp.exp(-x))


def _mod_kernel(c_ref, w_ref, b_ref, o_ref):
    c = c_ref[...]
    s = (c * _sigmoid(c)).astype(BF16)
    o_ref[...] = jnp.dot(s, w_ref[...].astype(BF16), preferred_element_type=F32) + b_ref[...]


def _modulation(cond8, w_mod, b_mod):
    n = 6 * D_MODEL
    tn = D_MODEL
    return pl.pallas_call(
        _mod_kernel,
        grid=(n // tn,),
        in_specs=[pl.BlockSpec((8, D_MODEL), lambda j: (0, 0)),
                  pl.BlockSpec((D_MODEL, tn), lambda j: (0, j)),
                  pl.BlockSpec((1, tn), lambda j: (0, j))],
        out_specs=pl.BlockSpec((8, tn), lambda j: (0, j)),
        out_shape=jax.ShapeDtypeStruct((8, n), F32),
        compiler_params=_cparams(1),
        name="modulation",
    )(cond8, w_mod, b_mod)


def _group_tile(n_prompt_tiles):
    return (lambda i: (jnp.minimum(i, n_prompt_tiles - 1), 0)), (lambda i: (jnp.maximum(i - n_prompt_tiles, 0), 0))


def _pick(prompt_ref, sample_ref, n_prompt_tiles):
    return jnp.where(pl.program_id(0) < n_prompt_tiles, prompt_ref[...], sample_ref[...])


def _mixer_in_kernel(x_ref, sh_ref, sc_ref, n1g_ref, w_ref, dw_ref, db_ref, lng_ref, lnb_ref, ones_ref,
                     w0_ref, wup_ref, a0_ref, aup_ref, gup_ref, kk_ref, ka_ref, rk_ref,
                     conv_o, rkv_o, kap_o, lw0_o, lw1_o, b0_o, b1_o, kd0_o, kd1_o, g_o, bonus_o, *, seg):
    x = x_ref[...]
    ms = jnp.mean(x * x, axis=-1, keepdims=True)
    h = x * lax.rsqrt(ms + RMS_EPS) * n1g_ref[...]
    h = (h * (1.0 + sc_ref[0]) + sh_ref[0]).astype(BF16)
    uv = jnp.dot(h, w_ref[:, 0:CONV_W], preferred_element_type=F32)
    ug = jnp.dot(h, w_ref[:, CONV_W:2 * CONV_W], preferred_element_type=F32)
    conv_o[...] = _conv_ln_silu(uv * _sigmoid(ug), seg, dw_ref, db_ref, lng_ref, lnb_ref)
    rkv = jnp.dot(h, w_ref[:, 2 * CONV_W:2 * CONV_W + 3 * RWKV_W], preferred_element_type=F32)
    rkv_o[...] = rkv
    lora = jnp.dot(h, w_ref[:, 2 * CONV_W + 3 * RWKV_W:D_IN], preferred_element_type=F32)
    _rwkv_operands(rkv, lora, ones_ref, w0_ref, wup_ref, a0_ref, aup_ref, gup_ref, kk_ref, ka_ref, rk_ref,
                   kap_o, (lw0_o, lw1_o), (b0_o, b1_o), (kd0_o, kd1_o), g_o, bonus_o)


def _mixer_in(x, mod3, norm1_g, w_in_bf16, conv_params, rwkv_params, *, seg, tile0):
    row = lambda c: pl.BlockSpec((ROW_TILE, c), lambda i: (i, 0))
    full = lambda a: pl.BlockSpec(a.shape, lambda i: (0,) * a.ndim)
    consts = (norm1_g, w_in_bf16) + tuple(conv_params) + tuple(rwkv_params)
    widths = [CONV_W, 3 * RWKV_W] + [RWKV_W] * 9
    return pl.pallas_call(
        functools.partial(_mixer_in_kernel, seg=seg),
        grid=(N_TOK // ROW_TILE,),
        in_specs=[row(D_MODEL),
                  pl.BlockSpec((1, 1, D_MODEL), lambda i: (_cond_index(tile0 + i), 0, 0)),
                  pl.BlockSpec((1, 1, D_MODEL), lambda i: (_cond_index(tile0 + i), 0, 1)),
                  ] + [full(a) for a in consts],
        out_specs=[row(c) for c in widths],
        out_shape=[jax.ShapeDtypeStruct((N_TOK, c), F32) for c in widths],
        compiler_params=_cparams(1),
        name=f"mixer_in_seg{seg}",
    )(x, mod3, mod3, *consts)


def _conv_ln_silu(u, seg, dw_ref, db_ref, g_ref, b_ref):
    n_seg = u.shape[0] // seg
    pitch = seg + CONV_GAP
    gap = jnp.zeros((CONV_GAP, u.shape[1]), F32)
    up = jnp.concatenate([piece for i in range(n_seg) for piece in (u[i * seg:(i + 1) * seg], gap)], axis=0)
    acc = up * dw_ref[CONV_PAD:CONV_PAD + 1, :]
    for k in range(CONV_K):
        s = k - CONV_PAD
        if s != 0:
            acc = acc + pltpu.roll(up, (-s) % up.shape[0], 0) * dw_ref[k:k + 1, :]
    out = jnp.concatenate([acc[i * pitch:i * pitch + seg] for i in range(n_seg)], axis=0) + db_ref[...]
    mu = jnp.mean(out, axis=-1, keepdims=True)
    d = out - mu
    var = jnp.mean(d * d, axis=-1, keepdims=True)
    out = d * lax.rsqrt(var + LN_EPS) * g_ref[...] + b_ref[...]
    return out * _sigmoid(out)


def _rwkv_operands(rkv, lora, ones_ref, w0_ref, wup_ref, a0_ref, aup_ref, gup_ref, kk_ref, ka_ref, rk_ref,
                   kap_o, lw_o, b_o, kd_o, g_o, bonus_o):
    r = rkv[:, 0:RWKV_W]
    k = rkv[:, RWKV_W:2 * RWKV_W]
    v = rkv[:, 2 * RWKV_W:3 * RWKV_W]
    xw = jnp.tanh(lora[:, 0:2 * LORA]).astype(BF16)
    xa = lora[:, 2 * LORA:4 * LORA].astype(BF16)
    xg = _sigmoid(lora[:, 4 * LORA:4 * LORA + GATE_LORA]).astype(BF16)
    ones = ones_ref[...]

    kk0 = k * kk_ref[...]
    ss = _head_sums(kk0 * kk0, ones)
    kap = kk0 * lax.rsqrt(jnp.maximum(ss, 1e-24))
    kap_o[...] = kap
    g_o[...] = jnp.dot(xg, gup_ref[...], preferred_element_type=F32)

    kd_sum = jnp.zeros_like(k)
    for d in range(2):
        wl = w0_ref[d:d + 1, :] + jnp.dot(xw, wup_ref[d], preferred_element_type=F32)
        lw_o[d][...] = -jnp.exp(-0.5) * _sigmoid(wl)
        a = _sigmoid(a0_ref[d:d + 1, :] + jnp.dot(xa, aup_ref[d], preferred_element_type=F32))
        b_o[d][...] = kap * a
        kd = k * (1.0 + (a - 1.0) * ka_ref[...])
        kd_o[d][...] = kd
        kd_sum = kd_sum + kd
    bonus_o[...] = _head_sums(r * kd_sum * rk_ref[...], ones) * v


def _pieces(x, passes):
    hi = x.astype(BF16)
    return (hi,) if passes == 1 else (hi, (x - hi.astype(F32)).astype(BF16))


_NN = ((1,), (0,))
_NT = ((1,), (1,))
_TN = ((0,), (0,))


def _mm(a, b, dims):
    dg = lambda p, q: lax.dot_general(p, q, (dims, ((), ())), preferred_element_type=F32)
    if len(a) == 1:
        out = dg(a[0], b[0])
    else:
        free = 1 - dims[0][0]
        both = dg(jnp.concatenate(a, axis=free), b[0])
        m = a[0].shape[free]
        out = both[:m] + both[m:]
    if len(b) > 1:
        out = out + dg(a[0], b[1])
    return out


def _block_diag(pieces, head_masks):
    return tuple(jnp.concatenate([p * m for m in head_masks], axis=0) for p in pieces)


def _chunk_masks(reverse):
    row = lax.broadcasted_iota(jnp.int32, (CHUNK, GROUP_W), 0)
    col = lax.broadcasted_iota(jnp.int32, (CHUNK, GROUP_W), 1) % CHUNK
    strict = (col > row) if reverse else (col < row)
    incl = (col >= row) if reverse else (col <= row)
    eye_side = jnp.where(col == row, 1.0, 0.0)
    tri = jnp.where(incl[:, :CHUNK], 1.0, 0.0).astype(BF16)
    return strict, incl, eye_side, tri, (0 if reverse else CHUNK - 1)


def _chunk_steps(chains):
    ln = CHUNK
    each = lambda f, *cols: [f(*args) for args in zip(*cols)]
    r, kap, v, lw, b, kd, n, masks = (list(col) for col in zip(*chains))
    strict = [m[0] for m in masks]
    incl = [m[1] for m in masks]
    lane_blk = lax.broadcasted_iota(jnp.int32, (ln, GROUP_W), 1) // HEAD
    head_masks = [jnp.where(lane_blk == h, 1.0, 0.0).astype(BF16) for h in range(HEADS_PER_GROUP)]
    bd = lambda x, passes: _block_diag(_pieces(x, passes), head_masks)
    r2 = lax.broadcasted_iota(jnp.int32, (GROUP_W, GROUP_W), 0) // HEAD
    c2 = lax.broadcasted_iota(jnp.int32, (GROUP_W, GROUP_W), 1) // HEAD
    same_head = r2 == c2

    def running_sum(lw_c, m):
        return _mm((m[3],), _pieces(lw_c, 3), _NN)

    cw = each(running_sum, lw, masks)
    tot = each(lambda cw_c, m: cw_c[m[4]:m[4] + 1, :], cw, masks)
    e_neg = each(lambda cw_c: jnp.exp(-cw_c), cw)
    kr = each(lambda kap_c, r_c, cw_c, lw_c: _pieces(
        jnp.concatenate([kap_c * jnp.exp(cw_c - lw_c), r_c * jnp.exp(cw_c)], axis=0), PASSES_INTRA),
        kap, r, cw, lw)
    ab = each(lambda kr_c, b_c, e: _mm(kr_c, bd(b_c * e, PASSES_INTRA), _NT), kr, b, e_neg)
    ak = each(lambda kr_c, kd_c, e: _mm(kr_c, bd(kd_c * e, PASSES_INTRA), _NT), kr, kd, e_neg)

    q = each(lambda ab_c, s: jnp.where(s, -ab_c[:ln], 0.0), ab, strict)
    x = each(lambda q_c, m: m[2] + q_c, q, masks)
    q = each(lambda q_c: _mm(_pieces(q_c, PASSES_INV), bd(q_c, PASSES_INV), _NN), q)
    n_rounds = CHUNK.bit_length() - 2
    for j in range(1, n_rounds + 1):
        last_round = j == n_rounds
        lhs = x if last_round else each(lambda q_c, x_c: jnp.concatenate([q_c, x_c], axis=0), q, x)
        prod = each(lambda l_c, q_c: _mm(_pieces(l_c, PASSES_INV), bd(q_c, PASSES_INV), _NN), lhs, q)
        if last_round:
            x = each(lambda x_c, p_c: x_c + p_c, x, prod)
        else:
            q = each(lambda p_c: p_c[:ln], prod)
            x = each(lambda x_c, p_c: x_c + p_c[ln:], x, prod)

    a3 = each(lambda ab_c, i: jnp.where(i, ab_c[ln:], 0.0), ab, incl)
    a24 = each(lambda ak_c, s, i: jnp.concatenate([jnp.where(s, ak_c[:ln], 0.0), jnp.where(i, ak_c[ln:], 0.0)], axis=0),
               ak, strict, incl)
    av = each(lambda a_c, v_c: _mm(_pieces(a_c, PASSES_INTRA), bd(v_c, PASSES_INTRA), _NN), a24, v)
    krn = each(lambda kr_c, n_c: _mm(kr_c[:PASSES_STATE], _pieces(n_c, PASSES_STATE), _NT), kr, n)
    u = each(lambda x_c, krn_c, av_c: -_mm(_pieces(x_c, PASSES_INV), bd(krn_c[:ln] + av_c[:ln], PASSES_INV), _NN),
             x, krn, av)
    y = each(lambda krn_c, a3_c, u_c, av_c: krn_c[ln:] + _mm(_pieces(a3_c, PASSES_STATE), bd(u_c, PASSES_STATE), _NN)
             + av_c[ln:], krn, a3, u, av)
    e_rest = each(lambda tot_c, cw_c: jnp.exp(tot_c - cw_c), tot, cw)
    upd = each(lambda u_c, v_c, b_c, kd_c, e: _mm(_pieces(jnp.concatenate([u_c, v_c], axis=0), PASSES_UPDATE),
                                                   _pieces(jnp.concatenate([b_c * e, kd_c * e], axis=0), PASSES_UPDATE),
                                                   _TN),
               u, v, b, kd, e_rest)
    n_new = each(lambda n_c, tot_c, upd_c: n_c * jnp.exp(tot_c) + jnp.where(same_head, upd_c, 0.0), n, tot, upd)
    return list(zip(y, n_new))


def _chunk_kernel(*refs, has_init, has_final):
    n_tok = 12 * SCAN_BATCHES
    tok_refs = [refs[k * 12:(k + 1) * 12] for k in range(SCAN_BATCHES)]
    refs = refs[n_tok:]
    s0 = refs[0] if has_init else None
    refs = refs[1:] if has_init else refs
    y_refs = refs[:2]
    sfin = refs[2] if has_final else None
    n_scr = refs[-1]
    c = pl.program_id(1)
    head_blocks = [(k, d, g, h) for k in range(SCAN_BATCHES) for d in range(2)
                   for g in range(N_HEADS // HEADS_PER_GROUP) for h in range(HEADS_PER_GROUP)]
    diag = lambda h: slice(h * HEAD, (h + 1) * HEAD)

    @pl.when(c == 0)
    def _():
        n_scr[...] = jnp.zeros(n_scr.shape, F32)
        if has_init:
            for k, d, g, h in head_blocks:
                n_scr[d, k, g, diag(h), diag(h)] = s0[k, d, g * HEADS_PER_GROUP + h]

    slots = [(k, d, g) for k in range(SCAN_BATCHES) for d in range(2) for g in range(N_HEADS // HEADS_PER_GROUP)]
    lanes = lambda g: slice(g * GROUP_W, (g + 1) * GROUP_W)
    masks = [_chunk_masks(reverse=False), _chunk_masks(reverse=True)]

    def operands(k, d, g):
        rf, kf, vf, rb, kb, vb, lwf, bf, kdf, lwb, bb, kdb = tok_refs[k]
        group = (rb, kb, vb, lwb, bb, kdb) if d else (rf, kf, vf, lwf, bf, kdf)
        return tuple(ref[:, lanes(g)] for ref in group)

    chains = [operands(k, d, g) + (n_scr[d, k, g], masks[d]) for k, d, g in slots]
    for (k, d, g), (y, n_new) in zip(slots, _chunk_steps(chains)):
        y_refs[d][k, :, lanes(g)] = y
        n_scr[d, k, g] = n_new

    if has_final:
        @pl.when(c == pl.num_programs(1) - 1)
        def _():
            for k, d, g, h in head_blocks:
                sfin[k, d, g * HEADS_PER_GROUP + h] = n_scr[d, k, g, diag(h), diag(h)]


def _chunk_scan(rkv, kap, dir_ops, s0, *, n_batch, t_len, want_final):
    n_chunks = t_len // CHUNK
    n_grp = N_HEADS // HEADS_PER_GROUP
    nb = SCAN_BATCHES

    def tok_spec(k, reverse, col=0):
        def index(bi, ci):
            return ((bi * nb + k) * n_chunks + (n_chunks - 1 - ci if reverse else ci), col)
        return pl.BlockSpec((CHUNK, RWKV_W), index)

    in_specs, args = [], []
    for k in range(nb):
        for rev in (False, True):
            in_specs += [tok_spec(k, rev, 0), tok_spec(k, rev), tok_spec(k, rev, 2)]
        in_specs += [tok_spec(k, False)] * 3 + [tok_spec(k, True)] * 3
        args += [rkv, kap, rkv, rkv, kap, rkv, *dir_ops[0], *dir_ops[1]]
    state = pl.BlockSpec((nb, 2, N_HEADS, HEAD, HEAD), lambda bi, ci: (bi, 0, 0, 0, 0))
    has_init = s0 is not None
    if has_init:
        in_specs.append(state)
        args.append(s0)
    y_shape = jax.ShapeDtypeStruct((n_batch, t_len, RWKV_W), F32)
    out_specs = [pl.BlockSpec((nb, CHUNK, RWKV_W), lambda bi, ci: (bi, ci, 0)),
                 pl.BlockSpec((nb, CHUNK, RWKV_W), lambda bi, ci: (bi, n_chunks - 1 - ci, 0))]
    out_shape = [y_shape, y_shape]
    if want_final:
        out_specs.append(state)
        out_shape.append(jax.ShapeDtypeStruct((n_batch, 2, N_HEADS, HEAD, HEAD), F32))
    outs = pl.pallas_call(
        functools.partial(_chunk_kernel, has_init=has_init, has_final=want_final),
        grid=(n_batch // nb, n_chunks),
        in_specs=in_specs,
        out_specs=out_specs,
        out_shape=out_shape,
        scratch_shapes=[pltpu.VMEM((2, nb, n_grp, GROUP_W, GROUP_W), F32)],
        compiler_params=_cparams(2),
        name=f"wkv_chunk_t{t_len}",
    )(*args)
    return [outs[0].reshape(n_batch * t_len, RWKV_W), outs[1].reshape(n_batch * t_len, RWKV_W)] + list(outs[2:])


def _postmix_kernel(yfp_ref, yfs_ref, ybp_ref, ybs_ref, xp_ref, xs_ref, convp_ref, convs_ref, bonusp_ref, bonuss_ref,
                    gp_ref, gs_ref, g1_ref, sh2_ref, sc2_ref, gng_ref, gnb_ref, ones_ref, wout_ref, n2g_ref, rhi_ref, rlo_ref,
                    x1_o, h2_o, logit_o):
    n_prompt = N_TOK // ROW_TILE
    ones = ones_ref[...]
    y = _pick(yfp_ref, yfs_ref, n_prompt) + _pick(ybp_ref, ybs_ref, n_prompt)
    mu = _head_sums(y, ones) * (1.0 / HEAD)
    d = y - mu
    var = _head_sums(d * d, ones) * (1.0 / HEAD)
    yn = d * lax.rsqrt(var + GN_EPS) * gng_ref[...] + gnb_ref[...]
    rw = ((yn + _pick(bonusp_ref, bonuss_ref, n_prompt)) * _pick(gp_ref, gs_ref, n_prompt)).astype(BF16)
    conv = _pick(convp_ref, convs_ref, n_prompt).astype(BF16)
    mix = (jnp.dot(conv, wout_ref[0:CONV_W, :], preferred_element_type=F32)
           + jnp.dot(rw, wout_ref[CONV_W:CONV_W + RWKV_W, :], preferred_element_type=F32))
    x1 = _pick(xp_ref, xs_ref, n_prompt) + g1_ref[0] * mix
    x1_o[...] = x1
    ms = jnp.mean(x1 * x1, axis=-1, keepdims=True)
    h2 = x1 * lax.rsqrt(ms + RMS_EPS) * n2g_ref[...]
    h2 = h2 * (1.0 + sc2_ref[0]) + sh2_ref[0]
    for c in range(TOK_ROWS):
        h2_o[pl.ds(c, ROW_TILE, stride=TOK_ROWS), :] = h2[:, c * 128:(c + 1) * 128]
    hi = h2.astype(BF16)
    lo = (h2 - hi.astype(F32)).astype(BF16)
    logit_o[...] = (jnp.dot(hi, rhi_ref[...], preferred_element_type=F32)
                    + jnp.dot(lo, rhi_ref[...], preferred_element_type=F32)
                    + jnp.dot(hi, rlo_ref[...], preferred_element_type=F32))


def _postmix(yf, yb, x, conv, bonus, g, mod3, gn_g, gn_b, ones_blk, w_out_bf16, norm2_g, r_hi, r_lo):
    n = 2 * N_TOK
    row = lambda c: pl.BlockSpec((ROW_TILE, c), lambda i: (i, 0))
    pair = lambda c: [pl.BlockSpec((ROW_TILE, c), m) for m in _group_tile(N_TOK // ROW_TILE)]
    full = lambda a: pl.BlockSpec(a.shape, lambda i: (0,) * a.ndim)
    modspec = lambda j: pl.BlockSpec((1, 1, D_MODEL), lambda i: (_cond_index(i), 0, j))
    return pl.pallas_call(
        _postmix_kernel,
        grid=(n // ROW_TILE,),
        in_specs=pair(RWKV_W) + pair(RWKV_W) + pair(D_MODEL) + pair(CONV_W) + pair(RWKV_W) + pair(RWKV_W) + [
            modspec(2), modspec(3), modspec(4), full(gn_g), full(gn_b), full(ones_blk),
            full(w_out_bf16), full(norm2_g), full(r_hi), full(r_lo)],
        out_specs=[row(D_MODEL), pl.BlockSpec((ROW_TILE * TOK_ROWS, 128), lambda i: (i, 0)), row(128)],
        out_shape=[jax.ShapeDtypeStruct((n, D_MODEL), F32), jax.ShapeDtypeStruct((n * TOK_ROWS, 128), F32),
                   jax.ShapeDtypeStruct((n, 128), F32)],
        compiler_params=_cparams(1),
        name="postmix",
    )(*yf, *yb, *x, *conv, *bonus, *g, mod3, mod3, mod3, gn_g, gn_b, ones_blk, w_out_bf16, norm2_g, r_hi, r_lo)


def _prefix_lanes(m, utri):
    rows = m.shape[0]
    off = jnp.zeros((rows, 1), F32)
    out, starts = [], []
    for c in range(N_TOK // 128):
        blk = m[:, c * 128:(c + 1) * 128]
        out.append(jnp.dot(blk.astype(BF16), utri, preferred_element_type=F32) + off)
        starts.append(off)
        off = off + jnp.sum(blk, axis=1, keepdims=True)
    return jnp.concatenate(out, axis=1), jnp.concatenate(starts, axis=1)


def _select_kernel(lt_ref, utri_ref, pos_o, aff_o, start_o):
    utri = utri_ref[...]
    affs = []
    for grp in range(2):
        x = lt_ref[grp * N_EXPERTS:(grp + 1) * N_EXPERTS, :]
        e = jnp.exp(x - jnp.max(x, axis=0, keepdims=True))
        affs.append(e / jnp.sum(e, axis=0, keepdims=True))
    aff = jnp.concatenate(affs, axis=0)
    aff_o[...] = aff
    n_rows = 2 * N_EXPERTS

    def body(_, lh):
        lo, hi = lh
        m2 = 0.5 * (lo + hi)
        m1 = 0.5 * (lo + m2)
        m3 = 0.5 * (m2 + hi)
        ge1, ge2, ge3 = (jnp.sum(jnp.where(aff > m, 1.0, 0.0), axis=1, keepdims=True) >= CAP for m in (m1, m2, m3))
        lo = jnp.where(ge3, m3, jnp.where(ge2, m2, jnp.where(ge1, m1, lo)))
        hi = jnp.where(ge3, hi, jnp.where(ge2, m3, jnp.where(ge1, m2, m1)))
        return lo, hi

    lo, hi = lax.fori_loop(0, BISECT_ITERS // 2, body,
                           (jnp.full((n_rows, 1), -1.0, F32), jnp.full((n_rows, 1), 1.0, F32)))
    gt = jnp.where(aff > hi, 1.0, 0.0)
    tie = jnp.where(aff > lo, 1.0, 0.0) - gt
    need = CAP - jnp.sum(gt, axis=1, keepdims=True)
    sel = gt + tie * jnp.where(_prefix_lanes(tie, utri)[0] < need, 1.0, 0.0)
    pos, starts = _prefix_lanes(sel, utri)
    pos_o[...] = jnp.where(sel > 0.5, pos, -1.0)
    start_o[...] = jnp.concatenate([starts, jnp.zeros((n_rows, 128 - N_TOK // 128), F32)], axis=1)


def _select(logits_t, utri):
    shp = jax.ShapeDtypeStruct((2 * N_EXPERTS, N_TOK), F32)
    full = lambda a: pl.BlockSpec(a.shape, lambda i: (0,) * a.ndim)
    return pl.pallas_call(
        _select_kernel,
        grid=(1,),
        in_specs=[full(logits_t), full(utri)],
        out_specs=[pl.BlockSpec(shp.shape, lambda i: (0, 0))] * 2 + [pl.BlockSpec((2 * N_EXPERTS, 128), lambda i: (0, 0))],
        out_shape=[shp, shp, jax.ShapeDtypeStruct((2 * N_EXPERTS, 128), F32)],
        compiler_params=_cparams(1),
        name="ec_select",
    )(logits_t, utri)


def _compact_kernel(start_ref, pos_ref, aff_ref, idx_o, gate_o, idx_acc, gate_acc):
    i = pl.program_id(0)
    n_blk = N_TOK // 128
    idx_acc[...] = jnp.zeros_like(idx_acc)
    gate_acc[...] = jnp.zeros_like(gate_acc)
    rel = lax.broadcasted_iota(jnp.int32, (COMPACT_WIN, 1), 0)
    for c in range(n_blk):
        base = pl.multiple_of(lax.shift_right_logical(start_ref[i * n_blk + c], 3) * 8, 8)
        slot = (rel + base).astype(F32)
        p = pos_ref[0, :, c * 128:(c + 1) * 128]
        a = aff_ref[0, :, c * 128:(c + 1) * 128]
        tok = (lax.broadcasted_iota(jnp.int32, (1, 128), 1) + c * 128).astype(F32)
        hit = p == slot
        idx_acc[pl.ds(base, COMPACT_WIN), :] = idx_acc[pl.ds(base, COMPACT_WIN), :] + jnp.where(hit, tok, 0.0)
        gate_acc[pl.ds(base, COMPACT_WIN), :] = gate_acc[pl.ds(base, COMPACT_WIN), :] + jnp.where(hit, a, 0.0)
    idx_o[0] = jnp.broadcast_to(jnp.sum(idx_acc[0:CAP, :], axis=1, keepdims=True), (CAP, 128))
    gate_o[0] = jnp.broadcast_to(jnp.sum(gate_acc[0:CAP, :], axis=1, keepdims=True), (CAP, 128))


def _compact(block_starts, pos3, aff3):
    rows = pos3.shape[0]
    shp = jax.ShapeDtypeStruct((rows, CAP, 128), F32)
    return pl.pallas_call(
        _compact_kernel,
        grid_spec=pltpu.PrefetchScalarGridSpec(
            num_scalar_prefetch=1,
            grid=(rows,),
            in_specs=[pl.BlockSpec((1, 1, N_TOK), lambda i, st: (i, 0, 0))] * 2,
            out_specs=[pl.BlockSpec((1, CAP, 128), lambda i, st: (i, 0, 0))] * 2,
            scratch_shapes=[pltpu.VMEM((CAP + COMPACT_WIN, 128), F32)] * 2),
        out_shape=[shp, shp],
        compiler_params=_cparams(1),
        name="ec_compact",
    )(block_starts, pos3, aff3)


def _gather_kernel(idx_ref, h_ref, o_ref, tile):
    grp = pl.program_id(0)
    eb = pl.program_id(1)
    for ei in range(ROUTE_EXPERTS):
        base = (grp * N_EXPERTS + eb * ROUTE_EXPERTS + ei) * CAP

        def tok(j8, c, base=base):
            for u in range(8):
                j = j8 * 8 + u
                row = pl.multiple_of(idx_ref[base + j], TOK_ROWS)
                tile[pl.ds(j, TOK_ROWS, stride=TOK_STRIDE), :] = h_ref[0, pl.ds(row, TOK_ROWS), :]
            return c

        lax.fori_loop(0, CAP // 8, tok, 0)
        for c in range(TOK_ROWS):
            o_ref[ei, :, c * 128:(c + 1) * 128] = tile[c * TOK_STRIDE:c * TOK_STRIDE + CAP, :].astype(BF16)


def _gather(idx_rows, h_rows):
    return pl.pallas_call(
        _gather_kernel,
        grid_spec=pltpu.PrefetchScalarGridSpec(
            num_scalar_prefetch=1,
            grid=(2, N_EXPERTS // ROUTE_EXPERTS),
            in_specs=[pl.BlockSpec((1, N_TOK * 8, 128), lambda g, e, idx: (g, 0, 0))],
            out_specs=pl.BlockSpec((ROUTE_EXPERTS, CAP, D_MODEL), lambda g, e, idx: (e, g, 0)),
            scratch_shapes=[pltpu.VMEM((8 * TOK_STRIDE, 128), F32)]),
        out_shape=jax.ShapeDtypeStruct((N_EXPERTS, 2 * CAP, D_MODEL), BF16),
        compiler_params=_cparams(2),
        name="ec_gather",
    )(idx_rows, h_rows)


def _expert_kernel(xe_ref, w1_ref, w3_ref, w2_ref, gate_p_ref, gate_s_ref, o_ref, hid_scr):
    s = pl.program_id(1)
    subs = [(lo, min(FF_SUB, FF_SLAB - lo)) for lo in range(0, FF_SLAB, FF_SUB)]

    @pl.when(s < FF_SPLIT)
    def _():
        x = xe_ref[0]
        for lo, width in subs:
            h1 = jnp.dot(x, w1_ref[0, :, lo:lo + width].astype(BF16), preferred_element_type=F32)
            h3 = jnp.dot(x, w3_ref[0, :, lo:lo + width].astype(BF16), preferred_element_type=F32)
            hid_scr[s, :, lo:lo + width] = (h1 * _sigmoid(h1) * h3).astype(BF16)

    @pl.when(s >= FF_SPLIT)
    def _():
        acc = None
        for lo, width in subs:
            part = jnp.dot(hid_scr[s - FF_SPLIT, :, lo:lo + width], w2_ref[0, lo:lo + width, :].astype(BF16),
                           preferred_element_type=F32)
            acc = part if acc is None else acc + part

        chunks = [(grp, c, slice(c * TOK_STRIDE, c * TOK_STRIDE + CAP)) for grp in range(2) for c in range(TOK_ROWS)]
        piece = lambda grp, c: acc[grp * CAP:(grp + 1) * CAP, c * 128:(c + 1) * 128]

        @pl.when(s == FF_SPLIT)
        def _():
            for grp, c, rows in chunks:
                o_ref[0, grp, rows, :] = piece(grp, c)
                o_ref[0, grp, c * TOK_STRIDE + CAP:(c + 1) * TOK_STRIDE, :] = jnp.zeros((TOK_STRIDE - CAP, 128), F32)

        @pl.when((s > FF_SPLIT) & (s < 2 * FF_SPLIT - 1))
        def _():
            for grp, c, rows in chunks:
                o_ref[0, grp, rows, :] = o_ref[0, grp, rows, :] + piece(grp, c)

        @pl.when(s == 2 * FF_SPLIT - 1)
        def _():
            gates = (gate_p_ref[0][:, 0:1], gate_s_ref[0][:, 0:1])
            for grp, c, rows in chunks:
                o_ref[0, grp, rows, :] = (o_ref[0, grp, rows, :] + piece(grp, c)) * gates[grp]


def _experts(xe, w1, w3, w2, gate_rows):
    m = xe.shape[1]
    up = lambda e, s: (e, 0, jnp.minimum(s, FF_SPLIT - 1))
    down = lambda e, s: (e, jnp.maximum(s - FF_SPLIT, 0), 0)
    return pl.pallas_call(
        _expert_kernel,
        grid=(N_EXPERTS, 2 * FF_SPLIT),
        in_specs=[pl.BlockSpec((1, m, D_MODEL), lambda e, s: (e, 0, 0)),
                  pl.BlockSpec((1, D_MODEL, FF_SLAB), up),
                  pl.BlockSpec((1, D_MODEL, FF_SLAB), up),
                  pl.BlockSpec((1, FF_SLAB, D_MODEL), down),
                  pl.BlockSpec((1, CAP, 128), lambda e, s: (e, 0, 0)),
                  pl.BlockSpec((1, CAP, 128), lambda e, s: (N_EXPERTS + e, 0, 0))],
        out_specs=pl.BlockSpec((1, 2, TOK_ROWS * TOK_STRIDE, 128), lambda e, s: (e, 0, 0, 0)),
        out_shape=jax.ShapeDtypeStruct((N_EXPERTS, 2, TOK_ROWS * TOK_STRIDE, 128), F32),
        scratch_shapes=[pltpu.VMEM((FF_SPLIT, m, FF_SLAB), BF16)],
        compiler_params=_cparams(2, vmem=EXPERT_VMEM_LIMIT),
        name="ec_experts",
    )(xe, w1, w3, w2, gate_rows, gate_rows)


def _combine_kernel(idx_ref, ye_ref, x1_ref, g2_ref, fg_ref, op_ref, os_ref, acc):
    grp = pl.program_id(0)
    s = pl.program_id(1)
    n_scatter = N_EXPERTS // ROUTE_EXPERTS

    @pl.when(s == 0)
    def _():
        acc[...] = jnp.zeros_like(acc)

    @pl.when(s < n_scatter)
    def _():
        for ei in range(ROUTE_EXPERTS):
            base = (grp * N_EXPERTS + s * ROUTE_EXPERTS + ei) * CAP
            def tok(jb, c, base=base):
                rows, vals = [], []
                for u in range(SCATTER_UNROLL):
                    j = jb * SCATTER_UNROLL + u
                    row = pl.multiple_of(idx_ref[base + j], TOK_ROWS)
                    rows.append(row)
                    vals.append(acc[pl.ds(row, TOK_ROWS), :]
                                + ye_ref[ei, 0, pl.ds(j, TOK_ROWS, stride=TOK_STRIDE), :])
                for row, val in zip(rows, vals):
                    acc[pl.ds(row, TOK_ROWS), :] = val
                return c

            lax.fori_loop(0, CAP // SCATTER_UNROLL, tok, 0)

    @pl.when(s >= n_scatter)
    def _():
        row0 = (s - n_scatter) * (ROW_TILE * TOK_ROWS)
        y = jnp.concatenate([acc[pl.ds(row0 + c, ROW_TILE, stride=TOK_ROWS), :] for c in range(TOK_ROWS)], axis=1)
        x = x1_ref[...] + g2_ref[0] * y
        ms = jnp.mean(x * x, axis=-1, keepdims=True)
        out = x * lax.rsqrt(ms + RMS_EPS) * fg_ref[...]

        @pl.when(grp == 0)
        def _():
            op_ref[...] = out

        @pl.when(grp == 1)
        def _():
            os_ref[...] = out


def _combine_final(idx_rows, ye, x1, mod3, final_g):
    n_scatter = N_EXPERTS // ROUTE_EXPERTS
    tiles = N_TOK // ROW_TILE
    tile_of = lambda g, s: g * tiles + jnp.maximum(s - n_scatter, 0)
    shp = jax.ShapeDtypeStruct((N_TOK, D_MODEL), F32)
    return pl.pallas_call(
        _combine_kernel,
        grid_spec=pltpu.PrefetchScalarGridSpec(
            num_scalar_prefetch=1,
            grid=(2, n_scatter + tiles),
            in_specs=[pl.BlockSpec((ROUTE_EXPERTS, 1, TOK_ROWS * TOK_STRIDE, 128),
                                   lambda g, s, idx: (jnp.minimum(s, n_scatter - 1), g, 0, 0)),
                      pl.BlockSpec((ROW_TILE, D_MODEL), lambda g, s, idx: (tile_of(g, s), 0)),
                      pl.BlockSpec((1, 1, D_MODEL), lambda g, s, idx: (_cond_index(tile_of(g, s)), 0, 5)),
                      pl.BlockSpec((1, D_MODEL), lambda g, s, idx: (0, 0))],
            out_specs=[pl.BlockSpec((ROW_TILE, D_MODEL),
                                    lambda g, s, idx: (jnp.where(g == 0, jnp.maximum(s - n_scatter, 0), tiles - 1), 0)),
                       pl.BlockSpec((ROW_TILE, D_MODEL),
                                    lambda g, s, idx: (jnp.where(g == 1, jnp.maximum(s - n_scatter, 0), 0), 0))],
            scratch_shapes=[pltpu.VMEM((N_TOK * TOK_ROWS, 128), F32)]),
        out_shape=[shp, shp],
        compiler_params=_cparams(2),
        name="ec_combine_final",
    )(idx_rows, ye, x1, mod3, final_g)


def kernel(x_prompt, x_sample, state_wkv, c, c_ctx, norm1_g, norm2_g, w_mod, b_mod, w_in, conv_dw, conv_b, conv_ln_g,
           conv_ln_b, rwkv_w0, rwkv_w_up, rwkv_a0, rwkv_a_up, rwkv_g_up, rwkv_k_k, rwkv_k_a, rwkv_r_k, rwkv_gn_g,
           rwkv_gn_b, w_out, router, exp_w1, exp_w3, exp_w2, final_norm_g):
    row2 = lambda a: a.reshape(1, -1)
    x = (x_prompt.reshape(N_TOK, D_MODEL), x_sample.reshape(N_TOK, D_MODEL))
    cond8 = jnp.concatenate([c_ctx[None, :], c, jnp.zeros((8 - 1 - DEC_BATCH, D_MODEL), F32)], axis=0)

    mod = _modulation(cond8, w_mod[0], b_mod)
    mod3 = mod.reshape(8, 1, 6 * D_MODEL)

    lane_head = jnp.arange(GROUP_W, dtype=jnp.int32) // HEAD
    ones_blk = (lane_head[:, None] == lane_head[None, :]).astype(BF16)

    def per_dir_rows(up):
        z = jnp.zeros_like(up[0])
        return jnp.stack([jnp.concatenate([up[0], z], axis=0), jnp.concatenate([z, up[1]], axis=0)]).astype(BF16)
    mixer_consts = (mod3, norm1_g, w_in[0].astype(BF16), (conv_dw[0], conv_b, conv_ln_g, conv_ln_b),
                    (ones_blk, rwkv_w0[0], per_dir_rows(rwkv_w_up[0]), rwkv_a0[0], per_dir_rows(rwkv_a_up[0]),
                     rwkv_g_up[0].astype(BF16), rwkv_k_k, rwkv_k_a, row2(rwkv_r_k[0])))
    groups = [_mixer_in(x[0], *mixer_consts, seg=SEQ, tile0=0),
              _mixer_in(x[1], *mixer_consts, seg=GRID_W, tile0=N_TOK // ROW_TILE)]
    conv, rkv, kap, w0, w1, b0, b1, kd0, kd1, g, bonus = zip(*groups)

    dir_ops = lambda i: ((w0[i], b0[i], kd0[i]), (w1[i], b1[i], kd1[i]))
    yp_f, yp_b, new_state = _chunk_scan(rkv[0], kap[0], dir_ops(0), None, n_batch=BATCH, t_len=SEQ, want_final=True)
    ys_f, ys_b = _chunk_scan(rkv[1], kap[1], dir_ops(1), state_wkv[:, 0], n_batch=DEC_BATCH, t_len=DEC_SEQ,
                             want_final=False)

    r_pad = jnp.pad(router[0], ((0, 0), (0, 128 - N_EXPERTS)))
    r_hi = r_pad.astype(BF16)
    r_lo = (r_pad - r_hi.astype(F32)).astype(BF16)
    x1, h2_rows, logits = _postmix((yp_f, ys_f), (yp_b, ys_b), x, conv, bonus, g, mod3, rwkv_gn_g, rwkv_gn_b, ones_blk,
                                   w_out[0].astype(BF16), norm2_g, r_hi, r_lo)

    logits_t = logits[:, :N_EXPERTS].reshape(2, N_TOK, N_EXPERTS).transpose(0, 2, 1).reshape(2 * N_EXPERTS, N_TOK)
    lane = jnp.arange(128, dtype=jnp.int32)
    utri = (lane[:, None] < lane[None, :]).astype(BF16)
    pos, aff, starts = _select(logits_t, utri)
    block_starts = starts[:, :N_TOK // 128].astype(jnp.int32).reshape(-1)
    idx_c, gate_c = _compact(block_starts, pos.reshape(2 * N_EXPERTS, 1, N_TOK), aff.reshape(2 * N_EXPERTS, 1, N_TOK))
    idx_rows = (idx_c[:, :, 0].astype(jnp.int32) * TOK_ROWS).reshape(-1)

    xe = _gather(idx_rows, h2_rows.reshape(2, N_TOK * TOK_ROWS, LANES))
    ye = _experts(xe, exp_w1[0], exp_w3[0], exp_w2[0], gate_c)
    out_p, out_s = _combine_final(idx_rows, ye, x1, mod3, row2(final_norm_g))
    return (out_p.reshape(BATCH, SEQ, D_MODEL), out_s.reshape(DEC_BATCH, DEC_SEQ, D_MODEL),
            new_state.reshape(BATCH, 1, 2, N_HEADS, HEAD, HEAD))
```

```python
import functools

import jax
import jax.numpy as jnp
from jax import lax
from jax.experimental import pallas as pl
from jax.experimental.pallas import tpu as pltpu

F32 = jnp.float32
BF16 = jnp.bfloat16
LANES = 128
SUBLANES = 8

D_MODEL = 1024
BATCH = 16
SEQ = 256
DEC_BATCH = 4
DEC_SEQ = 1024
GRID_W = 64
CONV_W = 512
CONV_K = 31
CONV_PAD = 15
CONV_GAP = 2 * SUBLANES
RWKV_W = 512
HEAD = 64
N_HEADS = 8
LORA = 64
GATE_LORA = 128
N_EXPERTS = 16
D_FF = 2816
D_IN = 2944
CAP = 512
N_TOK = 4096
RMS_EPS = 1e-6
LN_EPS = 1e-5
GN_EPS = 64e-5

ROW_TILE = 512
FF_SPLIT = 2
FF_SLAB = D_FF // FF_SPLIT
FF_SUB = 256
EXPERT_VMEM_LIMIT = 60 * 1024 * 1024
CHUNK = 64
HEADS_PER_GROUP = 4
GROUP_W = HEADS_PER_GROUP * HEAD
PASSES_INTRA = 1
PASSES_INV = 1
PASSES_STATE = 1
PASSES_UPDATE = 3
SCAN_BATCHES = 4
TOK_ROWS = D_MODEL // LANES
TOK_STRIDE = CAP + SUBLANES
ROUTE_EXPERTS = 4
GATHER_UNROLL = 32
SCATTER_UNROLL = 16
COMPACT_WIN = LANES + SUBLANES
BISECT_ITERS = 160
VMEM_LIMIT = 56 * 1024 * 1024


def _cparams(n_axes, vmem=None):
    return pltpu.CompilerParams(dimension_semantics=("arbitrary",) * n_axes,
                                vmem_limit_bytes=vmem or VMEM_LIMIT)


def _cond_index(i):
    n_prompt = (BATCH * SEQ) // ROW_TILE
    per_batch = DEC_SEQ // ROW_TILE
    return jnp.where(i < n_prompt, 0, 1 + (i - n_prompt) // per_batch)


def _split_dot(x, w_bf16):
    hi = x.astype(BF16)
    lo = (x - hi.astype(F32)).astype(BF16)
    return (jnp.dot(hi, w_bf16, preferred_element_type=F32)
            + jnp.dot(lo, w_bf16, preferred_element_type=F32))


def _head_sums(x, ones_bf16):
    return jnp.concatenate([_split_dot(x[:, c:c + GROUP_W], ones_bf16) for c in range(0, x.shape[1], GROUP_W)],
                           axis=1)


def _sigmoid(x):
    return 1.0 / (1.0 + jnp.exp(-x))


def _mod_kernel(c_ref, w_ref, b_ref, o_ref):
    c = c_ref[...]
    s = (c * _sigmoid(c)).astype(BF16)
    o_ref[...] = jnp.dot(s, w_ref[...].astype(BF16), preferred_element_type=F32) + b_ref[...]


def _modulation(cond8, w_mod, b_mod):
    n = 6 * D_MODEL
    tn = D_MODEL
    return pl.pallas_call(
        _mod_kernel,
        grid=(n // tn,),
        in_specs=[pl.BlockSpec((8, D_MODEL), lambda j: (0, 0)),
                  pl.BlockSpec((D_MODEL, tn), lambda j: (0, j)),
                  pl.BlockSpec((1, tn), lambda j: (0, j))],
        out_specs=pl.BlockSpec((8, tn), lambda j: (0, j)),
        out_shape=jax.ShapeDtypeStruct((8, n), F32),
        compiler_params=_cparams(1),
        name="modulation",
    )(cond8, w_mod, b_mod)


def _group_tile(n_prompt_tiles):
    return (lambda i: (jnp.minimum(i, n_prompt_tiles - 1), 0)), (lambda i: (jnp.maximum(i - n_prompt_tiles, 0), 0))


def _pick(prompt_ref, sample_ref, n_prompt_tiles):
    return jnp.where(pl.program_id(0) < n_prompt_tiles, prompt_ref[...], sample_ref[...])


def _mixer_in_kernel(x_ref, sh_ref, sc_ref, n1g_ref, w_ref, dw_ref, db_ref, lng_ref, lnb_ref, ones_ref,
                     w0_ref, wup_ref, a0_ref, aup_ref, gup_ref, kk_ref, ka_ref, rk_ref,
                     conv_o, rkv_o, kap_o, lw0_o, lw1_o, b0_o, b1_o, kd0_o, kd1_o, g_o, bonus_o, *, seg):
    x = x_ref[...]
    ms = jnp.mean(x * x, axis=-1, keepdims=True)
    h = x * lax.rsqrt(ms + RMS_EPS) * n1g_ref[...]
    h = (h * (1.0 + sc_ref[0]) + sh_ref[0]).astype(BF16)
    uv = jnp.dot(h, w_ref[:, 0:CONV_W], preferred_element_type=F32)
    ug = jnp.dot(h, w_ref[:, CONV_W:2 * CONV_W], preferred_element_type=F32)
    conv_o[...] = _conv_ln_silu(uv * _sigmoid(ug), seg, dw_ref, db_ref, lng_ref, lnb_ref)
    rkv = jnp.dot(h, w_ref[:, 2 * CONV_W:2 * CONV_W + 3 * RWKV_W], preferred_element_type=F32)
    rkv_o[...] = rkv
    lora = jnp.dot(h, w_ref[:, 2 * CONV_W + 3 * RWKV_W:D_IN], preferred_element_type=F32)
    _rwkv_operands(rkv, lora, ones_ref, w0_ref, wup_ref, a0_ref, aup_ref, gup_ref, kk_ref, ka_ref, rk_ref,
                   kap_o, (lw0_o, lw1_o), (b0_o, b1_o), (kd0_o, kd1_o), g_o, bonus_o)


def _mixer_in(x, mod3, norm1_g, w_in_bf16, conv_params, rwkv_params, *, seg, tile0):
    row = lambda c: pl.BlockSpec((ROW_TILE, c), lambda i: (i, 0))
    full = lambda a: pl.BlockSpec(a.shape, lambda i: (0,) * a.ndim)
    consts = (norm1_g, w_in_bf16) + tuple(conv_params) + tuple(rwkv_params)
    widths = [CONV_W, 3 * RWKV_W] + [RWKV_W] * 9
    return pl.pallas_call(
        functools.partial(_mixer_in_kernel, seg=seg),
        grid=(N_TOK // ROW_TILE,),
        in_specs=[row(D_MODEL),
                  pl.BlockSpec((1, 1, D_MODEL), lambda i: (_cond_index(tile0 + i), 0, 0)),
                  pl.BlockSpec((1, 1, D_MODEL), lambda i: (_cond_index(tile0 + i), 0, 1)),
                  ] + [full(a) for a in consts],
        out_specs=[row(c) for c in widths],
        out_shape=[jax.ShapeDtypeStruct((N_TOK, c), F32) for c in widths],
        compiler_params=_cparams(1),
        name=f"mixer_in_seg{seg}",
    )(x, mod3, mod3, *consts)


def _conv_ln_silu(u, seg, dw_ref, db_ref, g_ref, b_ref):
    n_seg = u.shape[0] // seg
    pitch = seg + CONV_GAP
    gap = jnp.zeros((CONV_GAP, u.shape[1]), F32)
    up = jnp.concatenate([piece for i in range(n_seg) for piece in (u[i * seg:(i + 1) * seg], gap)], axis=0)
    acc = up * dw_ref[CONV_PAD:CONV_PAD + 1, :]
    for k in range(CONV_K):
        s = k - CONV_PAD
        if s != 0:
            acc = acc + pltpu.roll(up, (-s) % up.shape[0], 0) * dw_ref[k:k + 1, :]
    out = jnp.concatenate([acc[i * pitch:i * pitch + seg] for i in range(n_seg)], axis=0) + db_ref[...]
    mu = jnp.mean(out, axis=-1, keepdims=True)
    d = out - mu
    var = jnp.mean(d * d, axis=-1, keepdims=True)
    out = d * lax.rsqrt(var + LN_EPS) * g_ref[...] + b_ref[...]
    return out * _sigmoid(out)


def _rwkv_operands(rkv, lora, ones_ref, w0_ref, wup_ref, a0_ref, aup_ref, gup_ref, kk_ref, ka_ref, rk_ref,
                   kap_o, lw_o, b_o, kd_o, g_o, bonus_o):
    r = rkv[:, 0:RWKV_W]
    k = rkv[:, RWKV_W:2 * RWKV_W]
    v = rkv[:, 2 * RWKV_W:3 * RWKV_W]
    xw = jnp.tanh(lora[:, 0:2 * LORA]).astype(BF16)
    xa = lora[:, 2 * LORA:4 * LORA].astype(BF16)
    xg = _sigmoid(lora[:, 4 * LORA:4 * LORA + GATE_LORA]).astype(BF16)
    ones = ones_ref[...]

    kk0 = k * kk_ref[...]
    ss = _head_sums(kk0 * kk0, ones)
    kap = kk0 * lax.rsqrt(jnp.maximum(ss, 1e-24))
    kap_o[...] = kap
    g_o[...] = jnp.dot(xg, gup_ref[...], preferred_element_type=F32)

    kd_sum = jnp.zeros_like(k)
    for d in range(2):
        wl = w0_ref[d:d + 1, :] + jnp.dot(xw, wup_ref[d], preferred_element_type=F32)
        lw_o[d][...] = -jnp.exp(-0.5) * _sigmoid(wl)
        a = _sigmoid(a0_ref[d:d + 1, :] + jnp.dot(xa, aup_ref[d], preferred_element_type=F32))
        b_o[d][...] = kap * a
        kd = k * (1.0 + (a - 1.0) * ka_ref[...])
        kd_o[d][...] = kd
        kd_sum = kd_sum + kd
    bonus_o[...] = _head_sums(r * kd_sum * rk_ref[...], ones) * v


def _pieces(x, passes):
    hi = x.astype(BF16)
    return (hi,) if passes == 1 else (hi, (x - hi.astype(F32)).astype(BF16))


_NN = ((1,), (0,))
_NT = ((1,), (1,))
_TN = ((0,), (0,))


def _mm(a, b, dims):
    dg = lambda p, q: lax.dot_general(p, q, (dims, ((), ())), preferred_element_type=F32)
    if len(a) == 1:
        out = dg(a[0], b[0])
    else:
        free = 1 - dims[0][0]
        both = dg(jnp.concatenate(a, axis=free), b[0])
        m = a[0].shape[free]
        out = both[:m] + both[m:]
    if len(b) > 1:
        out = out + dg(a[0], b[1])
    return out


def _block_diag(pieces, head_masks):
    return tuple(jnp.concatenate([p * m for m in head_masks], axis=0) for p in pieces)


def _chunk_masks(reverse):
    row = lax.broadcasted_iota(jnp.int32, (CHUNK, GROUP_W), 0)
    col = lax.broadcasted_iota(jnp.int32, (CHUNK, GROUP_W), 1) % CHUNK
    strict = (col > row) if reverse else (col < row)
    incl = (col >= row) if reverse else (col <= row)
    eye_side = jnp.where(col == row, 1.0, 0.0)
    tri = jnp.where(incl[:, :CHUNK], 1.0, 0.0).astype(BF16)
    return strict, incl, eye_side, tri, (0 if reverse else CHUNK - 1)


def _chunk_steps(chains):
    ln = CHUNK
    each = lambda f, *cols: [f(*args) for args in zip(*cols)]
    r, kap, v, lw, b, kd, n, masks = (list(col) for col in zip(*chains))
    strict = [m[0] for m in masks]
    incl = [m[1] for m in masks]
    lane_blk = lax.broadcasted_iota(jnp.int32, (ln, GROUP_W), 1) // HEAD
    head_masks = [jnp.where(lane_blk == h, 1.0, 0.0).astype(BF16) for h in range(HEADS_PER_GROUP)]
    bd = lambda x, passes: _block_diag(_pieces(x, passes), head_masks)
    r2 = lax.broadcasted_iota(jnp.int32, (GROUP_W, GROUP_W), 0) // HEAD
    c2 = lax.broadcasted_iota(jnp.int32, (GROUP_W, GROUP_W), 1) // HEAD
    same_head = r2 == c2

    def running_sum(lw_c, m):
        return _mm((m[3],), _pieces(lw_c, 3), _NN)

    cw = each(running_sum, lw, masks)
    tot = each(lambda cw_c, m: cw_c[m[4]:m[4] + 1, :], cw, masks)
    e_neg = each(lambda cw_c: jnp.exp(-cw_c), cw)
    kr = each(lambda kap_c, r_c, cw_c, lw_c: _pieces(
        jnp.concatenate([kap_c * jnp.exp(cw_c - lw_c), r_c * jnp.exp(cw_c)], axis=0), PASSES_INTRA),
        kap, r, cw, lw)
    ab = each(lambda kr_c, b_c, e: _mm(kr_c, bd(b_c * e, PASSES_INTRA), _NT), kr, b, e_neg)
    ak = each(lambda kr_c, kd_c, e: _mm(kr_c, bd(kd_c * e, PASSES_INTRA), _NT), kr, kd, e_neg)

    q = each(lambda ab_c, s: jnp.where(s, -ab_c[:ln], 0.0), ab, strict)
    x = each(lambda q_c, m: m[2] + q_c, q, masks)
    q = each(lambda q_c: _mm(_pieces(q_c, PASSES_INV), bd(q_c, PASSES_INV), _NN), q)
    n_rounds = CHUNK.bit_length() - 2
    for j in range(1, n_rounds + 1):
        last_round = j == n_rounds
        lhs = x if last_round else each(lambda q_c, x_c: jnp.concatenate([q_c, x_c], axis=0), q, x)
        prod = each(lambda l_c, q_c: _mm(_pieces(l_c, PASSES_INV), bd(q_c, PASSES_INV), _NN), lhs, q)
        if last_round:
            x = each(lambda x_c, p_c: x_c + p_c, x, prod)
        else:
            q = each(lambda p_c: p_c[:ln], prod)
            x = each(lambda x_c, p_c: x_c + p_c[ln:], x, prod)

    a3 = each(lambda ab_c, i: jnp.where(i, ab_c[ln:], 0.0), ab, incl)
    a24 = each(lambda ak_c, s, i: jnp.concatenate([jnp.where(s, ak_c[:ln], 0.0), jnp.where(i, ak_c[ln:], 0.0)], axis=0),
               ak, strict, incl)
    av = each(lambda a_c, v_c: _mm(_pieces(a_c, PASSES_INTRA), bd(v_c, PASSES_INTRA), _NN), a24, v)
    krn = each(lambda kr_c, n_c: _mm(kr_c[:PASSES_STATE], _pieces(n_c, PASSES_STATE), _NT), kr, n)
    u = each(lambda x_c, krn_c, av_c: -_mm(_pieces(x_c, PASSES_INV), bd(krn_c[:ln] + av_c[:ln], PASSES_INV), _NN),
             x, krn, av)
    y = each(lambda krn_c, a3_c, u_c, av_c: krn_c[ln:] + _mm(_pieces(a3_c, PASSES_STATE), bd(u_c, PASSES_STATE), _NN)
             + av_c[ln:], krn, a3, u, av)
    e_rest = each(lambda tot_c, cw_c: jnp.exp(tot_c - cw_c), tot, cw)
    upd = each(lambda u_c, v_c, b_c, kd_c, e: _mm(_pieces(jnp.concatenate([u_c, v_c], axis=0), PASSES_UPDATE),
                                                   _pieces(jnp.concatenate([b_c * e, kd_c * e], axis=0), PASSES_UPDATE),
                                                   _TN),
               u, v, b, kd, e_rest)
    n_new = each(lambda n_c, tot_c, upd_c: n_c * jnp.exp(tot_c) + jnp.where(same_head, upd_c, 0.0), n, tot, upd)
    return list(zip(y, n_new))


def _chunk_kernel(*refs, has_init, has_final):
    n_tok = 12 * SCAN_BATCHES
    tok_refs = [refs[k * 12:(k + 1) * 12] for k in range(SCAN_BATCHES)]
    refs = refs[n_tok:]
    s0 = refs[0] if has_init else None
    refs = refs[1:] if has_init else refs
    y_refs = refs[:2]
    sfin = refs[2] if has_final else None
    n_scr = refs[-1]
    c = pl.program_id(1)
    head_blocks = [(k, d, g, h) for k in range(SCAN_BATCHES) for d in range(2)
                   for g in range(N_HEADS // HEADS_PER_GROUP) for h in range(HEADS_PER_GROUP)]
    diag = lambda h: slice(h * HEAD, (h + 1) * HEAD)

    @pl.when(c == 0)
    def _():
        n_scr[...] = jnp.zeros(n_scr.shape, F32)
        if has_init:
            for k, d, g, h in head_blocks:
                n_scr[d, k, g, diag(h), diag(h)] = s0[k, d, g * HEADS_PER_GROUP + h]

    slots = [(k, d, g) for k in range(SCAN_BATCHES) for d in range(2) for g in range(N_HEADS // HEADS_PER_GROUP)]
    lanes = lambda g: slice(g * GROUP_W, (g + 1) * GROUP_W)
    masks = [_chunk_masks(reverse=False), _chunk_masks(reverse=True)]

    def operands(k, d, g):
        rf, kf, vf, rb, kb, vb, lwf, bf, kdf, lwb, bb, kdb = tok_refs[k]
        group = (rb, kb, vb, lwb, bb, kdb) if d else (rf, kf, vf, lwf, bf, kdf)
        return tuple(ref[:, lanes(g)] for ref in group)

    chains = [operands(k, d, g) + (n_scr[d, k, g], masks[d]) for k, d, g in slots]
    for (k, d, g), (y, n_new) in zip(slots, _chunk_steps(chains)):
        y_refs[d][k, :, lanes(g)] = y
        n_scr[d, k, g] = n_new

    if has_final:
        @pl.when(c == pl.num_programs(1) - 1)
        def _():
            for k, d, g, h in head_blocks:
                sfin[k, d, g * HEADS_PER_GROUP + h] = n_scr[d, k, g, diag(h), diag(h)]


def _chunk_scan(rkv, kap, dir_ops, s0, *, n_batch, t_len, want_final):
    n_chunks = t_len // CHUNK
    n_grp = N_HEADS // HEADS_PER_GROUP
    nb = SCAN_BATCHES

    def tok_spec(k, reverse, col=0):
        def index(bi, ci):
            return ((bi * nb + k) * n_chunks + (n_chunks - 1 - ci if reverse else ci), col)
        return pl.BlockSpec((CHUNK, RWKV_W), index)

    in_specs, args = [], []
    for k in range(nb):
        for rev in (False, True):
            in_specs += [tok_spec(k, rev, 0), tok_spec(k, rev), tok_spec(k, rev, 2)]
        in_specs += [tok_spec(k, False)] * 3 + [tok_spec(k, True)] * 3
        args += [rkv, kap, rkv, rkv, kap, rkv, *dir_ops[0], *dir_ops[1]]
    state = pl.BlockSpec((nb, 2, N_HEADS, HEAD, HEAD), lambda bi, ci: (bi, 0, 0, 0, 0))
    has_init = s0 is not None
    if has_init:
        in_specs.append(state)
        args.append(s0)
    y_shape = jax.ShapeDtypeStruct((n_batch, t_len, RWKV_W), F32)
    out_specs = [pl.BlockSpec((nb, CHUNK, RWKV_W), lambda bi, ci: (bi, ci, 0)),
                 pl.BlockSpec((nb, CHUNK, RWKV_W), lambda bi, ci: (bi, n_chunks - 1 - ci, 0))]
    out_shape = [y_shape, y_shape]
    if want_final:
        out_specs.append(state)
        out_shape.append(jax.ShapeDtypeStruct((n_batch, 2, N_HEADS, HEAD, HEAD), F32))
    outs = pl.pallas_call(
        functools.partial(_chunk_kernel, has_init=has_init, has_final=want_final),
        grid=(n_batch // nb, n_chunks),
        in_specs=in_specs,
        out_specs=out_specs,
        out_shape=out_shape,
        scratch_shapes=[pltpu.VMEM((2, nb, n_grp, GROUP_W, GROUP_W), F32)],
        compiler_params=_cparams(2),
        name=f"wkv_chunk_t{t_len}",
    )(*args)
    return [outs[0].reshape(n_batch * t_len, RWKV_W), outs[1].reshape(n_batch * t_len, RWKV_W)] + list(outs[2:])


def _postmix_kernel(yfp_ref, yfs_ref, ybp_ref, ybs_ref, xp_ref, xs_ref, convp_ref, convs_ref, bonusp_ref, bonuss_ref,
                    gp_ref, gs_ref, g1_ref, sh2_ref, sc2_ref, gng_ref, gnb_ref, ones_ref, wout_ref, n2g_ref, rhi_ref, rlo_ref,
                    x1_o, h2_o, logit_o):
    n_prompt = N_TOK // ROW_TILE
    ones = ones_ref[...]
    y = _pick(yfp_ref, yfs_ref, n_prompt) + _pick(ybp_ref, ybs_ref, n_prompt)
    mu = _head_sums(y, ones) * (1.0 / HEAD)
    d = y - mu
    var = _head_sums(d * d, ones) * (1.0 / HEAD)
    yn = d * lax.rsqrt(var + GN_EPS) * gng_ref[...] + gnb_ref[...]
    rw = ((yn + _pick(bonusp_ref, bonuss_ref, n_prompt)) * _pick(gp_ref, gs_ref, n_prompt)).astype(BF16)
    conv = _pick(convp_ref, convs_ref, n_prompt).astype(BF16)
    mix = (jnp.dot(conv, wout_ref[0:CONV_W, :], preferred_element_type=F32)
           + jnp.dot(rw, wout_ref[CONV_W:CONV_W + RWKV_W, :], preferred_element_type=F32))
    x1 = _pick(xp_ref, xs_ref, n_prompt) + g1_ref[0] * mix
    x1_o[...] = x1
    ms = jnp.mean(x1 * x1, axis=-1, keepdims=True)
    h2 = x1 * lax.rsqrt(ms + RMS_EPS) * n2g_ref[...]
    h2 = h2 * (1.0 + sc2_ref[0]) + sh2_ref[0]
    for c in range(TOK_ROWS):
        h2_o[pl.ds(c, ROW_TILE, stride=TOK_ROWS), :] = h2[:, c * 128:(c + 1) * 128]
    hi = h2.astype(BF16)
    lo = (h2 - hi.astype(F32)).astype(BF16)
    logit_o[...] = (jnp.dot(hi, rhi_ref[...], preferred_element_type=F32)
                    + jnp.dot(lo, rhi_ref[...], preferred_element_type=F32)
                    + jnp.dot(hi, rlo_ref[...], preferred_element_type=F32))


def _postmix(yf, yb, x, conv, bonus, g, mod3, gn_g, gn_b, ones_blk, w_out_bf16, norm2_g, r_hi, r_lo):
    n = 2 * N_TOK
    row = lambda c: pl.BlockSpec((ROW_TILE, c), lambda i: (i, 0))
    pair = lambda c: [pl.BlockSpec((ROW_TILE, c), m) for m in _group_tile(N_TOK // ROW_TILE)]
    full = lambda a: pl.BlockSpec(a.shape, lambda i: (0,) * a.ndim)
    modspec = lambda j: pl.BlockSpec((1, 1, D_MODEL), lambda i: (_cond_index(i), 0, j))
    return pl.pallas_call(
        _postmix_kernel,
        grid=(n // ROW_TILE,),
        in_specs=pair(RWKV_W) + pair(RWKV_W) + pair(D_MODEL) + pair(CONV_W) + pair(RWKV_W) + pair(RWKV_W) + [
            modspec(2), modspec(3), modspec(4), full(gn_g), full(gn_b), full(ones_blk),
            full(w_out_bf16), full(norm2_g), full(r_hi), full(r_lo)],
        out_specs=[row(D_MODEL), pl.BlockSpec((ROW_TILE * TOK_ROWS, 128), lambda i: (i, 0)), row(128)],
        out_shape=[jax.ShapeDtypeStruct((n, D_MODEL), F32), jax.ShapeDtypeStruct((n * TOK_ROWS, 128), F32),
                   jax.ShapeDtypeStruct((n, 128), F32)],
        compiler_params=_cparams(1),
        name="postmix",
    )(*yf, *yb, *x, *conv, *bonus, *g, mod3, mod3, mod3, gn_g, gn_b, ones_blk, w_out_bf16, norm2_g, r_hi, r_lo)


def _prefix_lanes(m, utri):
    rows = m.shape[0]
    off = jnp.zeros((rows, 1), F32)
    out, starts = [], []
    for c in range(N_TOK // 128):
        blk = m[:, c * 128:(c + 1) * 128]
        out.append(jnp.dot(blk.astype(BF16), utri, preferred_element_type=F32) + off)
        starts.append(off)
        off = off + jnp.sum(blk, axis=1, keepdims=True)
    return jnp.concatenate(out, axis=1), jnp.concatenate(starts, axis=1)


def _select_kernel(lt_ref, utri_ref, pos_o, aff_o, start_o):
    utri = utri_ref[...]
    affs = []
    for grp in range(2):
        x = lt_ref[grp * N_EXPERTS:(grp + 1) * N_EXPERTS, :]
        e = jnp.exp(x - jnp.max(x, axis=0, keepdims=True))
        affs.append(e / jnp.sum(e, axis=0, keepdims=True))
    aff = jnp.concatenate(affs, axis=0)
    aff_o[...] = aff
    n_rows = 2 * N_EXPERTS

    def body(_, lh):
        lo, hi = lh
        m2 = 0.5 * (lo + hi)
        m1 = 0.5 * (lo + m2)
        m3 = 0.5 * (m2 + hi)
        ge1, ge2, ge3 = (jnp.sum(jnp.where(aff > m, 1.0, 0.0), axis=1, keepdims=True) >= CAP for m in (m1, m2, m3))
        lo = jnp.where(ge3, m3, jnp.where(ge2, m2, jnp.where(ge1, m1, lo)))
        hi = jnp.where(ge3, hi, jnp.where(ge2, m3, jnp.where(ge1, m2, m1)))
        return lo, hi

    lo, hi = lax.fori_loop(0, BISECT_ITERS // 2, body,
                           (jnp.full((n_rows, 1), -1.0, F32), jnp.full((n_rows, 1), 1.0, F32)))
    gt = jnp.where(aff > hi, 1.0, 0.0)
    tie = jnp.where(aff > lo, 1.0, 0.0) - gt
    need = CAP - jnp.sum(gt, axis=1, keepdims=True)
    sel = gt + tie * jnp.where(_prefix_lanes(tie, utri)[0] < need, 1.0, 0.0)
    pos, starts = _prefix_lanes(sel, utri)
    pos_o[...] = jnp.where(sel > 0.5, pos, -1.0)
    start_o[...] = jnp.concatenate([starts, jnp.zeros((n_rows, 128 - N_TOK // 128), F32)], axis=1)


def _select(logits_t, utri):
    shp = jax.ShapeDtypeStruct((2 * N_EXPERTS, N_TOK), F32)
    full = lambda a: pl.BlockSpec(a.shape, lambda i: (0,) * a.ndim)
    return pl.pallas_call(
        _select_kernel,
        grid=(1,),
        in_specs=[full(logits_t), full(utri)],
        out_specs=[pl.BlockSpec(shp.shape, lambda i: (0, 0))] * 2 + [pl.BlockSpec((2 * N_EXPERTS, 128), lambda i: (0, 0))],
        out_shape=[shp, shp, jax.ShapeDtypeStruct((2 * N_EXPERTS, 128), F32)],
        compiler_params=_cparams(1),
        name="ec_select",
    )(logits_t, utri)


def _compact_kernel(start_ref, pos_ref, aff_ref, idx_o, gate_o, idx_acc, gate_acc):
    i = pl.program_id(0)
    n_blk = N_TOK // 128
    idx_acc[...] = jnp.zeros_like(idx_acc)
    gate_acc[...] = jnp.zeros_like(gate_acc)
    rel = lax.broadcasted_iota(jnp.int32, (COMPACT_WIN, 1), 0)
    for c in range(n_blk):
        base = pl.multiple_of(lax.shift_right_logical(start_ref[i * n_blk + c], 3) * 8, 8)
        slot = (rel + base).astype(F32)
        p = pos_ref[0, :, c * 128:(c + 1) * 128]
        a = aff_ref[0, :, c * 128:(c + 1) * 128]
        tok = (lax.broadcasted_iota(jnp.int32, (1, 128), 1) + c * 128).astype(F32)
        hit = p == slot
        idx_acc[pl.ds(base, COMPACT_WIN), :] = idx_acc[pl.ds(base, COMPACT_WIN), :] + jnp.where(hit, tok, 0.0)
        gate_acc[pl.ds(base, COMPACT_WIN), :] = gate_acc[pl.ds(base, COMPACT_WIN), :] + jnp.where(hit, a, 0.0)
    idx_o[0] = jnp.broadcast_to(jnp.sum(idx_acc[0:CAP, :], axis=1, keepdims=True), (CAP, 128))
    gate_o[0] = jnp.broadcast_to(jnp.sum(gate_acc[0:CAP, :], axis=1, keepdims=True), (CAP, 128))


def _compact(block_starts, pos3, aff3):
    rows = pos3.shape[0]
    shp = jax.ShapeDtypeStruct((rows, CAP, 128), F32)
    return pl.pallas_call(
        _compact_kernel,
        grid_spec=pltpu.PrefetchScalarGridSpec(
            num_scalar_prefetch=1,
            grid=(rows,),
            in_specs=[pl.BlockSpec((1, 1, N_TOK), lambda i, st: (i, 0, 0))] * 2,
            out_specs=[pl.BlockSpec((1, CAP, 128), lambda i, st: (i, 0, 0))] * 2,
            scratch_shapes=[pltpu.VMEM((CAP + COMPACT_WIN, 128), F32)] * 2),
        out_shape=[shp, shp],
        compiler_params=_cparams(1),
        name="ec_compact",
    )(block_starts, pos3, aff3)


def _gather_kernel(idx_ref, h_ref, o_ref, tile):
    grp = pl.program_id(0)
    eb = pl.program_id(1)
    for ei in range(ROUTE_EXPERTS):
        base = (grp * N_EXPERTS + eb * ROUTE_EXPERTS + ei) * CAP

        def tok(jb, c, base=base):
            for u in range(GATHER_UNROLL):
                j = jb * GATHER_UNROLL + u
                row = pl.multiple_of(idx_ref[base + j], TOK_ROWS)
                tile[pl.ds(j, TOK_ROWS, stride=TOK_STRIDE), :] = h_ref[0, pl.ds(row, TOK_ROWS), :]
            return c

        lax.fori_loop(0, CAP // GATHER_UNROLL, tok, 0)
        for c in range(TOK_ROWS):
            o_ref[ei, :, c * 128:(c + 1) * 128] = tile[c * TOK_STRIDE:c * TOK_STRIDE + CAP, :].astype(BF16)


def _gather(idx_rows, h_rows):
    return pl.pallas_call(
        _gather_kernel,
        grid_spec=pltpu.PrefetchScalarGridSpec(
            num_scalar_prefetch=1,
            grid=(2, N_EXPERTS // ROUTE_EXPERTS),
            in_specs=[pl.BlockSpec((1, N_TOK * 8, 128), lambda g, e, idx: (g, 0, 0))],
            out_specs=pl.BlockSpec((ROUTE_EXPERTS, CAP, D_MODEL), lambda g, e, idx: (e, g, 0)),
            scratch_shapes=[pltpu.VMEM((8 * TOK_STRIDE, 128), F32)]),
        out_shape=jax.ShapeDtypeStruct((N_EXPERTS, 2 * CAP, D_MODEL), BF16),
        compiler_params=_cparams(2),
        name="ec_gather",
    )(idx_rows, h_rows)


def _expert_kernel(xe_ref, w1_ref, w3_ref, w2_ref, gate_p_ref, gate_s_ref, o_ref, hid_scr):
    s = pl.program_id(1)
    subs = [(lo, min(FF_SUB, FF_SLAB - lo)) for lo in range(0, FF_SLAB, FF_SUB)]

    @pl.when(s < FF_SPLIT)
    def _():
        x = xe_ref[0]
        for lo, width in subs:
            h1 = jnp.dot(x, w1_ref[0, :, lo:lo + width].astype(BF16), preferred_element_type=F32)
            h3 = jnp.dot(x, w3_ref[0, :, lo:lo + width].astype(BF16), preferred_element_type=F32)
            hid_scr[s, :, lo:lo + width] = (h1 * _sigmoid(h1) * h3).astype(BF16)

    @pl.when(s >= FF_SPLIT)
    def _():
        acc = None
        for lo, width in subs:
            part = jnp.dot(hid_scr[s - FF_SPLIT, :, lo:lo + width], w2_ref[0, lo:lo + width, :].astype(BF16),
                           preferred_element_type=F32)
            acc = part if acc is None else acc + part

        chunks = [(grp, c, slice(c * TOK_STRIDE, c * TOK_STRIDE + CAP)) for grp in range(2) for c in range(TOK_ROWS)]
        piece = lambda grp, c: acc[grp * CAP:(grp + 1) * CAP, c * 128:(c + 1) * 128]

        @pl.when(s == FF_SPLIT)
        def _():
            for grp, c, rows in chunks:
                o_ref[0, grp, rows, :] = piece(grp, c)
                o_ref[0, grp, c * TOK_STRIDE + CAP:(c + 1) * TOK_STRIDE, :] = jnp.zeros((TOK_STRIDE - CAP, 128), F32)

        @pl.when((s > FF_SPLIT) & (s < 2 * FF_SPLIT - 1))
        def _():
            for grp, c, rows in chunks:
                o_ref[0, grp, rows, :] = o_ref[0, grp, rows, :] + piece(grp, c)

        @pl.when(s == 2 * FF_SPLIT - 1)
        def _():
            gates = (gate_p_ref[0][:, 0:1], gate_s_ref[0][:, 0:1])
            for grp, c, rows in chunks:
                o_ref[0, grp, rows, :] = (o_ref[0, grp, rows, :] + piece(grp, c)) * gates[grp]


def _experts(xe, w1, w3, w2, gate_rows):
    m = xe.shape[1]
    up = lambda e, s: (e, 0, jnp.minimum(s, FF_SPLIT - 1))
    down = lambda e, s: (e, jnp.maximum(s - FF_SPLIT, 0), 0)
    return pl.pallas_call(
        _expert_kernel,
        grid=(N_EXPERTS, 2 * FF_SPLIT),
        in_specs=[pl.BlockSpec((1, m, D_MODEL), lambda e, s: (e, 0, 0)),
                  pl.BlockSpec((1, D_MODEL, FF_SLAB), up),
                  pl.BlockSpec((1, D_MODEL, FF_SLAB), up),
                  pl.BlockSpec((1, FF_SLAB, D_MODEL), down),
                  pl.BlockSpec((1, CAP, 128), lambda e, s: (e, 0, 0)),
                  pl.BlockSpec((1, CAP, 128), lambda e, s: (N_EXPERTS + e, 0, 0))],
        out_specs=pl.BlockSpec((1, 2, TOK_ROWS * TOK_STRIDE, 128), lambda e, s: (e, 0, 0, 0)),
        out_shape=jax.ShapeDtypeStruct((N_EXPERTS, 2, TOK_ROWS * TOK_STRIDE, 128), F32),
        scratch_shapes=[pltpu.VMEM((FF_SPLIT, m, FF_SLAB), BF16)],
        compiler_params=_cparams(2, vmem=EXPERT_VMEM_LIMIT),
        name="ec_experts",
    )(xe, w1, w3, w2, gate_rows, gate_rows)


def _combine_kernel(idx_ref, ye_ref, x1_ref, g2_ref, fg_ref, op_ref, os_ref, acc):
    grp = pl.program_id(0)
    s = pl.program_id(1)
    n_scatter = N_EXPERTS // ROUTE_EXPERTS

    @pl.when(s == 0)
    def _():
        acc[...] = jnp.zeros_like(acc)

    @pl.when(s < n_scatter)
    def _():
        for ei in range(ROUTE_EXPERTS):
            base = (grp * N_EXPERTS + s * ROUTE_EXPERTS + ei) * CAP
            def tok(jb, c, base=base):
                rows, vals = [], []
                for u in range(SCATTER_UNROLL):
                    j = jb * SCATTER_UNROLL + u
                    row = pl.multiple_of(idx_ref[base + j], TOK_ROWS)
                    rows.append(row)
                    vals.append(acc[pl.ds(row, TOK_ROWS), :]
                                + ye_ref[ei, 0, pl.ds(j, TOK_ROWS, stride=TOK_STRIDE), :])
                for row, val in zip(rows, vals):
                    acc[pl.ds(row, TOK_ROWS), :] = val
                return c

            lax.fori_loop(0, CAP // SCATTER_UNROLL, tok, 0)

    @pl.when(s >= n_scatter)
    def _():
        row0 = (s - n_scatter) * (ROW_TILE * TOK_ROWS)
        y = jnp.concatenate([acc[pl.ds(row0 + c, ROW_TILE, stride=TOK_ROWS), :] for c in range(TOK_ROWS)], axis=1)
        x = x1_ref[...] + g2_ref[0] * y
        ms = jnp.mean(x * x, axis=-1, keepdims=True)
        out = x * lax.rsqrt(ms + RMS_EPS) * fg_ref[...]

        @pl.when(grp == 0)
        def _():
            op_ref[...] = out

        @pl.when(grp == 1)
        def _():
            os_ref[...] = out


def _combine_final(idx_rows, ye, x1, mod3, final_g):
    n_scatter = N_EXPERTS // ROUTE_EXPERTS
    tiles = N_TOK // ROW_TILE
    tile_of = lambda g, s: g * tiles + jnp.maximum(s - n_scatter, 0)
    shp = jax.ShapeDtypeStruct((N_TOK, D_MODEL), F32)
    return pl.pallas_call(
        _combine_kernel,
        grid_spec=pltpu.PrefetchScalarGridSpec(
            num_scalar_prefetch=1,
            grid=(2, n_scatter + tiles),
            in_specs=[pl.BlockSpec((ROUTE_EXPERTS, 1, TOK_ROWS * TOK_STRIDE, 128),
                                   lambda g, s, idx: (jnp.minimum(s, n_scatter - 1), g, 0, 0)),
                      pl.BlockSpec((ROW_TILE, D_MODEL), lambda g, s, idx: (tile_of(g, s), 0)),
                      pl.BlockSpec((1, 1, D_MODEL), lambda g, s, idx: (_cond_index(tile_of(g, s)), 0, 5)),
                      pl.BlockSpec((1, D_MODEL), lambda g, s, idx: (0, 0))],
            out_specs=[pl.BlockSpec((ROW_TILE, D_MODEL),
                                    lambda g, s, idx: (jnp.where(g == 0, jnp.maximum(s - n_scatter, 0), tiles - 1), 0)),
                       pl.BlockSpec((ROW_TILE, D_MODEL),
                                    lambda g, s, idx: (jnp.where(g == 1, jnp.maximum(s - n_scatter, 0), 0), 0))],
            scratch_shapes=[pltpu.VMEM((N_TOK * TOK_ROWS, 128), F32)]),
        out_shape=[shp, shp],
        compiler_params=_cparams(2),
        name="ec_combine_final",
    )(idx_rows, ye, x1, mod3, final_g)


def kernel(x_prompt, x_sample, state_wkv, c, c_ctx, norm1_g, norm2_g, w_mod, b_mod, w_in, conv_dw, conv_b, conv_ln_g,
           conv_ln_b, rwkv_w0, rwkv_w_up, rwkv_a0, rwkv_a_up, rwkv_g_up, rwkv_k_k, rwkv_k_a, rwkv_r_k, rwkv_gn_g,
           rwkv_gn_b, w_out, router, exp_w1, exp_w3, exp_w2, final_norm_g):
    row2 = lambda a: a.reshape(1, -1)
    x = (x_prompt.reshape(N_TOK, D_MODEL), x_sample.reshape(N_TOK, D_MODEL))
    cond8 = jnp.concatenate([c_ctx[None, :], c, jnp.zeros((8 - 1 - DEC_BATCH, D_MODEL), F32)], axis=0)

    mod = _modulation(cond8, w_mod[0], b_mod)
    mod3 = mod.reshape(8, 1, 6 * D_MODEL)

    lane_head = jnp.arange(GROUP_W, dtype=jnp.int32) // HEAD
    ones_blk = (lane_head[:, None] == lane_head[None, :]).astype(BF16)

    def per_dir_rows(up):
        z = jnp.zeros_like(up[0])
        return jnp.stack([jnp.concatenate([up[0], z], axis=0), jnp.concatenate([z, up[1]], axis=0)]).astype(BF16)
    mixer_consts = (mod3, norm1_g, w_in[0].astype(BF16), (conv_dw[0], conv_b, conv_ln_g, conv_ln_b),
                    (ones_blk, rwkv_w0[0], per_dir_rows(rwkv_w_up[0]), rwkv_a0[0], per_dir_rows(rwkv_a_up[0]),
                     rwkv_g_up[0].astype(BF16), rwkv_k_k, rwkv_k_a, row2(rwkv_r_k[0])))
    groups = [_mixer_in(x[0], *mixer_consts, seg=SEQ, tile0=0),
              _mixer_in(x[1], *mixer_consts, seg=GRID_W, tile0=N_TOK // ROW_TILE)]
    conv, rkv, kap, w0, w1, b0, b1, kd0, kd1, g, bonus = zip(*groups)

    dir_ops = lambda i: ((w0[i], b0[i], kd0[i]), (w1[i], b1[i], kd1[i]))
    yp_f, yp_b, new_state = _chunk_scan(rkv[0], kap[0], dir_ops(0), None, n_batch=BATCH, t_len=SEQ, want_final=True)
    ys_f, ys_b = _chunk_scan(rkv[1], kap[1], dir_ops(1), state_wkv[:, 0], n_batch=DEC_BATCH, t_len=DEC_SEQ,
                             want_final=False)

    r_pad = jnp.pad(router[0], ((0, 0), (0, 128 - N_EXPERTS)))
    r_hi = r_pad.astype(BF16)
    r_lo = (r_pad - r_hi.astype(F32)).astype(BF16)
    x1, h2_rows, logits = _postmix((yp_f, ys_f), (yp_b, ys_b), x, conv, bonus, g, mod3, rwkv_gn_g, rwkv_gn_b, ones_blk,
                                   w_out[0].astype(BF16), norm2_g, r_hi, r_lo)

    logits_t = logits[:, :N_EXPERTS].reshape(2, N_TOK, N_EXPERTS).transpose(0, 2, 1).reshape(2 * N_EXPERTS, N_TOK)
    lane = jnp.arange(128, dtype=jnp.int32)
    utri = (lane[:, None] < lane[None, :]).astype(BF16)
    pos, aff, starts = _select(logits_t, utri)
    block_starts = starts[:, :N_TOK // 128].astype(jnp.int32).reshape(-1)
    idx_c, gate_c = _compact(block_starts, pos.reshape(2 * N_EXPERTS, 1, N_TOK), aff.reshape(2 * N_EXPERTS, 1, N_TOK))
    idx_rows = (idx_c[:, :, 0].astype(jnp.int32) * TOK_ROWS).reshape(-1)

    xe = _gather(idx_rows, h2_rows.reshape(2, N_TOK * TOK_ROWS, LANES))
    ye = _experts(xe, exp_w1[0], exp_w3[0], exp_w2[0], gate_c)
    out_p, out_s = _combine_final(idx_rows, ye, x1, mod3, row2(final_norm_g))
    return (out_p.reshape(BATCH, SEQ, D_MODEL), out_s.reshape(DEC_BATCH, DEC_SEQ, D_MODEL),
            new_state.reshape(BATCH, 1, 2, N_HEADS, HEAD, HEAD))
```

```python
import functools

import jax
import jax.numpy as jnp
from jax import lax
from jax.experimental import pallas as pl
from jax.experimental.pallas import tpu as pltpu

F32 = jnp.float32
BF16 = jnp.bfloat16
LANES = 128
SUBLANES = 8

D_MODEL = 1024
BATCH = 16
SEQ = 256
DEC_BATCH = 4
DEC_SEQ = 1024
GRID_W = 64
CONV_W = 512
CONV_K = 31
CONV_PAD = 15
CONV_GAP = 2 * SUBLANES
RWKV_W = 512
HEAD = 64
N_HEADS = 8
LORA = 64
GATE_LORA = 128
N_EXPERTS = 16
D_FF = 2816
D_IN = 2944
CAP = 512
N_TOK = 4096
RMS_EPS = 1e-6
LN_EPS = 1e-5
GN_EPS = 64e-5

ROW_TILE = 512
FF_SPLIT = 2
FF_SLAB = D_FF // FF_SPLIT
FF_SUB = 256
EXPERT_VMEM_LIMIT = 60 * 1024 * 1024
CHUNK = 64
HEADS_PER_GROUP = 4
GROUP_W = HEADS_PER_GROUP * HEAD
PASSES_INTRA = 1
PASSES_INV = 1
PASSES_STATE = 1
PASSES_UPDATE = 3
SCAN_BATCHES = 4
TOK_ROWS = D_MODEL // LANES
TOK_STRIDE = CAP + SUBLANES
ROUTE_EXPERTS = 4
GATHER_UNROLL = 32
SCATTER_UNROLL = 16
COMPACT_WIN = LANES + SUBLANES
BISECT_ITERS = 160
VMEM_LIMIT = 56 * 1024 * 1024


def _cparams(n_axes, vmem=None):
    return pltpu.CompilerParams(dimension_semantics=("arbitrary",) * n_axes,
                                vmem_limit_bytes=vmem or VMEM_LIMIT)


def _cond_index(i):
    n_prompt = (BATCH * SEQ) // ROW_TILE
    per_batch = DEC_SEQ // ROW_TILE
    return jnp.where(i < n_prompt, 0, 1 + (i - n_prompt) // per_batch)


def _split_dot(x, w_bf16):
    hi = x.astype(BF16)
    lo = (x - hi.astype(F32)).astype(BF16)
    return (jnp.dot(hi, w_bf16, preferred_element_type=F32)
            + jnp.dot(lo, w_bf16, preferred_element_type=F32))


def _head_sums(x, ones_bf16):
    return jnp.concatenate([_split_dot(x[:, c:c + GROUP_W], ones_bf16) for c in range(0, x.shape[1], GROUP_W)],
                           axis=1)


def _sigmoid(x):
    return 1.0 / (1.0 + jnp.exp(-x))


def _mod_kernel(c_ref, w_ref, b_ref, o_ref):
    c = c_ref[...]
    s = (c * _sigmoid(c)).astype(BF16)
    o_ref[...] = jnp.dot(s, w_ref[...].astype(BF16), preferred_element_type=F32) + b_ref[...]


def _modulation(cond8, w_mod, b_mod):
    n = 6 * D_MODEL
    tn = D_MODEL
    return pl.pallas_call(
        _mod_kernel,
        grid=(n // tn,),
        in_specs=[pl.BlockSpec((8, D_MODEL), lambda j: (0, 0)),
                  pl.BlockSpec((D_MODEL, tn), lambda j: (0, j)),
                  pl.BlockSpec((1, tn), lambda j: (0, j))],
        out_specs=pl.BlockSpec((8, tn), lambda j: (0, j)),
        out_shape=jax.ShapeDtypeStruct((8, n), F32),
        compiler_params=_cparams(1),
        name="modulation",
    )(cond8, w_mod, b_mod)


def _group_tile(n_prompt_tiles):
    return (lambda i: (jnp.minimum(i, n_prompt_tiles - 1), 0)), (lambda i: (jnp.maximum(i - n_prompt_tiles, 0), 0))


def _pick(prompt_ref, sample_ref, n_prompt_tiles):
    return jnp.where(pl.program_id(0) < n_prompt_tiles, prompt_ref[...], sample_ref[...])


def _mixer_in_kernel(x_ref, sh_ref, sc_ref, n1g_ref, w_ref, dw_ref, db_ref, lng_ref, lnb_ref, ones_ref,
                     w0_ref, wup_ref, a0_ref, aup_ref, gup_ref, kk_ref, ka_ref, rk_ref,
                     conv_o, rkv_o, kap_o, lw0_o, lw1_o, b0_o, b1_o, kd0_o, kd1_o, g_o, bonus_o, *, seg):
    x = x_ref[...]
    ms = jnp.mean(x * x, axis=-1, keepdims=True)
    h = x * lax.rsqrt(ms + RMS_EPS) * n1g_ref[...]
    h = (h * (1.0 + sc_ref[0]) + sh_ref[0]).astype(BF16)
    uv = jnp.dot(h, w_ref[:, 0:CONV_W], preferred_element_type=F32)
    ug = jnp.dot(h, w_ref[:, CONV_W:2 * CONV_W], preferred_element_type=F32)
    conv_o[...] = _conv_ln_silu(uv * _sigmoid(ug), seg, dw_ref, db_ref, lng_ref, lnb_ref)
    rkv = jnp.dot(h, w_ref[:, 2 * CONV_W:2 * CONV_W + 3 * RWKV_W], preferred_element_type=F32)
    rkv_o[...] = rkv
    lora = jnp.dot(h, w_ref[:, 2 * CONV_W + 3 * RWKV_W:D_IN], preferred_element_type=F32)
    _rwkv_operands(rkv, lora, ones_ref, w0_ref, wup_ref, a0_ref, aup_ref, gup_ref, kk_ref, ka_ref, rk_ref,
                   kap_o, (lw0_o, lw1_o), (b0_o, b1_o), (kd0_o, kd1_o), g_o, bonus_o)


def _mixer_in(x, mod3, norm1_g, w_in_bf16, conv_params, rwkv_params, *, seg, tile0):
    row = lambda c: pl.BlockSpec((ROW_TILE, c), lambda i: (i, 0))
    full = lambda a: pl.BlockSpec(a.shape, lambda i: (0,) * a.ndim)
    consts = (norm1_g, w_in_bf16) + tuple(conv_params) + tuple(rwkv_params)
    widths = [CONV_W, 3 * RWKV_W] + [RWKV_W] * 9
    return pl.pallas_call(
        functools.partial(_mixer_in_kernel, seg=seg),
        grid=(N_TOK // ROW_TILE,),
        in_specs=[row(D_MODEL),
                  pl.BlockSpec((1, 1, D_MODEL), lambda i: (_cond_index(tile0 + i), 0, 0)),
                  pl.BlockSpec((1, 1, D_MODEL), lambda i: (_cond_index(tile0 + i), 0, 1)),
                  ] + [full(a) for a in consts],
        out_specs=[row(c) for c in widths],
        out_shape=[jax.ShapeDtypeStruct((N_TOK, c), F32) for c in widths],
        compiler_params=_cparams(1),
        name=f"mixer_in_seg{seg}",
    )(x, mod3, mod3, *consts)


def _conv_ln_silu(u, seg, dw_ref, db_ref, g_ref, b_ref):
    n_seg = u.shape[0] // seg
    pitch = seg + CONV_GAP
    gap = jnp.zeros((CONV_GAP, u.shape[1]), F32)
    up = jnp.concatenate([piece for i in range(n_seg) for piece in (u[i * seg:(i + 1) * seg], gap)], axis=0)
    acc = up * dw_ref[CONV_PAD:CONV_PAD + 1, :]
    for k in range(CONV_K):
        s = k - CONV_PAD
        if s != 0:
            acc = acc + pltpu.roll(up, (-s) % up.shape[0], 0) * dw_ref[k:k + 1, :]
    out = jnp.concatenate([acc[i * pitch:i * pitch + seg] for i in range(n_seg)], axis=0) + db_ref[...]
    mu = jnp.mean(out, axis=-1, keepdims=True)
    d = out - mu
    var = jnp.mean(d * d, axis=-1, keepdims=True)
    out = d * lax.rsqrt(var + LN_EPS) * g_ref[...] + b_ref[...]
    return out * _sigmoid(out)


def _rwkv_operands(rkv, lora, ones_ref, w0_ref, wup_ref, a0_ref, aup_ref, gup_ref, kk_ref, ka_ref, rk_ref,
                   kap_o, lw_o, b_o, kd_o, g_o, bonus_o):
    r = rkv[:, 0:RWKV_W]
    k = rkv[:, RWKV_W:2 * RWKV_W]
    v = rkv[:, 2 * RWKV_W:3 * RWKV_W]
    xw = jnp.tanh(lora[:, 0:2 * LORA]).astype(BF16)
    xa = lora[:, 2 * LORA:4 * LORA].astype(BF16)
    xg = _sigmoid(lora[:, 4 * LORA:4 * LORA + GATE_LORA]).astype(BF16)
    ones = ones_ref[...]

    kk0 = k * kk_ref[...]
    ss = _head_sums(kk0 * kk0, ones)
    kap = kk0 * lax.rsqrt(jnp.maximum(ss, 1e-24))
    kap_o[...] = kap
    g_o[...] = jnp.dot(xg, gup_ref[...], preferred_element_type=F32)

    kd_sum = jnp.zeros_like(k)
    for d in range(2):
        wl = w0_ref[d:d + 1, :] + jnp.dot(xw, wup_ref[d], preferred_element_type=F32)
        lw_o[d][...] = -jnp.exp(-0.5) * _sigmoid(wl)
        a = _sigmoid(a0_ref[d:d + 1, :] + jnp.dot(xa, aup_ref[d], preferred_element_type=F32))
        b_o[d][...] = kap * a
        kd = k * (1.0 + (a - 1.0) * ka_ref[...])
        kd_o[d][...] = kd
        kd_sum = kd_sum + kd
    bonus_o[...] = _head_sums(r * kd_sum * rk_ref[...], ones) * v


def _pieces(x, passes):
    hi = x.astype(BF16)
    return (hi,) if passes == 1 else (hi, (x - hi.astype(F32)).astype(BF16))


_NN = ((1,), (0,))
_NT = ((1,), (1,))
_TN = ((0,), (0,))


def _mm(a, b, dims):
    dg = lambda p, q: lax.dot_general(p, q, (dims, ((), ())), preferred_element_type=F32)
    if len(a) == 1:
        out = dg(a[0], b[0])
    else:
        free = 1 - dims[0][0]
        both = dg(jnp.concatenate(a, axis=free), b[0])
        m = a[0].shape[free]
        out = both[:m] + both[m:]
    if len(b) > 1:
        out = out + dg(a[0], b[1])
    return out


def _block_diag(pieces, head_masks):
    return tuple(jnp.concatenate([p * m for m in head_masks], axis=0) for p in pieces)


def _chunk_masks(reverse):
    row = lax.broadcasted_iota(jnp.int32, (CHUNK, GROUP_W), 0)
    col = lax.broadcasted_iota(jnp.int32, (CHUNK, GROUP_W), 1) % CHUNK
    strict = (col > row) if reverse else (col < row)
    incl = (col >= row) if reverse else (col <= row)
    eye_side = jnp.where(col == row, 1.0, 0.0)
    tri = jnp.where(incl[:, :CHUNK], 1.0, 0.0).astype(BF16)
    return strict, incl, eye_side, tri, (0 if reverse else CHUNK - 1)


def _chunk_steps(chains):
    ln = CHUNK
    each = lambda f, *cols: [f(*args) for args in zip(*cols)]
    r, kap, v, lw, b, kd, n, masks = (list(col) for col in zip(*chains))
    strict = [m[0] for m in masks]
    incl = [m[1] for m in masks]
    lane_blk = lax.broadcasted_iota(jnp.int32, (ln, GROUP_W), 1) // HEAD
    head_masks = [jnp.where(lane_blk == h, 1.0, 0.0).astype(BF16) for h in range(HEADS_PER_GROUP)]
    bd = lambda x, passes: _block_diag(_pieces(x, passes), head_masks)
    r2 = lax.broadcasted_iota(jnp.int32, (GROUP_W, GROUP_W), 0) // HEAD
    c2 = lax.broadcasted_iota(jnp.int32, (GROUP_W, GROUP_W), 1) // HEAD
    same_head = r2 == c2

    def running_sum(lw_c, m):
        return _mm((m[3],), _pieces(lw_c, 3), _NN)

    cw = each(running_sum, lw, masks)
    tot = each(lambda cw_c, m: cw_c[m[4]:m[4] + 1, :], cw, masks)
    e_neg = each(lambda cw_c: jnp.exp(-cw_c), cw)
    kr = each(lambda kap_c, r_c, cw_c, lw_c: _pieces(
        jnp.concatenate([kap_c * jnp.exp(cw_c - lw_c), r_c * jnp.exp(cw_c)], axis=0), PASSES_INTRA),
        kap, r, cw, lw)
    ab = each(lambda kr_c, b_c, e: _mm(kr_c, bd(b_c * e, PASSES_INTRA), _NT), kr, b, e_neg)
    ak = each(lambda kr_c, kd_c, e: _mm(kr_c, bd(kd_c * e, PASSES_INTRA), _NT), kr, kd, e_neg)

    q = each(lambda ab_c, s: jnp.where(s, -ab_c[:ln], 0.0), ab, strict)
    x = each(lambda q_c, m: m[2] + q_c, q, masks)
    q = each(lambda q_c: _mm(_pieces(q_c, PASSES_INV), bd(q_c, PASSES_INV), _NN), q)
    n_rounds = CHUNK.bit_length() - 2
    for j in range(1, n_rounds + 1):
        last_round = j == n_rounds
        lhs = x if last_round else each(lambda q_c, x_c: jnp.concatenate([q_c, x_c], axis=0), q, x)
        prod = each(lambda l_c, q_c: _mm(_pieces(l_c, PASSES_INV), bd(q_c, PASSES_INV), _NN), lhs, q)
        if last_round:
            x = each(lambda x_c, p_c: x_c + p_c, x, prod)
        else:
            q = each(lambda p_c: p_c[:ln], prod)
            x = each(lambda x_c, p_c: x_c + p_c[ln:], x, prod)

    a3 = each(lambda ab_c, i: jnp.where(i, ab_c[ln:], 0.0), ab, incl)
    a24 = each(lambda ak_c, s, i: jnp.concatenate([jnp.where(s, ak_c[:ln], 0.0), jnp.where(i, ak_c[ln:], 0.0)], axis=0),
               ak, strict, incl)
    av = each(lambda a_c, v_c: _mm(_pieces(a_c, PASSES_INTRA), bd(v_c, PASSES_INTRA), _NN), a24, v)
    krn = each(lambda kr_c, n_c: _mm(kr_c[:PASSES_STATE], _pieces(n_c, PASSES_STATE), _NT), kr, n)
    u = each(lambda x_c, krn_c, av_c: -_mm(_pieces(x_c, PASSES_INV), bd(krn_c[:ln] + av_c[:ln], PASSES_INV), _NN),
             x, krn, av)
    y = each(lambda krn_c, a3_c, u_c, av_c: krn_c[ln:] + _mm(_pieces(a3_c, PASSES_STATE), bd(u_c, PASSES_STATE), _NN)
             + av_c[ln:], krn, a3, u, av)
    e_rest = each(lambda tot_c, cw_c: jnp.exp(tot_c - cw_c), tot, cw)
    upd = each(lambda u_c, v_c, b_c, kd_c, e: _mm(_pieces(jnp.concatenate([u_c, v_c], axis=0), PASSES_UPDATE),
                                                   _pieces(jnp.concatenate([b_c * e, kd_c * e], axis=0), PASSES_UPDATE),
                                                   _TN),
               u, v, b, kd, e_rest)
    n_new = each(lambda n_c, tot_c, upd_c: n_c * jnp.exp(tot_c) + jnp.where(same_head, upd_c, 0.0), n, tot, upd)
    return list(zip(y, n_new))


def _chunk_kernel(*refs, has_init, has_final):
    n_tok = 12 * SCAN_BATCHES
    tok_refs = [refs[k * 12:(k + 1) * 12] for k in range(SCAN_BATCHES)]
    refs = refs[n_tok:]
    s0 = refs[0] if has_init else None
    refs = refs[1:] if has_init else refs
    y_refs = refs[:2]
    sfin = refs[2] if has_final else None
    n_scr = refs[-1]
    c = pl.program_id(1)
    head_blocks = [(k, d, g, h) for k in range(SCAN_BATCHES) for d in range(2)
                   for g in range(N_HEADS // HEADS_PER_GROUP) for h in range(HEADS_PER_GROUP)]
    diag = lambda h: slice(h * HEAD, (h + 1) * HEAD)

    @pl.when(c == 0)
    def _():
        n_scr[...] = jnp.zeros(n_scr.shape, F32)
        if has_init:
            for k, d, g, h in head_blocks:
                n_scr[d, k, g, diag(h), diag(h)] = s0[k, d, g * HEADS_PER_GROUP + h]

    slots = [(k, d, g) for k in range(SCAN_BATCHES) for d in range(2) for g in range(N_HEADS // HEADS_PER_GROUP)]
    lanes = lambda g: slice(g * GROUP_W, (g + 1) * GROUP_W)
    masks = [_chunk_masks(reverse=False), _chunk_masks(reverse=True)]

    def operands(k, d, g):
        rf, kf, vf, rb, kb, vb, lwf, bf, kdf, lwb, bb, kdb = tok_refs[k]
        group = (rb, kb, vb, lwb, bb, kdb) if d else (rf, kf, vf, lwf, bf, kdf)
        return tuple(ref[:, lanes(g)] for ref in group)

    chains = [operands(k, d, g) + (n_scr[d, k, g], masks[d]) for k, d, g in slots]
    for (k, d, g), (y, n_new) in zip(slots, _chunk_steps(chains)):
        y_refs[d][k, :, lanes(g)] = y
        n_scr[d, k, g] = n_new

    if has_final:
        @pl.when(c == pl.num_programs(1) - 1)
        def _():
            for k, d, g, h in head_blocks:
                sfin[k, d, g * HEADS_PER_GROUP + h] = n_scr[d, k, g, diag(h), diag(h)]


def _chunk_scan(rkv, kap, dir_ops, s0, *, n_batch, t_len, want_final):
    n_chunks = t_len // CHUNK
    n_grp = N_HEADS // HEADS_PER_GROUP
    nb = SCAN_BATCHES

    def tok_spec(k, reverse, col=0):
        def index(bi, ci):
            return ((bi * nb + k) * n_chunks + (n_chunks - 1 - ci if reverse else ci), col)
        return pl.BlockSpec((CHUNK, RWKV_W), index)

    in_specs, args = [], []
    for k in range(nb):
        for rev in (False, True):
            in_specs += [tok_spec(k, rev, 0), tok_spec(k, rev), tok_spec(k, rev, 2)]
        in_specs += [tok_spec(k, False)] * 3 + [tok_spec(k, True)] * 3
        args += [rkv, kap, rkv, rkv, kap, rkv, *dir_ops[0], *dir_ops[1]]
    state = pl.BlockSpec((nb, 2, N_HEADS, HEAD, HEAD), lambda bi, ci: (bi, 0, 0, 0, 0))
    has_init = s0 is not None
    if has_init:
        in_specs.append(state)
        args.append(s0)
    y_shape = jax.ShapeDtypeStruct((n_batch, t_len, RWKV_W), F32)
    out_specs = [pl.BlockSpec((nb, CHUNK, RWKV_W), lambda bi, ci: (bi, ci, 0)),
                 pl.BlockSpec((nb, CHUNK, RWKV_W), lambda bi, ci: (bi, n_chunks - 1 - ci, 0))]
    out_shape = [y_shape, y_shape]
    if want_final:
        out_specs.append(state)
        out_shape.append(jax.ShapeDtypeStruct((n_batch, 2, N_HEADS, HEAD, HEAD), F32))
    outs = pl.pallas_call(
        functools.partial(_chunk_kernel, has_init=has_init, has_final=want_final),
        grid=(n_batch // nb, n_chunks),
        in_specs=in_specs,
        out_specs=out_specs,
        out_shape=out_shape,
        scratch_shapes=[pltpu.VMEM((2, nb, n_grp, GROUP_W, GROUP_W), F32)],
        compiler_params=_cparams(2),
        name=f"wkv_chunk_t{t_len}",
    )(*args)
    return [outs[0].reshape(n_batch * t_len, RWKV_W), outs[1].reshape(n_batch * t_len, RWKV_W)] + list(outs[2:])


def _postmix_kernel(yfp_ref, yfs_ref, ybp_ref, ybs_ref, xp_ref, xs_ref, convp_ref, convs_ref, bonusp_ref, bonuss_ref,
                    gp_ref, gs_ref, g1_ref, sh2_ref, sc2_ref, gng_ref, gnb_ref, ones_ref, wout_ref, n2g_ref, rhi_ref, rlo_ref,
                    x1_o, h2_o, logit_o):
    n_prompt = N_TOK // ROW_TILE
    ones = ones_ref[...]
    y = _pick(yfp_ref, yfs_ref, n_prompt) + _pick(ybp_ref, ybs_ref, n_prompt)
    mu = _head_sums(y, ones) * (1.0 / HEAD)
    d = y - mu
    var = _head_sums(d * d, ones) * (1.0 / HEAD)
    yn = d * lax.rsqrt(var + GN_EPS) * gng_ref[...] + gnb_ref[...]
    rw = ((yn + _pick(bonusp_ref, bonuss_ref, n_prompt)) * _pick(gp_ref, gs_ref, n_prompt)).astype(BF16)
    conv = _pick(convp_ref, convs_ref, n_prompt).astype(BF16)
    mix = (jnp.dot(conv, wout_ref[0:CONV_W, :], preferred_element_type=F32)
           + jnp.dot(rw, wout_ref[CONV_W:CONV_W + RWKV_W, :], preferred_element_type=F32))
    x1 = _pick(xp_ref, xs_ref, n_prompt) + g1_ref[0] * mix
    x1_o[...] = x1
    ms = jnp.mean(x1 * x1, axis=-1, keepdims=True)
    h2 = x1 * lax.rsqrt(ms + RMS_EPS) * n2g_ref[...]
    h2 = h2 * (1.0 + sc2_ref[0]) + sh2_ref[0]
    for c in range(TOK_ROWS):
        h2_o[pl.ds(c, ROW_TILE, stride=TOK_ROWS), :] = h2[:, c * LANES:(c + 1) * LANES]
    hi = h2.astype(BF16)
    lo = (h2 - hi.astype(F32)).astype(BF16)
    logit_o[...] = (jnp.dot(hi, rhi_ref[...], preferred_element_type=F32)
                    + jnp.dot(lo, rhi_ref[...], preferred_element_type=F32)
                    + jnp.dot(hi, rlo_ref[...], preferred_element_type=F32))


def _postmix(yf, yb, x, conv, bonus, g, mod3, gn_g, gn_b, ones_blk, w_out_bf16, norm2_g, r_hi, r_lo):
    n = 2 * N_TOK
    row = lambda c: pl.BlockSpec((ROW_TILE, c), lambda i: (i, 0))
    pair = lambda c: [pl.BlockSpec((ROW_TILE, c), m) for m in _group_tile(N_TOK // ROW_TILE)]
    full = lambda a: pl.BlockSpec(a.shape, lambda i: (0,) * a.ndim)
    modspec = lambda j: pl.BlockSpec((1, 1, D_MODEL), lambda i: (_cond_index(i), 0, j))
    return pl.pallas_call(
        _postmix_kernel,
        grid=(n // ROW_TILE,),
        in_specs=pair(RWKV_W) + pair(RWKV_W) + pair(D_MODEL) + pair(CONV_W) + pair(RWKV_W) + pair(RWKV_W) + [
            modspec(2), modspec(3), modspec(4), full(gn_g), full(gn_b), full(ones_blk),
            full(w_out_bf16), full(norm2_g), full(r_hi), full(r_lo)],
        out_specs=[row(D_MODEL), pl.BlockSpec((ROW_TILE * TOK_ROWS, LANES), lambda i: (i, 0)), row(LANES)],
        out_shape=[jax.ShapeDtypeStruct((n, D_MODEL), F32), jax.ShapeDtypeStruct((n * TOK_ROWS, LANES), F32),
                   jax.ShapeDtypeStruct((n, LANES), F32)],
        compiler_params=_cparams(1),
        name="postmix",
    )(*yf, *yb, *x, *conv, *bonus, *g, mod3, mod3, mod3, gn_g, gn_b, ones_blk, w_out_bf16, norm2_g, r_hi, r_lo)


def _prefix_lanes(m, utri):
    rows = m.shape[0]
    off = jnp.zeros((rows, 1), F32)
    out, starts = [], []
    for c in range(N_TOK // LANES):
        blk = m[:, c * LANES:(c + 1) * LANES]
        out.append(jnp.dot(blk.astype(BF16), utri, preferred_element_type=F32) + off)
        starts.append(off)
        off = off + jnp.sum(blk, axis=1, keepdims=True)
    return jnp.concatenate(out, axis=1), jnp.concatenate(starts, axis=1)


def _select_kernel(lt_ref, utri_ref, pos_o, aff_o, start_o):
    utri = utri_ref[...]
    affs = []
    for grp in range(2):
        x = lt_ref[grp * N_EXPERTS:(grp + 1) * N_EXPERTS, :]
        e = jnp.exp(x - jnp.max(x, axis=0, keepdims=True))
        affs.append(e / jnp.sum(e, axis=0, keepdims=True))
    aff = jnp.concatenate(affs, axis=0)
    aff_o[...] = aff
    n_rows = 2 * N_EXPERTS

    def body(_, lh):
        lo, hi = lh
        m2 = 0.5 * (lo + hi)
        m1 = 0.5 * (lo + m2)
        m3 = 0.5 * (m2 + hi)
        ge1, ge2, ge3 = (jnp.sum(jnp.where(aff > m, 1.0, 0.0), axis=1, keepdims=True) >= CAP for m in (m1, m2, m3))
        lo = jnp.where(ge3, m3, jnp.where(ge2, m2, jnp.where(ge1, m1, lo)))
        hi = jnp.where(ge3, hi, jnp.where(ge2, m3, jnp.where(ge1, m2, m1)))
        return lo, hi

    lo, hi = lax.fori_loop(0, BISECT_ITERS // 2, body,
                           (jnp.full((n_rows, 1), -1.0, F32), jnp.full((n_rows, 1), 1.0, F32)))
    gt = jnp.where(aff > hi, 1.0, 0.0)
    tie = jnp.where(aff > lo, 1.0, 0.0) - gt
    need = CAP - jnp.sum(gt, axis=1, keepdims=True)
    sel = gt + tie * jnp.where(_prefix_lanes(tie, utri)[0] < need, 1.0, 0.0)
    pos, starts = _prefix_lanes(sel, utri)
    pos_o[...] = jnp.where(sel > 0.5, pos, -1.0)
    start_o[...] = jnp.concatenate([starts, jnp.zeros((n_rows, LANES - N_TOK // LANES), F32)], axis=1)


def _select(logits_t, utri):
    shp = jax.ShapeDtypeStruct((2 * N_EXPERTS, N_TOK), F32)
    full = lambda a: pl.BlockSpec(a.shape, lambda i: (0,) * a.ndim)
    return pl.pallas_call(
        _select_kernel,
        grid=(1,),
        in_specs=[full(logits_t), full(utri)],
        out_specs=[pl.BlockSpec(shp.shape, lambda i: (0, 0))] * 2 + [pl.BlockSpec((2 * N_EXPERTS, LANES), lambda i: (0, 0))],
        out_shape=[shp, shp, jax.ShapeDtypeStruct((2 * N_EXPERTS, LANES), F32)],
        compiler_params=_cparams(1),
        name="ec_select",
    )(logits_t, utri)


def _compact_kernel(start_ref, pos_ref, aff_ref, idx_o, gate_o, idx_acc, gate_acc):
    i = pl.program_id(0)
    n_blk = N_TOK // LANES
    idx_acc[...] = jnp.zeros_like(idx_acc)
    gate_acc[...] = jnp.zeros_like(gate_acc)
    rel = lax.broadcasted_iota(jnp.int32, (COMPACT_WIN, 1), 0)
    for c in range(n_blk):
        base = pl.multiple_of(lax.shift_right_logical(start_ref[i * n_blk + c], 3) * 8, 8)
        slot = (rel + base).astype(F32)
        p = pos_ref[0, :, c * LANES:(c + 1) * LANES]
        a = aff_ref[0, :, c * LANES:(c + 1) * LANES]
        tok = (lax.broadcasted_iota(jnp.int32, (1, LANES), 1) + c * LANES).astype(F32)
        hit = p == slot
        idx_acc[pl.ds(base, COMPACT_WIN), :] = idx_acc[pl.ds(base, COMPACT_WIN), :] + jnp.where(hit, tok, 0.0)
        gate_acc[pl.ds(base, COMPACT_WIN), :] = gate_acc[pl.ds(base, COMPACT_WIN), :] + jnp.where(hit, a, 0.0)
    idx_o[0] = jnp.broadcast_to(jnp.sum(idx_acc[0:CAP, :], axis=1, keepdims=True), (CAP, LANES))
    gate_o[0] = jnp.broadcast_to(jnp.sum(gate_acc[0:CAP, :], axis=1, keepdims=True), (CAP, LANES))


def _compact(block_starts, pos3, aff3):
    rows = pos3.shape[0]
    shp = jax.ShapeDtypeStruct((rows, CAP, LANES), F32)
    return pl.pallas_call(
        _compact_kernel,
        grid_spec=pltpu.PrefetchScalarGridSpec(
            num_scalar_prefetch=1,
            grid=(rows,),
            in_specs=[pl.BlockSpec((1, 1, N_TOK), lambda i, st: (i, 0, 0))] * 2,
            out_specs=[pl.BlockSpec((1, CAP, LANES), lambda i, st: (i, 0, 0))] * 2,
            scratch_shapes=[pltpu.VMEM((CAP + COMPACT_WIN, LANES), F32)] * 2),
        out_shape=[shp, shp],
        compiler_params=_cparams(1),
        name="ec_compact",
    )(block_starts, pos3, aff3)


def _gather_kernel(idx_ref, h_ref, o_ref, tile):
    grp = pl.program_id(0)
    eb = pl.program_id(1)
    for ei in range(ROUTE_EXPERTS):
        base = (grp * N_EXPERTS + eb * ROUTE_EXPERTS + ei) * CAP

        def tok(jb, c, base=base):
            for u in range(GATHER_UNROLL):
                j = jb * GATHER_UNROLL + u
                row = pl.multiple_of(idx_ref[base + j], TOK_ROWS)
                tile[pl.ds(j, TOK_ROWS, stride=TOK_STRIDE), :] = h_ref[0, pl.ds(row, TOK_ROWS), :]
            return c

        lax.fori_loop(0, CAP // GATHER_UNROLL, tok, 0)
        for c in range(TOK_ROWS):
            o_ref[ei, :, c * LANES:(c + 1) * LANES] = tile[c * TOK_STRIDE:c * TOK_STRIDE + CAP, :].astype(BF16)


def _gather(idx_rows, h_rows):
    return pl.pallas_call(
        _gather_kernel,
        grid_spec=pltpu.PrefetchScalarGridSpec(
            num_scalar_prefetch=1,
            grid=(2, N_EXPERTS // ROUTE_EXPERTS),
            in_specs=[pl.BlockSpec((1, N_TOK * TOK_ROWS, LANES), lambda g, e, idx: (g, 0, 0))],
            out_specs=pl.BlockSpec((ROUTE_EXPERTS, CAP, D_MODEL), lambda g, e, idx: (e, g, 0)),
            scratch_shapes=[pltpu.VMEM((TOK_ROWS * TOK_STRIDE, LANES), F32)]),
        out_shape=jax.ShapeDtypeStruct((N_EXPERTS, 2 * CAP, D_MODEL), BF16),
        compiler_params=_cparams(2),
        name="ec_gather",
    )(idx_rows, h_rows)


def _expert_kernel(xe_ref, w1_ref, w3_ref, w2_ref, gate_p_ref, gate_s_ref, o_ref, hid_scr):
    s = pl.program_id(1)
    subs = [(lo, min(FF_SUB, FF_SLAB - lo)) for lo in range(0, FF_SLAB, FF_SUB)]

    @pl.when(s < FF_SPLIT)
    def _():
        x = xe_ref[0]
        for lo, width in subs:
            h1 = jnp.dot(x, w1_ref[0, :, lo:lo + width].astype(BF16), preferred_element_type=F32)
            h3 = jnp.dot(x, w3_ref[0, :, lo:lo + width].astype(BF16), preferred_element_type=F32)
            hid_scr[s, :, lo:lo + width] = (h1 * _sigmoid(h1) * h3).astype(BF16)

    @pl.when(s >= FF_SPLIT)
    def _():
        acc = None
        for lo, width in subs:
            part = jnp.dot(hid_scr[s - FF_SPLIT, :, lo:lo + width], w2_ref[0, lo:lo + width, :].astype(BF16),
                           preferred_element_type=F32)
            acc = part if acc is None else acc + part

        chunks = [(grp, c, slice(c * TOK_STRIDE, c * TOK_STRIDE + CAP)) for grp in range(2) for c in range(TOK_ROWS)]
        piece = lambda grp, c: acc[grp * CAP:(grp + 1) * CAP, c * LANES:(c + 1) * LANES]

        @pl.when(s == FF_SPLIT)
        def _():
            for grp, c, rows in chunks:
                o_ref[0, grp, rows, :] = piece(grp, c)
                o_ref[0, grp, c * TOK_STRIDE + CAP:(c + 1) * TOK_STRIDE, :] = jnp.zeros((TOK_STRIDE - CAP, LANES), F32)

        @pl.when((s > FF_SPLIT) & (s < 2 * FF_SPLIT - 1))
        def _():
            for grp, c, rows in chunks:
                o_ref[0, grp, rows, :] = o_ref[0, grp, rows, :] + piece(grp, c)

        @pl.when(s == 2 * FF_SPLIT - 1)
        def _():
            gates = (gate_p_ref[0][:, 0:1], gate_s_ref[0][:, 0:1])
            for grp, c, rows in chunks:
                o_ref[0, grp, rows, :] = (o_ref[0, grp, rows, :] + piece(grp, c)) * gates[grp]


def _experts(xe, w1, w3, w2, gate_rows):
    m = xe.shape[1]
    up = lambda e, s: (e, 0, jnp.minimum(s, FF_SPLIT - 1))
    down = lambda e, s: (e, jnp.maximum(s - FF_SPLIT, 0), 0)
    return pl.pallas_call(
        _expert_kernel,
        grid=(N_EXPERTS, 2 * FF_SPLIT),
        in_specs=[pl.BlockSpec((1, m, D_MODEL), lambda e, s: (e, 0, 0)),
                  pl.BlockSpec((1, D_MODEL, FF_SLAB), up),
                  pl.BlockSpec((1, D_MODEL, FF_SLAB), up),
                  pl.BlockSpec((1, FF_SLAB, D_MODEL), down),
                  pl.BlockSpec((1, CAP, LANES), lambda e, s: (e, 0, 0)),
                  pl.BlockSpec((1, CAP, LANES), lambda e, s: (N_EXPERTS + e, 0, 0))],
        out_specs=pl.BlockSpec((1, 2, TOK_ROWS * TOK_STRIDE, LANES), lambda e, s: (e, 0, 0, 0)),
        out_shape=jax.ShapeDtypeStruct((N_EXPERTS, 2, TOK_ROWS * TOK_STRIDE, LANES), F32),
        scratch_shapes=[pltpu.VMEM((FF_SPLIT, m, FF_SLAB), BF16)],
        compiler_params=_cparams(2, vmem=EXPERT_VMEM_LIMIT),
        name="ec_experts",
    )(xe, w1, w3, w2, gate_rows, gate_rows)


def _combine_kernel(idx_ref, ye_ref, x1_ref, g2_ref, fg_ref, op_ref, os_ref, acc):
    grp = pl.program_id(0)
    s = pl.program_id(1)
    n_scatter = N_EXPERTS // ROUTE_EXPERTS

    @pl.when(s == 0)
    def _():
        acc[...] = jnp.zeros_like(acc)

    @pl.when(s < n_scatter)
    def _():
        for ei in range(ROUTE_EXPERTS):
            base = (grp * N_EXPERTS + s * ROUTE_EXPERTS + ei) * CAP
            def tok(jb, c, base=base):
                rows, vals = [], []
                for u in range(SCATTER_UNROLL):
                    j = jb * SCATTER_UNROLL + u
                    row = pl.multiple_of(idx_ref[base + j], TOK_ROWS)
                    rows.append(row)
                    vals.append(acc[pl.ds(row, TOK_ROWS), :]
                                + ye_ref[ei, 0, pl.ds(j, TOK_ROWS, stride=TOK_STRIDE), :])
                for row, val in zip(rows, vals):
                    acc[pl.ds(row, TOK_ROWS), :] = val
                return c

            lax.fori_loop(0, CAP // SCATTER_UNROLL, tok, 0)

    @pl.when(s >= n_scatter)
    def _():
        row0 = (s - n_scatter) * (ROW_TILE * TOK_ROWS)
        y = jnp.concatenate([acc[pl.ds(row0 + c, ROW_TILE, stride=TOK_ROWS), :] for c in range(TOK_ROWS)], axis=1)
        x = x1_ref[...] + g2_ref[0] * y
        ms = jnp.mean(x * x, axis=-1, keepdims=True)
        out = x * lax.rsqrt(ms + RMS_EPS) * fg_ref[...]

        @pl.when(grp == 0)
        def _():
            op_ref[...] = out

        @pl.when(grp == 1)
        def _():
            os_ref[...] = out


def _combine_final(idx_rows, ye, x1, mod3, final_g):
    n_scatter = N_EXPERTS // ROUTE_EXPERTS
    tiles = N_TOK // ROW_TILE
    tile_of = lambda g, s: g * tiles + jnp.maximum(s - n_scatter, 0)
    shp = jax.ShapeDtypeStruct((N_TOK, D_MODEL), F32)
    return pl.pallas_call(
        _combine_kernel,
        grid_spec=pltpu.PrefetchScalarGridSpec(
            num_scalar_prefetch=1,
            grid=(2, n_scatter + tiles),
            in_specs=[pl.BlockSpec((ROUTE_EXPERTS, 1, TOK_ROWS * TOK_STRIDE, LANES),
                                   lambda g, s, idx: (jnp.minimum(s, n_scatter - 1), g, 0, 0)),
                      pl.BlockSpec((ROW_TILE, D_MODEL), lambda g, s, idx: (tile_of(g, s), 0)),
                      pl.BlockSpec((1, 1, D_MODEL), lambda g, s, idx: (_cond_index(tile_of(g, s)), 0, 5)),
                      pl.BlockSpec((1, D_MODEL), lambda g, s, idx: (0, 0))],
            out_specs=[pl.BlockSpec((ROW_TILE, D_MODEL),
                                    lambda g, s, idx: (jnp.where(g == 0, jnp.maximum(s - n_scatter, 0), tiles - 1), 0)),
                       pl.BlockSpec((ROW_TILE, D_MODEL),
                                    lambda g, s, idx: (jnp.where(g == 1, jnp.maximum(s - n_scatter, 0), 0), 0))],
            scratch_shapes=[pltpu.VMEM((N_TOK * TOK_ROWS, LANES), F32)]),
        out_shape=[shp, shp],
        compiler_params=_cparams(2),
        name="ec_combine_final",
    )(idx_rows, ye, x1, mod3, final_g)


def kernel(x_prompt, x_sample, state_wkv, c, c_ctx, norm1_g, norm2_g, w_mod, b_mod, w_in, conv_dw, conv_b, conv_ln_g,
           conv_ln_b, rwkv_w0, rwkv_w_up, rwkv_a0, rwkv_a_up, rwkv_g_up, rwkv_k_k, rwkv_k_a, rwkv_r_k, rwkv_gn_g,
           rwkv_gn_b, w_out, router, exp_w1, exp_w3, exp_w2, final_norm_g):
    row2 = lambda a: a.reshape(1, -1)
    x = (x_prompt.reshape(N_TOK, D_MODEL), x_sample.reshape(N_TOK, D_MODEL))
    cond8 = jnp.concatenate([c_ctx[None, :], c, jnp.zeros((8 - 1 - DEC_BATCH, D_MODEL), F32)], axis=0)

    mod = _modulation(cond8, w_mod[0], b_mod)
    mod3 = mod.reshape(8, 1, 6 * D_MODEL)

    lane_head = jnp.arange(GROUP_W, dtype=jnp.int32) // HEAD
    ones_blk = (lane_head[:, None] == lane_head[None, :]).astype(BF16)

    def per_dir_rows(up):
        z = jnp.zeros_like(up[0])
        return jnp.stack([jnp.concatenate([up[0], z], axis=0), jnp.concatenate([z, up[1]], axis=0)]).astype(BF16)
    mixer_consts = (mod3, norm1_g, w_in[0].astype(BF16), (conv_dw[0], conv_b, conv_ln_g, conv_ln_b),
                    (ones_blk, rwkv_w0[0], per_dir_rows(rwkv_w_up[0]), rwkv_a0[0], per_dir_rows(rwkv_a_up[0]),
                     rwkv_g_up[0].astype(BF16), rwkv_k_k, rwkv_k_a, row2(rwkv_r_k[0])))
    groups = [_mixer_in(x[0], *mixer_consts, seg=SEQ, tile0=0),
              _mixer_in(x[1], *mixer_consts, seg=GRID_W, tile0=N_TOK // ROW_TILE)]
    conv, rkv, kap, w0, w1, b0, b1, kd0, kd1, g, bonus = zip(*groups)

    dir_ops = lambda i: ((w0[i], b0[i], kd0[i]), (w1[i], b1[i], kd1[i]))
    yp_f, yp_b, new_state = _chunk_scan(rkv[0], kap[0], dir_ops(0), None, n_batch=BATCH, t_len=SEQ, want_final=True)
    ys_f, ys_b = _chunk_scan(rkv[1], kap[1], dir_ops(1), state_wkv[:, 0], n_batch=DEC_BATCH, t_len=DEC_SEQ,
                             want_final=False)

    r_pad = jnp.pad(router[0], ((0, 0), (0, LANES - N_EXPERTS)))
    r_hi = r_pad.astype(BF16)
    r_lo = (r_pad - r_hi.astype(F32)).astype(BF16)
    x1, h2_rows, logits = _postmix((yp_f, ys_f), (yp_b, ys_b), x, conv, bonus, g, mod3, rwkv_gn_g, rwkv_gn_b, ones_blk,
                                   w_out[0].astype(BF16), norm2_g, r_hi, r_lo)

    logits_t = logits[:, :N_EXPERTS].reshape(2, N_TOK, N_EXPERTS).transpose(0, 2, 1).reshape(2 * N_EXPERTS, N_TOK)
    lane = jnp.arange(LANES, dtype=jnp.int32)
    utri = (lane[:, None] < lane[None, :]).astype(BF16)
    pos, aff, starts = _select(logits_t, utri)
    block_starts = starts[:, :N_TOK // LANES].astype(jnp.int32).reshape(-1)
    idx_c, gate_c = _compact(block_starts, pos.reshape(2 * N_EXPERTS, 1, N_TOK), aff.reshape(2 * N_EXPERTS, 1, N_TOK))
    idx_rows = (idx_c[:, :, 0].astype(jnp.int32) * TOK_ROWS).reshape(-1)

    xe = _gather(idx_rows, h2_rows.reshape(2, N_TOK * TOK_ROWS, LANES))
    ye = _experts(xe, exp_w1[0], exp_w3[0], exp_w2[0], gate_c)
    out_p, out_s = _combine_final(idx_rows, ye, x1, mod3, row2(final_norm_g))
    return (out_p.reshape(BATCH, SEQ, D_MODEL), out_s.reshape(DEC_BATCH, DEC_SEQ, D_MODEL),
            new_state.reshape(BATCH, 1, 2, N_HEADS, HEAD, HEAD))
```
